```python
import math
import jax, jax.numpy as jnp
from jax import lax
import numpy as np

D_MODEL = 4096
BATCH = 4
SEQ = 2048
DEPTH = 2

GRID_W = 64
CTX_LEN = 256
FOURIER_WIDTH = 2048
FOURIER_GROUPS = 4
FOURIER_GROUP_DIM = FOURIER_WIDTH // FOURIER_GROUPS
SSM_WIDTH = 2048
SSM_GROUP_DIM = 16
SSM_GROUPS = SSM_WIDTH // SSM_GROUP_DIM
SSM_STATE = 64
SCAN_DIRECTIONS = (False, True)
FFN_DIM = 11008
CONV_WIDTH = 3
N_MOD = 6
IN_SPLITS = (FOURIER_WIDTH, FOURIER_WIDTH + SSM_WIDTH, FOURIER_WIDTH + SSM_WIDTH + D_MODEL)
IN_COLS = FOURIER_WIDTH + SSM_WIDTH + 2 * D_MODEL
DEEPNORM_ALPHA = (2.0 * DEPTH) ** 0.25
DEEPNORM_BETA = (8.0 * DEPTH) ** -0.25
LN_EPS = 1e-6
DT_MIN = 1e-3
DT_MAX = 1e-1

kernel_name = "hybrid_fourier_s5_convffn_dit_block"


def layer_norm(x):
    xf = x.astype(jnp.float32)
    mu = jnp.mean(xf, axis=-1, keepdims=True)
    var = jnp.mean(jnp.square(xf - mu), axis=-1, keepdims=True)
    return (xf - mu) * lax.rsqrt(var + LN_EPS)


def post_norm(res, out, g, b):
    y = layer_norm(DEEPNORM_ALPHA * res + out) * g + b
    return y.astype(res.dtype)


def modulate(x, shift, scale):
    return (layer_norm(x) * (1 + scale) + shift).astype(x.dtype)


def adaln(cond, w, b, n):
    m = jax.nn.silu(cond) @ w[:, : n * D_MODEL] + b[: n * D_MODEL]
    return jnp.split(m, n, axis=-1)


def fourier_mix(u):
    bn, length, _ = u.shape
    ug = u.astype(jnp.float32).reshape(bn, length, FOURIER_GROUPS, FOURIER_GROUP_DIM)
    f = jnp.fft.fft2(ug, axes=(1, 3), norm="ortho").real
    return f.reshape(bn, length, FOURIER_WIDTH).astype(u.dtype)


def zoh_discretize(a_re, a_im, log_dt, b_re, b_im):
    lam = lax.complex(a_re.astype(jnp.float32), a_im.astype(jnp.float32))
    dt = jnp.exp(log_dt.astype(jnp.float32))[:, None]
    a_bar = jnp.exp(lam * dt)
    b_mat = lax.complex(b_re.astype(jnp.float32), b_im.astype(jnp.float32))
    b_bar = ((a_bar - 1.0) / lam)[..., None] * b_mat
    return a_bar, b_bar


def ssm_combine(left, right):
    a_l, b_l = left
    a_r, b_r = right
    return a_r * a_l, a_r * b_l + b_r


def ssm_states(u, h0s, p):
    bn, length, _ = u.shape
    u_t = jnp.moveaxis(u.astype(jnp.float32).reshape(bn, length, SSM_GROUPS, SSM_GROUP_DIM), 1, 0)
    u_c = u_t.astype(jnp.complex64)
    states = []
    for d, reverse in enumerate(SCAN_DIRECTIONS):
        a_bar, b_bar = zoh_discretize(p["ssm_a_re"][d], p["ssm_a_im"][d], p["ssm_log_dt"][d],
                                      p["ssm_b_re"][d], p["ssm_b_im"][d])
        bu = jnp.einsum("gpc,lbgc->lbgp", b_bar, u_c)
        if h0s is not None:
            first = length - 1 if reverse else 0
            bu = bu.at[first].add(a_bar * h0s[d])
        a_seq = jnp.broadcast_to(a_bar, (length, 1) + a_bar.shape)
        _, h = lax.associative_scan(ssm_combine, (a_seq, bu), axis=0, reverse=reverse)
        states.append(h)
    return states


def ssm_final_states(states):
    return (states[0][-1], states[1][0])


def ssm_readout(states, u, p):
    bn, length, _ = u.shape
    y = p["ssm_d"].astype(jnp.float32) * u.astype(jnp.float32)
    for d, h in enumerate(states):
        c_mat = lax.complex(p["ssm_c_re"][d].astype(jnp.float32), p["ssm_c_im"][d].astype(jnp.float32))
        y = y + jnp.einsum("gcp,lbgp->blgc", c_mat, h).real.reshape(bn, length, SSM_WIDTH)
    y = jax.nn.gelu(y).astype(u.dtype)
    val, gate = jnp.split(y @ p["glu_w"], 2, axis=-1)
    return val * jax.nn.sigmoid(gate)


def token_mixer(h, h0s, p):
    proj = h @ p["w_in"]
    u_f, u_s, g_f, g_s = jnp.split(proj, IN_SPLITS, axis=-1)
    y_f = fourier_mix(u_f) @ p["fourier_w"]
    states = ssm_states(u_s, h0s, p)
    y_s = ssm_readout(states, u_s, p)
    merged = jax.nn.sigmoid(g_f) * y_f + jax.nn.sigmoid(g_s) * y_s
    return merged @ p["w_out"], states


def dwconv_centred(u, w, b, axis):
    length = u.shape[axis]
    pad = [(0, 0)] * u.ndim
    pad[axis] = (CONV_WIDTH // 2, CONV_WIDTH // 2)
    up = jnp.pad(u, pad)
    out = b
    for k in range(CONV_WIDTH):
        out = out + w[k] * lax.slice_in_dim(up, k, k + length, axis=axis)
    return out


def conv_ffn(h, p, grid_rows):
    u, v = jnp.split(h @ p["ffn_w12"], 2, axis=-1)
    if grid_rows is None:
        u = dwconv_centred(u, p["ffn_conv_w"], p["ffn_conv_b"], axis=1)
    else:
        bn, length, f = u.shape
        u = dwconv_centred(u.reshape(bn, grid_rows, GRID_W, f), p["ffn_conv_w"], p["ffn_conv_b"],
                           axis=2).reshape(bn, length, f)
    return (jax.nn.gelu(u) * v) @ p["ffn_w2"]


def setup_inputs(seed: int = 0) -> dict:
    key = jax.random.key(seed)
    ks = jax.random.split(key, 26)
    f32 = jnp.float32

    def nrm(k, shape, scale):
        return scale * jax.random.normal(k, shape, f32)

    G, P, CG = SSM_GROUPS, SSM_STATE, SSM_GROUP_DIM
    a_im0 = jnp.pi * jnp.arange(P, dtype=f32)
    return {
        "x": nrm(ks[0], (BATCH, SEQ, D_MODEL), 1.0),
        "c": nrm(ks[1], (BATCH, D_MODEL), 1.0),
        "ctx": nrm(ks[2], (BATCH, CTX_LEN, D_MODEL), 1.0),
        "c_ctx": nrm(ks[3], (D_MODEL,), 1.0),
        "ada_w": nrm(ks[4], (DEPTH, D_MODEL, N_MOD * D_MODEL), 0.5 * D_MODEL ** -0.5),
        "ada_b": nrm(ks[5], (DEPTH, N_MOD * D_MODEL), 0.01),
        "w_in": nrm(ks[6], (DEPTH, D_MODEL, IN_COLS), D_MODEL ** -0.5),
        "fourier_w": nrm(ks[7], (DEPTH, FOURIER_WIDTH, D_MODEL), FOURIER_WIDTH ** -0.5),
        "ssm_a_re": -0.5 + nrm(ks[8], (DEPTH, 2, G, P), 0.01),
        "ssm_a_im": a_im0 + nrm(ks[9], (DEPTH, 2, G, P), 0.01),
        "ssm_log_dt": jax.random.uniform(ks[10], (DEPTH, 2, G), f32, math.log(DT_MIN), math.log(DT_MAX)),
        "ssm_b_re": nrm(ks[11], (DEPTH, 2, G, P, CG), (2 * CG) ** -0.5),
        "ssm_b_im": nrm(ks[12], (DEPTH, 2, G, P, CG), (2 * CG) ** -0.5),
        "ssm_c_re": nrm(ks[13], (DEPTH, 2, G, CG, P), (2 * P) ** -0.5),
        "ssm_c_im": nrm(ks[14], (DEPTH, 2, G, CG, P), (2 * P) ** -0.5),
        "ssm_d": nrm(ks[15], (DEPTH, SSM_WIDTH), 1.0),
        "glu_w": nrm(ks[16], (DEPTH, SSM_WIDTH, 2 * D_MODEL), SSM_WIDTH ** -0.5),
        "w_out": nrm(ks[17], (DEPTH, D_MODEL, D_MODEL), DEEPNORM_BETA * D_MODEL ** -0.5),
        "ln1_g": 1.0 + nrm(ks[18], (DEPTH, D_MODEL), 0.01),
        "ln1_b": nrm(ks[19], (DEPTH, D_MODEL), 0.01),
        "ffn_w12": nrm(ks[20], (DEPTH, D_MODEL, 2 * FFN_DIM), D_MODEL ** -0.5),
        "ffn_conv_w": nrm(ks[21], (DEPTH, CONV_WIDTH, FFN_DIM), CONV_WIDTH ** -0.5),
        "ffn_conv_b": nrm(ks[22], (DEPTH, FFN_DIM), 0.01),
        "ffn_w2": nrm(ks[23], (DEPTH, FFN_DIM, D_MODEL), DEEPNORM_BETA * FFN_DIM ** -0.5),
        "ln2_g": 1.0 + nrm(ks[24], (DEPTH, D_MODEL), 0.01),
        "ln2_b": nrm(ks[25], (DEPTH, D_MODEL), 0.01),
    }


def reference(x, c, ctx, c_ctx, ada_w, ada_b, w_in, fourier_w, ssm_a_re, ssm_a_im, ssm_log_dt,
              ssm_b_re, ssm_b_im, ssm_c_re, ssm_c_im, ssm_d, glu_w, w_out, ln1_g, ln1_b,
              ffn_w12, ffn_conv_w, ffn_conv_b, ffn_w2, ln2_g, ln2_b):
    ROWS = x.shape[1] // GRID_W
    cond_x = c[:, None, :]
    cond_c = c_ctx[None, None, :]
    for i in range(DEPTH):
        p = {
            "ada_w": ada_w[i], "ada_b": ada_b[i], "w_in": w_in[i], "fourier_w": fourier_w[i],
            "ssm_a_re": ssm_a_re[i], "ssm_a_im": ssm_a_im[i], "ssm_log_dt": ssm_log_dt[i],
            "ssm_b_re": ssm_b_re[i], "ssm_b_im": ssm_b_im[i], "ssm_c_re": ssm_c_re[i],
            "ssm_c_im": ssm_c_im[i], "ssm_d": ssm_d[i], "glu_w": glu_w[i], "w_out": w_out[i],
            "ffn_w12": ffn_w12[i], "ffn_conv_w": ffn_conv_w[i], "ffn_conv_b": ffn_conv_b[i],
            "ffn_w2": ffn_w2[i],
        }
        last = i == DEPTH - 1
        mx = adaln(cond_x, p["ada_w"], p["ada_b"], N_MOD)
        mc = adaln(cond_c, p["ada_w"], p["ada_b"], 2 if last else N_MOD)

        hc = modulate(ctx, mc[0], mc[1])
        if last:
            u_s_ctx = hc @ p["w_in"][:, IN_SPLITS[0]:IN_SPLITS[1]]
            ctx_states = ssm_states(u_s_ctx, None, p)
        else:
            out_c, ctx_states = token_mixer(hc, None, p)
            ctx_mid = post_norm(ctx, mc[2] * out_c, ln1_g[i], ln1_b[i])
        h0s = ssm_final_states(ctx_states)

        hx = modulate(x, mx[0], mx[1])
        out_x, _ = token_mixer(hx, h0s, p)
        x = post_norm(x, mx[2] * out_x, ln1_g[i], ln1_b[i])
        x = post_norm(x, mx[5] * conv_ffn(modulate(x, mx[3], mx[4]), p, ROWS), ln2_g[i], ln2_b[i])

        if not last:
            ctx = post_norm(ctx_mid, mc[5] * conv_ffn(modulate(ctx_mid, mc[3], mc[4]), p, None),
                            ln2_g[i], ln2_b[i])
    return x
```

```python
import functools
import math

import jax
import jax.numpy as jnp
from jax import lax
from jax.experimental import pallas as pl
from jax.experimental.pallas import tpu as pltpu

GRID_W = 64
FOURIER_GROUPS = 4
N_MOD = 6
LN_EPS = 1e-6
SSM_BLOCK_GROUPS = 16
LANES = 128
SUBLANES = 8
VMEM_LIMIT = 56 * 1024 * 1024
EPILOGUE_ROWS = 64

F32 = jnp.float32
BF16 = jnp.bfloat16


def _params(sem):
    return pltpu.CompilerParams(dimension_semantics=sem, vmem_limit_bytes=VMEM_LIMIT)


def _tile(n, pref):
    if n <= pref:
        return n
    while n % pref:
        pref //= 2
    assert pref >= LANES, (n, pref)
    return pref


def _dot(a, b):
    return jnp.dot(a, b, preferred_element_type=F32)


def _layer_norm(x):
    mu = jnp.mean(x, axis=-1, keepdims=True)
    xc = x - mu
    var = jnp.mean(xc * xc, axis=-1, keepdims=True)
    return xc * lax.rsqrt(var + LN_EPS)


def _mod_spec(mod_row, k, d):
    if mod_row is None:
        return pl.BlockSpec((None, 1, d), lambda b, *_: (b, 0, k))
    return pl.BlockSpec((None, 1, d), lambda b, *_: (mod_row, 0, k))


def _adaln_kernel(c_ref, w_ref, b_ref, o_ref):
    c = c_ref[...]
    s = (c * jax.nn.sigmoid(c)).astype(BF16)
    o_ref[...] = _dot(s, w_ref[...].astype(BF16)) + b_ref[...]


def _adaln_call(cond, ada_w, ada_b):
    depth, d, n = ada_w.shape
    r = cond.shape[0]
    tn = _tile(n, 512)
    return pl.pallas_call(
        _adaln_kernel,
        grid=(depth, n // tn),
        in_specs=[
            pl.BlockSpec((r, d), lambda l, j: (0, 0)),
            pl.BlockSpec((None, d, tn), lambda l, j: (l, 0, j)),
            pl.BlockSpec((None, 1, tn), lambda l, j: (l, 0, j)),
        ],
        out_specs=pl.BlockSpec((None, r, tn), lambda l, j: (l, 0, j)),
        out_shape=jax.ShapeDtypeStruct((depth, r, n), F32),
        compiler_params=_params(("arbitrary", "arbitrary")),
        name="adaln",
    )(cond, ada_w, ada_b.reshape(depth, 1, n))


def _ln_mod_kernel(x_ref, sh_ref, sc_ref, o_ref):
    y = _layer_norm(x_ref[...])
    o_ref[...] = (y * (1.0 + sc_ref[...]) + sh_ref[...]).astype(o_ref.dtype)


def _ln_mod_call(x, mods, mod_row, k_shift, k_scale):
    bm, lm, d = x.shape
    tm = _tile(lm, 256)
    return pl.pallas_call(
        _ln_mod_kernel,
        grid=(bm, lm // tm),
        in_specs=[
            pl.BlockSpec((None, tm, d), lambda b, i: (b, i, 0)),
            _mod_spec(mod_row, k_shift, d),
            _mod_spec(mod_row, k_scale, d),
        ],
        out_specs=pl.BlockSpec((None, tm, d), lambda b, i: (b, i, 0)),
        out_shape=jax.ShapeDtypeStruct((bm, lm, d), BF16),
        compiler_params=_params(("arbitrary", "arbitrary")),
        name="ln_mod",
    )(x, mods, mods)


def _mm_kernel(a_ref, w_ref, o_ref):
    o_ref[...] = _dot(a_ref[...], w_ref[...]).astype(o_ref.dtype)


def _mm_call(a, w, out_dtype, name):
    ba, m, k = a.shape
    bw, _, n = w.shape
    nb = max(ba, bw)
    tm = _tile(m, 1024)
    tn = _tile(n, 1024)
    a_map = (lambda b, i, j: (b, i, 0)) if ba > 1 else (lambda b, i, j: (0, i, 0))
    w_map = (lambda b, i, j: (b, 0, j)) if bw > 1 else (lambda b, i, j: (0, 0, j))
    return pl.pallas_call(
        _mm_kernel,
        grid=(nb, m // tm, n // tn),
        in_specs=[pl.BlockSpec((None, tm, k), a_map), pl.BlockSpec((None, k, tn), w_map)],
        out_specs=pl.BlockSpec((None, tm, tn), lambda b, i, j: (b, i, j)),
        out_shape=jax.ShapeDtypeStruct((nb, m, n), out_dtype),
        compiler_params=_params(("arbitrary", "arbitrary", "arbitrary")),
        name=name,
    )(a, w)


def _chan_dft_kernel(u_ref, cs_ref, o_ref, *, gd):
    r = _dot(u_ref[...], cs_ref[...])
    o_ref[0] = r[:, :gd].astype(o_ref.dtype)
    o_ref[1] = r[:, gd:].astype(o_ref.dtype)


def _chan_dft_call(proj, cs, fw):
    bs, ls, _ = proj.shape
    gd = fw // FOURIER_GROUPS
    tm = _tile(ls, 1024)
    return pl.pallas_call(
        functools.partial(_chan_dft_kernel, gd=gd),
        grid=(bs, ls // tm, FOURIER_GROUPS),
        in_specs=[
            pl.BlockSpec((None, tm, gd), lambda b, i, g: (b, i, g)),
            pl.BlockSpec((gd, 2 * gd), lambda b, i, g: (0, 0)),
        ],
        out_specs=pl.BlockSpec((None, 2, tm, gd), lambda b, i, g: (b, 0, i, g)),
        out_shape=jax.ShapeDtypeStruct((bs, 2, ls, fw), BF16),
        compiler_params=_params(("arbitrary", "arbitrary", "arbitrary")),
        name="chan_dft",
    )(proj, cs)


def _ssm_kernel(u_ref, bm_ref, cm_ref, a_ref, y_ref, s_scr, h_scr, *, chunk, nbat, half):
    d = pl.program_id(0)
    c = pl.program_id(2)
    nslab = half // LANES
    rows = SUBLANES

    @pl.when(c == 0)
    def _():
        h_scr[...] = jnp.zeros_like(h_scr)

    bm = bm_ref[...]
    for b in range(nbat):
        res = _dot(u_ref[b], bm)
        for h in range(2):
            for s in range(nslab):
                lo = h * half + s * LANES
                s_scr[s, pl.ds(2 * b + h, chunk, stride=rows), :] = res[:, lo:lo + LANES]

    ar = a_ref[0]
    ai = a_ref[1]
    nre = nslab // 2

    def body(t, carry):
        hr, hi = carry
        tt = jnp.where(d == 1, chunk - 1 - t, t)
        row = pl.multiple_of(tt * rows, rows)
        x = s_scr[:, pl.ds(row, rows), :]
        nr = ar * hr - ai * hi + x[:nre]
        ni = ar * hi + ai * hr + x[nre:]
        s_scr[:nre, pl.ds(row, rows), :] = nr
        s_scr[nre:, pl.ds(row, rows), :] = ni
        return nr, ni

    hr, hi = lax.fori_loop(0, chunk, body, (h_scr[:nre], h_scr[nre:]), unroll=8)
    h_scr[:nre] = hr
    h_scr[nre:] = hi

    cm = cm_ref[...]
    for b in range(nbat):
        acc = None
        for h in range(2):
            pieces = [s_scr[s, pl.ds(2 * b + h, chunk, stride=rows), :] for s in range(nslab)]
            hb = jnp.concatenate(pieces, axis=1).astype(BF16)
            p = _dot(hb, cm[h * half:(h + 1) * half])
            acc = p if acc is None else acc + p
        y_ref[b] = acc


def _ssm_call(u_cat, bmat, cmat, amat, chunk, n_lat, n_ctx):
    nbat, s_len, sw = u_cat.shape
    assert 2 * nbat == SUBLANES, "the scan packs 2 column halves x batch on the 8 sublanes"
    ndir, nblk, cin, cst = bmat.shape
    half = cst // 2
    nslab = half // LANES
    n_all = n_lat + n_ctx

    def chunk_idx(d, c):
        ctx_fwd = n_lat + c
        ctx_bwd = n_lat + n_ctx - 1 - c
        lat_fwd = c - n_ctx
        lat_bwd = n_lat - 1 - (c - n_ctx)
        return jnp.where(c < n_ctx, jnp.where(d == 0, ctx_fwd, ctx_bwd),
                         jnp.where(d == 0, lat_fwd, lat_bwd))

    return pl.pallas_call(
        functools.partial(_ssm_kernel, chunk=chunk, nbat=nbat, half=half),
        grid=(ndir, nblk, n_all),
        in_specs=[
            pl.BlockSpec((nbat, chunk, cin), lambda d, k, c: (0, chunk_idx(d, c), k)),
            pl.BlockSpec((None, None, cin, cst), lambda d, k, c: (d, k, 0, 0)),
            pl.BlockSpec((None, None, cst, cin), lambda d, k, c: (d, k, 0, 0)),
            pl.BlockSpec((None, None, 2, nslab // 2, SUBLANES, LANES),
                         lambda d, k, c: (d, k, 0, 0, 0, 0)),
        ],
        out_specs=pl.BlockSpec((None, nbat, chunk, cin), lambda d, k, c: (d, 0, chunk_idx(d, c), k)),
        out_shape=jax.ShapeDtypeStruct((ndir, nbat, s_len, sw), F32),
        scratch_shapes=[
            pltpu.VMEM((nslab, SUBLANES * chunk, LANES), F32),
            pltpu.VMEM((nslab, SUBLANES, LANES), F32),
        ],
        compiler_params=_params(("arbitrary", "arbitrary", "arbitrary")),
        name="ssm",
    )(u_cat, bmat, cmat, amat)


def _ssm_tables(a_re, a_im, log_dt, b_re, b_im, c_re, c_im):
    ndir, g, p = a_re.shape
    cg = b_re.shape[-1]
    gb = SSM_BLOCK_GROUPS
    nblk = g // gb
    hg = gb // 2
    lam = lax.complex(a_re.astype(F32), a_im.astype(F32))
    dt = jnp.exp(log_dt.astype(F32))[..., None]
    a_bar = jnp.exp(lam * dt)
    b_bar = ((a_bar - 1.0) / lam)[..., None] * lax.complex(b_re.astype(F32), b_im.astype(F32))
    eye = jnp.eye(hg, dtype=F32)

    bb = jnp.stack([b_bar.real, b_bar.imag], axis=0)
    bb = bb.reshape(2, ndir, nblk, 2, hg, p, cg)
    bmat = jnp.einsum("rdkhgpc,gj->dkhgcrjp", bb, eye)
    eye2 = jnp.eye(2, dtype=F32)
    bmat = jnp.einsum("dkhgcrjp,hi->dkhgcirjp", bmat, eye2)
    bmat = bmat.reshape(ndir, nblk, gb * cg, 2 * 2 * hg * p).astype(BF16)

    cc = jnp.stack([c_re.astype(F32), -c_im.astype(F32)], axis=0)
    cc = cc.reshape(2, ndir, nblk, 2, hg, cg, p)
    cmat = jnp.einsum("rdkhgcp,gj->dkhrjpgc", cc, eye)
    cmat = jnp.einsum("dkhrjpgc,hi->dkhrjpigc", cmat, eye2)
    cmat = cmat.reshape(ndir, nblk, 2 * 2 * hg * p, gb * cg).astype(BF16)

    aa = jnp.stack([a_bar.real, a_bar.imag], axis=2)
    aa = aa.reshape(ndir, nblk, 2, hg, 2, p)
    aa = jnp.transpose(aa, (0, 1, 4, 2, 3, 5)).reshape(ndir, nblk, 2, 2, hg * p)
    nslab = hg * p // LANES
    aa = aa.reshape(ndir, nblk, 2, 2, nslab, LANES)
    aa = jnp.transpose(aa, (0, 1, 2, 4, 3, 5))
    aa = jnp.tile(aa, (1, 1, 1, 1, SUBLANES // 2, 1))
    return bmat, cmat, aa


def _glu_kernel(u_ref, yf_ref, yb_ref, d_ref, wv_ref, wg_ref, gf_ref, gs_ref, fo_ref, o_ref, a_scr):
    @pl.when(pl.program_id(2) == 0)
    def _():
        y = d_ref[...] * u_ref[...].astype(F32) + yf_ref[...] + yb_ref[...]
        a_scr[...] = jax.nn.gelu(y).astype(a_scr.dtype)

    a = a_scr[...]
    y_s = _dot(a, wv_ref[...]) * jax.nn.sigmoid(_dot(a, wg_ref[...]))
    merged = (jax.nn.sigmoid(gf_ref[...].astype(F32)) * fo_ref[...].astype(F32)
              + jax.nn.sigmoid(gs_ref[...].astype(F32)) * y_s)
    o_ref[...] = merged.astype(o_ref.dtype)


def _glu_call(proj, ydir, row_off, ssm_d, glu_w, y_f, fw, sw, d):
    bs, ls, _ = proj.shape
    tm = _tile(ls, 512)
    tn = _tile(d, 512)
    assert fw % sw == 0 and (fw + sw) % tn == 0 and row_off % tm == 0
    u_blk = fw // sw
    gf_blk = (fw + sw) // tn
    gs_blk = (fw + sw + d) // tn
    r_blk = row_off // tm
    nj = d // tn
    return pl.pallas_call(
        _glu_kernel,
        grid=(bs, ls // tm, nj),
        in_specs=[
            pl.BlockSpec((None, tm, sw), lambda b, i, j: (b, i, u_blk)),
            pl.BlockSpec((None, None, tm, sw), lambda b, i, j: (0, b, i + r_blk, 0)),
            pl.BlockSpec((None, None, tm, sw), lambda b, i, j: (1, b, i + r_blk, 0)),
            pl.BlockSpec((1, sw), lambda b, i, j: (0, 0)),
            pl.BlockSpec((sw, tn), lambda b, i, j: (0, j)),
            pl.BlockSpec((sw, tn), lambda b, i, j: (0, nj + j)),
            pl.BlockSpec((None, tm, tn), lambda b, i, j: (b, i, gf_blk + j)),
            pl.BlockSpec((None, tm, tn), lambda b, i, j: (b, i, gs_blk + j)),
            pl.BlockSpec((None, tm, tn), lambda b, i, j: (b, i, j)),
        ],
        out_specs=pl.BlockSpec((None, tm, tn), lambda b, i, j: (b, i, j)),
        out_shape=jax.ShapeDtypeStruct((bs, ls, d), BF16),
        scratch_shapes=[pltpu.VMEM((tm, sw), BF16)],
        compiler_params=_params(("arbitrary", "arbitrary", "arbitrary")),
        name="glu_merge",
    )(proj, ydir, ydir, ssm_d, glu_w, glu_w, proj, proj, y_f)


def _mm_postnorm_kernel(a_ref, w_ref, res_ref, gate_ref, g_ref, b_ref, o_ref, *, alpha, nk):
    k = pl.program_id(2)

    @pl.when(k == 0)
    def _():
        o_ref[...] = jnp.zeros_like(o_ref)

    o_ref[...] += _dot(a_ref[...], w_ref[...])

    @pl.when(k == nk - 1)
    def _():
        tm = o_ref.shape[0]
        rc = min(tm, EPILOGUE_ROWS)
        for r0 in range(0, tm, rc):
            z = alpha * res_ref[r0:r0 + rc] + gate_ref[...] * o_ref[r0:r0 + rc]
            o_ref[r0:r0 + rc] = _layer_norm(z) * g_ref[...] + b_ref[...]


def _mm_postnorm_call(a, w, res, mods, mod_row, k_gate, ln_g, ln_b, alpha, name):
    bm, lm, kdim = a.shape
    d = w.shape[1]
    tm = _tile(lm, 512)
    tk = _tile(kdim, 512)
    nk = kdim // tk
    return pl.pallas_call(
        functools.partial(_mm_postnorm_kernel, alpha=alpha, nk=nk),
        grid=(bm, lm // tm, nk),
        in_specs=[
            pl.BlockSpec((None, tm, tk), lambda b, i, k: (b, i, k)),
            pl.BlockSpec((tk, d), lambda b, i, k: (k, 0)),
            pl.BlockSpec((None, tm, d), lambda b, i, k: (b, i, 0), pipeline_mode=pl.Buffered(1)),
            _mod_spec(mod_row, k_gate, d),
            pl.BlockSpec((1, d), lambda b, i, k: (0, 0)),
            pl.BlockSpec((1, d), lambda b, i, k: (0, 0)),
        ],
        out_specs=pl.BlockSpec((None, tm, d), lambda b, i, k: (b, i, 0)),
        out_shape=jax.ShapeDtypeStruct((bm, lm, d), F32),
        compiler_params=_params(("arbitrary", "arbitrary", "arbitrary")),
        name=name,
    )(a, w, res, mods, ln_g, ln_b)


def _ffn1_kernel(h_ref, wu_ref, wv_ref, cw_ref, cb_ref, o_ref, *, period):
    h = h_ref[...]
    u = _dot(h, wu_ref[...])
    v = _dot(h, wv_ref[...])
    tm = u.shape[0]
    pos = lax.broadcasted_iota(jnp.int32, u.shape, 0) % period
    prev = jnp.where(pos == 0, 0.0, pltpu.roll(u, 1, axis=0))
    nxt = jnp.where(pos == period - 1, 0.0, pltpu.roll(u, tm - 1, axis=0))
    cw = cw_ref[...]
    conv = cb_ref[...] + cw[0:1] * prev + cw[1:2] * u + cw[2:3] * nxt
    o_ref[...] = (jax.nn.gelu(conv) * v).astype(o_ref.dtype)


def _ffn1_call(h, w_u, w_v, conv_w, conv_b, period):
    bm, lm, d = h.shape
    fp = w_u.shape[1]
    tm = _tile(lm, 1024)
    tn = _tile(fp, 512)
    assert tm % period == 0
    return pl.pallas_call(
        functools.partial(_ffn1_kernel, period=period),
        grid=(bm, lm // tm, fp // tn),
        in_specs=[
            pl.BlockSpec((None, tm, d), lambda b, i, j: (b, i, 0)),
            pl.BlockSpec((d, tn), lambda b, i, j: (0, j)),
            pl.BlockSpec((d, tn), lambda b, i, j: (0, j)),
            pl.BlockSpec((3, tn), lambda b, i, j: (0, j)),
            pl.BlockSpec((1, tn), lambda b, i, j: (0, j)),
        ],
        out_specs=pl.BlockSpec((None, tm, tn), lambda b, i, j: (b, i, j)),
        out_shape=jax.ShapeDtypeStruct((bm, lm, fp), BF16),
        compiler_params=_params(("arbitrary", "arbitrary", "arbitrary")),
        name="ffn1",
    )(h, w_u, w_v, conv_w, conv_b)


def _dft_cos_sin(n):
    idx = jnp.arange(n, dtype=jnp.int32)
    ang = (2.0 * math.pi / n) * ((idx[:, None] * idx[None, :]) % n).astype(F32)
    scale = 1.0 / math.sqrt(n)
    return jnp.cos(ang) * scale, jnp.sin(ang) * scale


def _pad_cols(w, n):
    return jnp.pad(w, ((0, 0), (0, n - w.shape[1])))


def kernel(x, c, ctx, c_ctx, ada_w, ada_b, w_in, fourier_w, ssm_a_re, ssm_a_im, ssm_log_dt,
           ssm_b_re, ssm_b_im, ssm_c_re, ssm_c_im, ssm_d, glu_w, w_out, ln1_g, ln1_b,
           ffn_w12, ffn_conv_w, ffn_conv_b, ffn_w2, ln2_g, ln2_b):
    nb, seq, d = x.shape
    clen = ctx.shape[1]
    depth = ada_w.shape[0]
    fw = fourier_w.shape[1]
    sw = ssm_d.shape[1]
    ffn = ffn_conv_b.shape[1]
    alpha = (2.0 * depth) ** 0.25
    gd = fw // FOURIER_GROUPS
    chunk = clen
    assert seq % chunk == 0 and seq % GRID_W == 0
    n_lat = seq // chunk
    ffn_pad = -(-ffn // 512) * 512

    ctx_row = nb
    n_rows = -(-(nb + 1) // SUBLANES) * SUBLANES
    cond = jnp.zeros((n_rows, d), F32).at[:nb].set(c).at[ctx_row].set(c_ctx)
    mods_all = _adaln_call(cond, ada_w, ada_b)

    cos_c, sin_c = _dft_cos_sin(gd)
    cs_chan = jnp.concatenate([cos_c, sin_c], axis=1).astype(BF16)

    def pos_dft_matrix(n):
        cos_l, sin_l = _dft_cos_sin(n)
        return jnp.concatenate([cos_l, -sin_l], axis=1).astype(BF16)[None]

    csl_x = pos_dft_matrix(seq)
    csl_c = pos_dft_matrix(clen)

    def token_mix_tail(proj, csl, ydir, row_off, lw):
        ls = proj.shape[1]
        ab = _chan_dft_call(proj, cs_chan, fw)
        f = _mm_call(csl, ab.reshape(nb, 2 * ls, fw), BF16, "pos_dft")
        y_f = _mm_call(f.reshape(1, nb * ls, fw), lw["fourier_w"], BF16, "fourier_out")
        return _glu_call(proj, ydir, row_off, lw["ssm_d"], lw["glu_w"],
                         y_f.reshape(nb, ls, d), fw, sw, d)

    xs = x
    cs = ctx.reshape(1, nb * clen, d)
    for i in range(depth):
        last = i == depth - 1
        mods = mods_all[i].reshape(n_rows, 1, N_MOD * d)
        w_in_b = w_in[i].astype(BF16)
        lw = {
            "fourier_w": fourier_w[i].astype(BF16)[None],
            "ssm_d": ssm_d[i].reshape(1, sw),
            "glu_w": glu_w[i].astype(BF16),
        }
        w_out_b = w_out[i].astype(BF16)
        w_u = _pad_cols(ffn_w12[i, :, :ffn], ffn_pad).astype(BF16)
        w_v = _pad_cols(ffn_w12[i, :, ffn:], ffn_pad).astype(BF16)
        w_2 = jnp.pad(ffn_w2[i], ((0, ffn_pad - ffn), (0, 0))).astype(BF16)
        conv_w = _pad_cols(ffn_conv_w[i], ffn_pad)
        conv_b = _pad_cols(ffn_conv_b[i].reshape(1, ffn), ffn_pad)
        g1, b1 = ln1_g[i].reshape(1, d), ln1_b[i].reshape(1, d)
        g2, b2 = ln2_g[i].reshape(1, d), ln2_b[i].reshape(1, d)
        bmat, cmat, amat = _ssm_tables(ssm_a_re[i], ssm_a_im[i], ssm_log_dt[i], ssm_b_re[i],
                                       ssm_b_im[i], ssm_c_re[i], ssm_c_im[i])

        hx = _ln_mod_call(xs, mods, None, 0, 1)
        proj_x = _mm_call(hx.reshape(1, nb * seq, d), w_in_b[None], BF16, "w_in").reshape(nb, seq, -1)
        hc = _ln_mod_call(cs, mods, ctx_row, 0, 1)
        if last:
            us_c = _mm_call(hc, w_in_b[None, :, fw:fw + sw], BF16, "w_in_ctx_ssm")
            proj_c = None
        else:
            proj_c = _mm_call(hc, w_in_b[None], BF16, "w_in").reshape(nb, clen, -1)
            us_c = proj_c[:, :, fw:fw + sw]
        u_cat = jnp.concatenate([proj_x[:, :, fw:fw + sw], us_c.reshape(nb, clen, sw)], axis=1)
        ydir = _ssm_call(u_cat, bmat, cmat, amat, chunk, n_lat, 1)

        merged = token_mix_tail(proj_x, csl_x, ydir, 0, lw)
        x1 = _mm_postnorm_call(merged, w_out_b, xs, mods, None, 2, g1, b1, alpha, "w_out_norm")
        h2 = _ln_mod_call(x1, mods, None, 3, 4)
        act = _ffn1_call(h2, w_u, w_v, conv_w, conv_b, GRID_W)
        xs = _mm_postnorm_call(act, w_2, x1, mods, None, 5, g2, b2, alpha, "ffn2_norm")

        if not last:
            merged_c = token_mix_tail(proj_c, csl_c, ydir, seq, lw).reshape(1, nb * clen, d)
            c1 = _mm_postnorm_call(merged_c, w_out_b, cs, mods, ctx_row, 2, g1, b1, alpha, "w_out_norm")
            hc2 = _ln_mod_call(c1, mods, ctx_row, 3, 4)
            act_c = _ffn1_call(hc2, w_u, w_v, conv_w, conv_b, clen)
            cs = _mm_postnorm_call(act_c, w_2, c1, mods, ctx_row, 5, g2, b2, alpha, "ffn2_norm")
    return xs
```

```python
import functools
import math

import jax
import jax.numpy as jnp
from jax import lax
from jax.experimental import pallas as pl
from jax.experimental.pallas import tpu as pltpu

GRID_W = 64
FOURIER_GROUPS = 4
N_MOD = 6
LN_EPS = 1e-6
SSM_BLOCK_GROUPS = 16
LANES = 128
SUBLANES = 8
VMEM_LIMIT = 56 * 1024 * 1024

F32 = jnp.float32
BF16 = jnp.bfloat16


def _params(sem):
    return pltpu.CompilerParams(dimension_semantics=sem, vmem_limit_bytes=VMEM_LIMIT)


def _tile(n, pref):
    if n <= pref:
        return n
    while n % pref:
        pref //= 2
    assert pref >= LANES, (n, pref)
    return pref


def _lane_tile(n, cap):
    for t in range(cap - cap % LANES, 0, -LANES):
        if n % t == 0:
            return t
    return n


def _dot(a, b):
    return jnp.dot(a, b, preferred_element_type=F32)


def _layer_norm(x):
    mu = jnp.mean(x, axis=-1, keepdims=True)
    xc = x - mu
    var = jnp.mean(xc * xc, axis=-1, keepdims=True)
    return xc * lax.rsqrt(var + LN_EPS)


def _mod_spec(mod_row, k, d):
    if mod_row is None:
        return pl.BlockSpec((None, 1, d), lambda b, *_: (b, 0, k))
    return pl.BlockSpec((None, 1, d), lambda b, *_: (mod_row, 0, k))


def _cast_kernel(w_ref, o_ref):
    o_ref[...] = w_ref[...].astype(o_ref.dtype)


def _cast_call(w, layer):
    _, r, c = w.shape
    tr = _tile(r, 512)
    tc = _lane_tile(c, 6144)
    return pl.pallas_call(
        _cast_kernel,
        grid=(r // tr, c // tc),
        in_specs=[pl.BlockSpec((None, tr, tc), lambda i, j: (layer, i, j))],
        out_specs=pl.BlockSpec((tr, tc), lambda i, j: (i, j)),
        out_shape=jax.ShapeDtypeStruct((r, c), BF16),
        compiler_params=_params(("arbitrary", "arbitrary")),
        name="cast_w",
    )(w)


def _adaln_kernel(c_ref, w_ref, b_ref, o_ref):
    c = c_ref[...]
    s = (c * jax.nn.sigmoid(c)).astype(BF16)
    o_ref[...] = _dot(s, w_ref[...].astype(BF16)) + b_ref[...]


def _adaln_call(cond, ada_w, ada_b):
    depth, d, n = ada_w.shape
    r = cond.shape[0]
    tn = _tile(n, 512)
    return pl.pallas_call(
        _adaln_kernel,
        grid=(depth, n // tn),
        in_specs=[
            pl.BlockSpec((r, d), lambda l, j: (0, 0)),
            pl.BlockSpec((None, d, tn), lambda l, j: (l, 0, j)),
            pl.BlockSpec((None, 1, tn), lambda l, j: (l, 0, j)),
        ],
        out_specs=pl.BlockSpec((None, r, tn), lambda l, j: (l, 0, j)),
        out_shape=jax.ShapeDtypeStruct((depth, r, n), F32),
        compiler_params=_params(("arbitrary", "arbitrary")),
        name="adaln",
    )(cond, ada_w, ada_b.reshape(depth, 1, n))


def _ln_mod_kernel(x_ref, sh_ref, sc_ref, o_ref):
    y = _layer_norm(x_ref[...])
    o_ref[...] = (y * (1.0 + sc_ref[...]) + sh_ref[...]).astype(o_ref.dtype)


def _ln_mod_call(x, mods, mod_row, k_shift, k_scale):
    bm, lm, d = x.shape
    tm = _tile(lm, 256)
    return pl.pallas_call(
        _ln_mod_kernel,
        grid=(bm, lm // tm),
        in_specs=[
            pl.BlockSpec((None, tm, d), lambda b, i: (b, i, 0)),
            _mod_spec(mod_row, k_shift, d),
            _mod_spec(mod_row, k_scale, d),
        ],
        out_specs=pl.BlockSpec((None, tm, d), lambda b, i: (b, i, 0)),
        out_shape=jax.ShapeDtypeStruct((bm, lm, d), BF16),
        compiler_params=_params(("arbitrary", "arbitrary")),
        name="ln_mod",
    )(x, mods, mods)


def _postnorm_kernel(res_ref, pre_ref, gate_ref, g_ref, b_ref, *rest, alpha, with_next):
    z = alpha * res_ref[...] + gate_ref[...] * pre_ref[...].astype(F32)
    x_new = _layer_norm(z) * g_ref[...] + b_ref[...]
    if with_next:
        sh_ref, sc_ref, o_ref, h_ref = rest
        o_ref[...] = x_new
        h_ref[...] = (_layer_norm(x_new) * (1.0 + sc_ref[...]) + sh_ref[...]).astype(h_ref.dtype)
    else:
        (o_ref,) = rest
        o_ref[...] = x_new


def _postnorm_call(res, pre, mods, mod_row, k_gate, ln_g, ln_b, alpha, next_mod=None):
    bm, lm, d = res.shape
    tm = _tile(lm, 256)
    row = pl.BlockSpec((None, tm, d), lambda b, i: (b, i, 0))
    vec = pl.BlockSpec((1, d), lambda b, i: (0, 0))
    in_specs = [row, row, _mod_spec(mod_row, k_gate, d), vec, vec]
    args = [res, pre, mods, ln_g, ln_b]
    out_specs = [row]
    out_shape = [jax.ShapeDtypeStruct((bm, lm, d), F32)]
    if next_mod is not None:
        nmods, k_shift, k_scale = next_mod
        in_specs += [_mod_spec(mod_row, k_shift, d), _mod_spec(mod_row, k_scale, d)]
        args += [nmods, nmods]
        out_specs.append(row)
        out_shape.append(jax.ShapeDtypeStruct((bm, lm, d), BF16))
    out = pl.pallas_call(
        functools.partial(_postnorm_kernel, alpha=alpha, with_next=next_mod is not None),
        grid=(bm, lm // tm),
        in_specs=in_specs,
        out_specs=out_specs,
        out_shape=out_shape,
        compiler_params=_params(("arbitrary", "arbitrary")),
        name="postnorm",
    )(*args)
    return (out[0], out[1]) if next_mod is not None else (out[0], None)


def _mm_kernel(a_ref, w_ref, o_ref):
    o_ref[...] = _dot(a_ref[...], w_ref[...]).astype(o_ref.dtype)


def _mm_call(a, w, out_dtype, name, tm_pref=1024, tn_pref=1024):
    ba, m, k = a.shape
    bw, _, n = w.shape
    nb = max(ba, bw)
    tm = _tile(m, tm_pref)
    tn = _tile(n, tn_pref)
    a_map = (lambda b, i, j: (b, i, 0)) if ba > 1 else (lambda b, i, j: (0, i, 0))
    w_map = (lambda b, i, j: (b, 0, j)) if bw > 1 else (lambda b, i, j: (0, 0, j))
    return pl.pallas_call(
        _mm_kernel,
        grid=(nb, m // tm, n // tn),
        in_specs=[pl.BlockSpec((None, tm, k), a_map), pl.BlockSpec((None, k, tn), w_map)],
        out_specs=pl.BlockSpec((None, tm, tn), lambda b, i, j: (b, i, j)),
        out_shape=jax.ShapeDtypeStruct((nb, m, n), out_dtype),
        compiler_params=_params(("arbitrary", "arbitrary", "arbitrary")),
        name=name,
    )(a, w)


def _chan_dft_kernel(u_ref, cs_ref, o_ref, *, gd):
    r = _dot(u_ref[...], cs_ref[...])
    o_ref[0] = r[:, :gd].astype(o_ref.dtype)
    o_ref[1] = r[:, gd:].astype(o_ref.dtype)


def _chan_dft_call(proj, cs, fw):
    bs, ls, _ = proj.shape
    gd = fw // FOURIER_GROUPS
    tm = _tile(ls, 1024)
    return pl.pallas_call(
        functools.partial(_chan_dft_kernel, gd=gd),
        grid=(bs, ls // tm, FOURIER_GROUPS),
        in_specs=[
            pl.BlockSpec((None, tm, gd), lambda b, i, g: (b, i, g)),
            pl.BlockSpec((gd, 2 * gd), lambda b, i, g: (0, 0)),
        ],
        out_specs=pl.BlockSpec((None, 2, tm, gd), lambda b, i, g: (b, 0, i, g)),
        out_shape=jax.ShapeDtypeStruct((bs, 2, ls, fw), BF16),
        compiler_params=_params(("arbitrary", "arbitrary", "arbitrary")),
        name="chan_dft",
    )(proj, cs)


def _ssm_kernel(uxf_ref, uxb_ref, ucf_ref, ucb_ref, bm_ref, cm_ref, a_ref, yf_ref, yb_ref,
                u_scr, s_scr, h_scr, *, chunk, nbat, half, n_ctx):
    c = pl.program_id(1)
    nslab = half // LANES
    nre = nslab // 2
    rows = SUBLANES

    @pl.when(c == 0)
    def _():
        h_scr[...] = jnp.zeros_like(h_scr)

    @pl.when(c < n_ctx)
    def _():
        u_scr[0] = ucf_ref[...]
        u_scr[1] = ucb_ref[...]

    @pl.when(c >= n_ctx)
    def _():
        u_scr[0] = uxf_ref[...]
        u_scr[1] = uxb_ref[...]

    for d in range(2):
        bm = bm_ref[d]
        for b in range(nbat):
            res = _dot(u_scr[d, b], bm)
            for h in range(2):
                for s in range(nslab):
                    lo = h * half + s * LANES
                    s_scr[d, s, pl.ds(2 * b + h, chunk, stride=rows), :] = res[:, lo:lo + LANES]

    arf, aif = a_ref[0, 0], a_ref[0, 1]
    arb, aib = a_ref[1, 0], a_ref[1, 1]

    def body(t, carry):
        hrf, hif, hrb, hib = carry
        rowf = pl.multiple_of(t * rows, rows)
        rowb = pl.multiple_of((chunk - 1 - t) * rows, rows)
        xf = s_scr[0, :, pl.ds(rowf, rows), :]
        xb = s_scr[1, :, pl.ds(rowb, rows), :]
        nrf = arf * hrf - aif * hif + xf[:nre]
        nif = arf * hif + aif * hrf + xf[nre:]
        nrb = arb * hrb - aib * hib + xb[:nre]
        nib = arb * hib + aib * hrb + xb[nre:]
        s_scr[0, :nre, pl.ds(rowf, rows), :] = nrf
        s_scr[0, nre:, pl.ds(rowf, rows), :] = nif
        s_scr[1, :nre, pl.ds(rowb, rows), :] = nrb
        s_scr[1, nre:, pl.ds(rowb, rows), :] = nib
        return nrf, nif, nrb, nib

    init = (h_scr[0, :nre], h_scr[0, nre:], h_scr[1, :nre], h_scr[1, nre:])
    hrf, hif, hrb, hib = lax.fori_loop(0, chunk, body, init, unroll=8)
    h_scr[0, :nre] = hrf
    h_scr[0, nre:] = hif
    h_scr[1, :nre] = hrb
    h_scr[1, nre:] = hib

    for d, y_ref in enumerate((yf_ref, yb_ref)):
        cm = cm_ref[d]
        for b in range(nbat):
            acc = None
            for h in range(2):
                pieces = [s_scr[d, s, pl.ds(2 * b + h, chunk, stride=rows), :] for s in range(nslab)]
                hb = jnp.concatenate(pieces, axis=1).astype(BF16)
                p = _dot(hb, cm[h * half:(h + 1) * half])
                acc = p if acc is None else acc + p
            y_ref[b] = acc.astype(y_ref.dtype)


def _ssm_call(ux, ux_col, uc, uc_col, bmat, cmat, amat, chunk):
    nbat, seq, _ = ux.shape
    clen = uc.shape[1]
    assert 2 * nbat == SUBLANES, "the scan packs 2 column halves x batch on the 8 sublanes"
    ndir, nblk, cin, cst = bmat.shape
    sw = nblk * cin
    half = cst // 2
    nslab = half // LANES
    n_lat, n_ctx = seq // chunk, clen // chunk
    n_all = n_lat + n_ctx
    assert ux_col % cin == 0 and uc_col % cin == 0
    xo, co = ux_col // cin, uc_col // cin

    def lat_f(c):
        return jnp.maximum(c - n_ctx, 0)

    def lat_b(c):
        return n_lat - 1 - jnp.maximum(c - n_ctx, 0)

    def ctx_f(c):
        return jnp.minimum(c, n_ctx - 1)

    def ctx_b(c):
        return jnp.maximum(n_ctx - 1 - c, 0)

    def out_f(c):
        return jnp.where(c < n_ctx, n_lat + c, c - n_ctx)

    def out_b(c):
        return jnp.where(c < n_ctx, n_lat + n_ctx - 1 - c, n_lat - 1 - (c - n_ctx))

    ublk = (nbat, chunk, cin)
    return pl.pallas_call(
        functools.partial(_ssm_kernel, chunk=chunk, nbat=nbat, half=half, n_ctx=n_ctx),
        grid=(nblk, n_all),
        in_specs=[
            pl.BlockSpec(ublk, lambda k, c: (0, lat_f(c), xo + k)),
            pl.BlockSpec(ublk, lambda k, c: (0, lat_b(c), xo + k)),
            pl.BlockSpec(ublk, lambda k, c: (0, ctx_f(c), co + k)),
            pl.BlockSpec(ublk, lambda k, c: (0, ctx_b(c), co + k)),
            pl.BlockSpec((ndir, None, cin, cst), lambda k, c: (0, k, 0, 0)),
            pl.BlockSpec((ndir, None, cst, cin), lambda k, c: (0, k, 0, 0)),
            pl.BlockSpec((ndir, None, 2, nslab // 2, SUBLANES, LANES), lambda k, c: (0, k, 0, 0, 0, 0)),
        ],
        out_specs=[
            pl.BlockSpec(ublk, lambda k, c: (0, out_f(c), k)),
            pl.BlockSpec(ublk, lambda k, c: (0, out_b(c), k)),
        ],
        out_shape=[jax.ShapeDtypeStruct((nbat, seq + clen, sw), BF16)] * 2,
        scratch_shapes=[
            pltpu.VMEM((ndir, nbat, chunk, cin), BF16),
            pltpu.VMEM((ndir, nslab, SUBLANES * chunk, LANES), F32),
            pltpu.VMEM((ndir, nslab, SUBLANES, LANES), F32),
        ],
        compiler_params=_params(("arbitrary", "arbitrary")),
        name="ssm",
    )(ux, ux, uc, uc, bmat, cmat, amat)


def _ssm_tables(a_re, a_im, log_dt, b_re, b_im, c_re, c_im):
    ndir, g, p = a_re.shape
    cg = b_re.shape[-1]
    gb = SSM_BLOCK_GROUPS
    nblk = g // gb
    hg = gb // 2
    lr, li = a_re.astype(F32), a_im.astype(F32)
    dt = jnp.exp(log_dt.astype(F32))[..., None]
    zr, zi = lr * dt, li * dt
    a_bar_r = jnp.exp(zr) * jnp.cos(zi)
    a_bar_i = jnp.exp(zr) * jnp.sin(zi)
    em1_r = jnp.expm1(zr) * jnp.cos(zi) - 2.0 * jnp.square(jnp.sin(0.5 * zi))
    den = lr * lr + li * li
    cf_r = ((em1_r * lr + a_bar_i * li) / den)[..., None]
    cf_i = ((a_bar_i * lr - em1_r * li) / den)[..., None]
    br, bi = b_re.astype(F32), b_im.astype(F32)
    b_bar_r = cf_r * br - cf_i * bi
    b_bar_i = cf_r * bi + cf_i * br
    eye = jnp.eye(hg, dtype=F32)

    bb = jnp.stack([b_bar_r, b_bar_i], axis=0)
    bb = bb.reshape(2, ndir, nblk, 2, hg, p, cg)
    bmat = jnp.einsum("rdkhgpc,gj->dkhgcrjp", bb, eye)
    eye2 = jnp.eye(2, dtype=F32)
    bmat = jnp.einsum("dkhgcrjp,hi->dkhgcirjp", bmat, eye2)
    bmat = bmat.reshape(ndir, nblk, gb * cg, 2 * 2 * hg * p).astype(BF16)

    cc = jnp.stack([c_re.astype(F32), -c_im.astype(F32)], axis=0)
    cc = cc.reshape(2, ndir, nblk, 2, hg, cg, p)
    cmat = jnp.einsum("rdkhgcp,gj->dkhrjpgc", cc, eye)
    cmat = jnp.einsum("dkhrjpgc,hi->dkhrjpigc", cmat, eye2)
    cmat = cmat.reshape(ndir, nblk, 2 * 2 * hg * p, gb * cg).astype(BF16)

    aa = jnp.stack([a_bar_r, a_bar_i], axis=2)
    aa = aa.reshape(ndir, nblk, 2, hg, 2, p)
    aa = jnp.transpose(aa, (0, 1, 4, 2, 3, 5)).reshape(ndir, nblk, 2, 2, hg * p)
    nslab = hg * p // LANES
    aa = aa.reshape(ndir, nblk, 2, 2, nslab, LANES)
    aa = jnp.transpose(aa, (0, 1, 2, 4, 3, 5))
    aa = jnp.tile(aa, (1, 1, 1, 1, SUBLANES // 2, 1))
    return bmat, cmat, aa


def _glu_kernel(u_ref, yf_ref, yb_ref, d_ref, wv_ref, wg_ref, gf_ref, gs_ref, fo_ref, o_ref, a_scr):
    @pl.when(pl.program_id(2) == 0)
    def _():
        y = (d_ref[...] * u_ref[...].astype(F32) + yf_ref[...].astype(F32)
             + yb_ref[...].astype(F32))
        a_scr[...] = jax.nn.gelu(y).astype(a_scr.dtype)

    a = a_scr[...]
    y_s = _dot(a, wv_ref[...]) * jax.nn.sigmoid(_dot(a, wg_ref[...]))
    merged = (jax.nn.sigmoid(gf_ref[...].astype(F32)) * fo_ref[...].astype(F32)
              + jax.nn.sigmoid(gs_ref[...].astype(F32)) * y_s)
    o_ref[...] = merged.astype(o_ref.dtype)


def _glu_call(proj, yf, yb, row_off, ssm_d, glu_w, y_f, fw, sw, d):
    bs, ls, _ = proj.shape
    tm = _tile(ls, 512)
    tn = _tile(d, 1024)
    assert fw % sw == 0 and (fw + sw) % tn == 0 and row_off % tm == 0
    u_blk = fw // sw
    gf_blk = (fw + sw) // tn
    gs_blk = (fw + sw + d) // tn
    r_blk = row_off // tm
    nj = d // tn
    return pl.pallas_call(
        _glu_kernel,
        grid=(bs, ls // tm, nj),
        in_specs=[
            pl.BlockSpec((None, tm, sw), lambda b, i, j: (b, i, u_blk)),
            pl.BlockSpec((None, tm, sw), lambda b, i, j: (b, i + r_blk, 0)),
            pl.BlockSpec((None, tm, sw), lambda b, i, j: (b, i + r_blk, 0)),
            pl.BlockSpec((1, sw), lambda b, i, j: (0, 0)),
            pl.BlockSpec((sw, tn), lambda b, i, j: (0, j)),
            pl.BlockSpec((sw, tn), lambda b, i, j: (0, nj + j)),
            pl.BlockSpec((None, tm, tn), lambda b, i, j: (b, i, gf_blk + j)),
            pl.BlockSpec((None, tm, tn), lambda b, i, j: (b, i, gs_blk + j)),
            pl.BlockSpec((None, tm, tn), lambda b, i, j: (b, i, j)),
        ],
        out_specs=pl.BlockSpec((None, tm, tn), lambda b, i, j: (b, i, j)),
        out_shape=jax.ShapeDtypeStruct((bs, ls, d), BF16),
        scratch_shapes=[pltpu.VMEM((tm, sw), BF16)],
        compiler_params=_params(("arbitrary", "arbitrary", "arbitrary")),
        name="glu_merge",
    )(proj, yf, yb, ssm_d, glu_w, glu_w, proj, proj, y_f)


def _ffn1_kernel(h_ref, wu_ref, wv_ref, cw_ref, cb_ref, o_ref, *, period):
    h = h_ref[...]
    u = _dot(h, wu_ref[...])
    v = _dot(h, wv_ref[...])
    tm = u.shape[0]
    pos = lax.broadcasted_iota(jnp.int32, u.shape, 0) % period
    prev = jnp.where(pos == 0, 0.0, pltpu.roll(u, 1, axis=0))
    nxt = jnp.where(pos == period - 1, 0.0, pltpu.roll(u, tm - 1, axis=0))
    cw = cw_ref[...]
    conv = cb_ref[...] + cw[0:1] * prev + cw[1:2] * u + cw[2:3] * nxt
    o_ref[...] = (jax.nn.gelu(conv) * v).astype(o_ref.dtype)


def _ffn1_call(h, w12, conv_w, conv_b, period):
    bm, lm, d = h.shape
    ffn = w12.shape[1] // 2
    tm = _tile(lm, 1024)
    tn = _tile(ffn, 512)
    nj = ffn // tn
    assert tm % period == 0
    return pl.pallas_call(
        functools.partial(_ffn1_kernel, period=period),
        grid=(bm, lm // tm, nj),
        in_specs=[
            pl.BlockSpec((None, tm, d), lambda b, i, j: (b, i, 0)),
            pl.BlockSpec((d, tn), lambda b, i, j: (0, j)),
            pl.BlockSpec((d, tn), lambda b, i, j: (0, nj + j)),
            pl.BlockSpec((3, tn), lambda b, i, j: (0, j)),
            pl.BlockSpec((1, tn), lambda b, i, j: (0, j)),
        ],
        out_specs=pl.BlockSpec((None, tm, tn), lambda b, i, j: (b, i, j)),
        out_shape=jax.ShapeDtypeStruct((bm, lm, ffn), BF16),
        compiler_params=_params(("arbitrary", "arbitrary", "arbitrary")),
        name="ffn1",
    )(h, w12, w12, conv_w, conv_b)


def _dft_cos_sin(n):
    idx = jnp.arange(n, dtype=jnp.int32)
    ang = (2.0 * math.pi / n) * ((idx[:, None] * idx[None, :]) % n).astype(F32)
    scale = 1.0 / math.sqrt(n)
    return jnp.cos(ang) * scale, jnp.sin(ang) * scale


def kernel(x, c, ctx, c_ctx, ada_w, ada_b, w_in, fourier_w, ssm_a_re, ssm_a_im, ssm_log_dt,
           ssm_b_re, ssm_b_im, ssm_c_re, ssm_c_im, ssm_d, glu_w, w_out, ln1_g, ln1_b,
           ffn_w12, ffn_conv_w, ffn_conv_b, ffn_w2, ln2_g, ln2_b):
    nb, seq, d = x.shape
    clen = ctx.shape[1]
    depth = ada_w.shape[0]
    fw = fourier_w.shape[1]
    sw = ssm_d.shape[1]
    ffn = ffn_conv_b.shape[1]
    alpha = (2.0 * depth) ** 0.25
    gd = fw // FOURIER_GROUPS
    chunk = clen
    assert seq % chunk == 0 and seq % GRID_W == 0

    ctx_row = nb
    n_rows = -(-(nb + 1) // SUBLANES) * SUBLANES
    cond = jnp.zeros((n_rows, d), F32).at[:nb].set(c).at[ctx_row].set(c_ctx)
    mods_all = _adaln_call(cond, ada_w, ada_b).reshape(depth, n_rows, 1, N_MOD * d)

    cos_c, sin_c = _dft_cos_sin(gd)
    cs_chan = jnp.concatenate([cos_c, sin_c], axis=1).astype(BF16)

    def pos_dft_matrix(n):
        cos_l, sin_l = _dft_cos_sin(n)
        return jnp.concatenate([cos_l, -sin_l], axis=1).astype(BF16)[None]

    csl_x = pos_dft_matrix(seq)
    csl_c = pos_dft_matrix(clen)

    def token_mix_tail(proj, csl, yf, yb, row_off, lw):
        ls = proj.shape[1]
        ab = _chan_dft_call(proj, cs_chan, fw)
        f = _mm_call(csl, ab.reshape(nb, 2 * ls, fw), BF16, "pos_dft")
        y_f = _mm_call(f.reshape(1, nb * ls, fw), lw["fourier_w"][None], BF16, "fourier_out")
        merged = _glu_call(proj, yf, yb, row_off, lw["ssm_d"], lw["glu_w"],
                           y_f.reshape(nb, ls, d), fw, sw, d)
        out = _mm_call(merged.reshape(1, nb * ls, d), lw["w_out"][None], BF16, "w_out")
        return out.reshape(nb, ls, d)

    def conv_ffn(h, lw, period):
        act = _ffn1_call(h, lw["w12"], lw["conv_w"], lw["conv_b"], period)
        return _mm_call(act, lw["w2"][None], BF16, "ffn2", tm_pref=512, tn_pref=512)

    xs = x
    cs = ctx.reshape(1, nb * clen, d)
    hx = _ln_mod_call(xs, mods_all[0], None, 0, 1)
    hc = _ln_mod_call(cs, mods_all[0], ctx_row, 0, 1)
    for i in range(depth):
        last = i == depth - 1
        mods = mods_all[i]
        nxt = None if last else (mods_all[i + 1], 0, 1)
        w_in_b = _cast_call(w_in, i)
        lw = {
            "fourier_w": _cast_call(fourier_w, i),
            "ssm_d": ssm_d[i].reshape(1, sw),
            "glu_w": _cast_call(glu_w, i),
            "w_out": _cast_call(w_out, i),
            "w12": _cast_call(ffn_w12, i),
            "w2": _cast_call(ffn_w2, i),
            "conv_w": ffn_conv_w[i],
            "conv_b": ffn_conv_b[i].reshape(1, ffn),
        }
        g1, b1 = ln1_g[i].reshape(1, d), ln1_b[i].reshape(1, d)
        g2, b2 = ln2_g[i].reshape(1, d), ln2_b[i].reshape(1, d)
        bmat, cmat, amat = _ssm_tables(ssm_a_re[i], ssm_a_im[i], ssm_log_dt[i], ssm_b_re[i],
                                       ssm_b_im[i], ssm_c_re[i], ssm_c_im[i])

        proj_x = _mm_call(hx.reshape(1, nb * seq, d), w_in_b[None], BF16, "w_in").reshape(nb, seq, -1)
        if last:
            proj_c = _mm_call(hc, w_in_b[None, :, fw:fw + sw], BF16, "w_in_ctx_ssm").reshape(nb, clen, sw)
            c_col = 0
        else:
            proj_c = _mm_call(hc, w_in_b[None], BF16, "w_in").reshape(nb, clen, -1)
            c_col = fw
        yf, yb = _ssm_call(proj_x, fw, proj_c, c_col, bmat, cmat, amat, chunk)

        out_x = token_mix_tail(proj_x, csl_x, yf, yb, 0, lw)
        x1, h2 = _postnorm_call(xs, out_x, mods, None, 2, g1, b1, alpha, (mods, 3, 4))
        ffn_x = conv_ffn(h2, lw, GRID_W)
        xs, hx = _postnorm_call(x1, ffn_x, mods, None, 5, g2, b2, alpha, nxt)

        if not last:
            out_c = token_mix_tail(proj_c, csl_c, yf, yb, seq, lw).reshape(1, nb * clen, d)
            c1, hc2 = _postnorm_call(cs, out_c, mods, ctx_row, 2, g1, b1, alpha, (mods, 3, 4))
            ffn_c = conv_ffn(hc2, lw, clen)
            cs, hc = _postnorm_call(c1, ffn_c, mods, ctx_row, 5, g2, b2, alpha, nxt)
    return xs
```

```python
import functools
import math

import jax
import jax.numpy as jnp
import ml_dtypes
import numpy as np
from jax import lax
from jax.experimental import pallas as pl
from jax.experimental.pallas import tpu as pltpu

GRID_W = 64
FOURIER_GROUPS = 4
N_MOD = 6
LN_EPS = 1e-6
SSM_BLOCK_GROUPS = 16
LANES = 128
SUBLANES = 8
VMEM_LIMIT = 56 * 1024 * 1024
GLU_SUBTILE = 256

F32 = jnp.float32
BF16 = jnp.bfloat16


def _params(sem):
    return pltpu.CompilerParams(dimension_semantics=sem, vmem_limit_bytes=VMEM_LIMIT)


def _tile(n, pref):
    if n <= pref:
        return n
    while n % pref:
        pref //= 2
    assert pref >= LANES, (n, pref)
    return pref


def _lane_tile(n, cap):
    for t in range(cap - cap % LANES, 0, -LANES):
        if n % t == 0:
            return t
    return n


def _dot(a, b):
    return jnp.dot(a, b, preferred_element_type=F32)


def _layer_norm(x):
    mu = jnp.mean(x, axis=-1, keepdims=True)
    xc = x - mu
    var = jnp.mean(xc * xc, axis=-1, keepdims=True)
    return xc * lax.rsqrt(var + LN_EPS)


def _mod_spec(mod_row, k, d):
    if mod_row is None:
        return pl.BlockSpec((None, 1, d), lambda b, *_: (b, 0, k))
    return pl.BlockSpec((None, 1, d), lambda b, *_: (mod_row, 0, k))


def _cast_kernel(w_ref, o_ref):
    o_ref[...] = w_ref[...].astype(o_ref.dtype)


def _cast_call(w, layer):
    _, r, c = w.shape
    tr = _tile(r, 512)
    tc = _lane_tile(c, 6144)
    return pl.pallas_call(
        _cast_kernel,
        grid=(r // tr, c // tc),
        in_specs=[pl.BlockSpec((None, tr, tc), lambda i, j: (layer, i, j))],
        out_specs=pl.BlockSpec((tr, tc), lambda i, j: (i, j)),
        out_shape=jax.ShapeDtypeStruct((r, c), BF16),
        compiler_params=_params(("arbitrary", "arbitrary")),
        name="cast_w",
    )(w)


def _cast_pad_kernel(w_ref, o_ref, *, axis, nvalid):
    valid = pl.program_id(axis) % (nvalid + 1) < nvalid

    @pl.when(valid)
    def _():
        o_ref[...] = w_ref[...].astype(o_ref.dtype)

    @pl.when(jnp.logical_not(valid))
    def _():
        o_ref[...] = jnp.zeros_like(o_ref)


def _cast_pad_rows_call(w, layer, tr):
    _, r, c = w.shape
    nvalid = r // tr
    assert r % tr == 0
    return pl.pallas_call(
        functools.partial(_cast_pad_kernel, axis=0, nvalid=nvalid),
        grid=(nvalid + 1, 1),
        in_specs=[pl.BlockSpec((None, tr, c), lambda i, j: (layer, jnp.minimum(i, nvalid - 1), 0))],
        out_specs=pl.BlockSpec((tr, c), lambda i, j: (i, 0)),
        out_shape=jax.ShapeDtypeStruct((r + tr, c), BF16),
        compiler_params=_params(("arbitrary", "arbitrary")),
        name="cast_w_pad_rows",
    )(w)


def _cast_pad_halves_call(w, layer, tc):
    _, r, c2 = w.shape
    f = c2 // 2
    nvalid = f // tc
    assert f % tc == 0

    def in_col(j):
        return (j // (nvalid + 1)) * nvalid + jnp.minimum(j % (nvalid + 1), nvalid - 1)

    return pl.pallas_call(
        functools.partial(_cast_pad_kernel, axis=1, nvalid=nvalid),
        grid=(1, 2 * (nvalid + 1)),
        in_specs=[pl.BlockSpec((None, r, tc), lambda i, j: (layer, 0, in_col(j)))],
        out_specs=pl.BlockSpec((r, tc), lambda i, j: (0, j)),
        out_shape=jax.ShapeDtypeStruct((r, 2 * (f + tc)), BF16),
        compiler_params=_params(("arbitrary", "arbitrary")),
        name="cast_w_pad_halves",
    )(w)


def _adaln_kernel(c_ref, w_ref, b_ref, o_ref):
    c = c_ref[...]
    s = (c * jax.nn.sigmoid(c)).astype(BF16)
    o_ref[...] = _dot(s, w_ref[...].astype(BF16)) + b_ref[...]


def _adaln_call(cond, ada_w, ada_b):
    depth, d, n = ada_w.shape
    r = cond.shape[0]
    tn = _tile(n, 512)
    return pl.pallas_call(
        _adaln_kernel,
        grid=(depth, n // tn),
        in_specs=[
            pl.BlockSpec((r, d), lambda l, j: (0, 0)),
            pl.BlockSpec((None, d, tn), lambda l, j: (l, 0, j)),
            pl.BlockSpec((None, 1, tn), lambda l, j: (l, 0, j)),
        ],
        out_specs=pl.BlockSpec((None, r, tn), lambda l, j: (l, 0, j)),
        out_shape=jax.ShapeDtypeStruct((depth, r, n), F32),
        compiler_params=_params(("arbitrary", "arbitrary")),
        name="adaln",
    )(cond, ada_w, ada_b.reshape(depth, 1, n))


def _ln_mod_kernel(x_ref, sh_ref, sc_ref, o_ref):
    y = _layer_norm(x_ref[...])
    o_ref[...] = (y * (1.0 + sc_ref[...]) + sh_ref[...]).astype(o_ref.dtype)


def _ln_mod_call(x, mods, mod_row, k_shift, k_scale):
    bm, lm, d = x.shape
    tm = _tile(lm, 256)
    return pl.pallas_call(
        _ln_mod_kernel,
        grid=(bm, lm // tm),
        in_specs=[
            pl.BlockSpec((None, tm, d), lambda b, i: (b, i, 0)),
            _mod_spec(mod_row, k_shift, d),
            _mod_spec(mod_row, k_scale, d),
        ],
        out_specs=pl.BlockSpec((None, tm, d), lambda b, i: (b, i, 0)),
        out_shape=jax.ShapeDtypeStruct((bm, lm, d), BF16),
        compiler_params=_params(("arbitrary", "arbitrary")),
        name="ln_mod",
    )(x, mods, mods)


def _postnorm_kernel(res_ref, pre_ref, gate_ref, g_ref, b_ref, *rest, alpha, with_next):
    z = alpha * res_ref[...] + gate_ref[...] * pre_ref[...].astype(F32)
    x_new = _layer_norm(z) * g_ref[...] + b_ref[...]
    if with_next:
        sh_ref, sc_ref, o_ref, h_ref = rest
        o_ref[...] = x_new
        h_ref[...] = (_layer_norm(x_new) * (1.0 + sc_ref[...]) + sh_ref[...]).astype(h_ref.dtype)
    else:
        (o_ref,) = rest
        o_ref[...] = x_new


def _postnorm_call(res, pre, mods, mod_row, k_gate, ln_g, ln_b, alpha, next_mod=None):
    bm, lm, d = res.shape
    tm = _tile(lm, 256)
    row = pl.BlockSpec((None, tm, d), lambda b, i: (b, i, 0))
    vec = pl.BlockSpec((1, d), lambda b, i: (0, 0))
    in_specs = [row, row, _mod_spec(mod_row, k_gate, d), vec, vec]
    args = [res, pre, mods, ln_g, ln_b]
    out_specs = [row]
    out_shape = [jax.ShapeDtypeStruct((bm, lm, d), F32)]
    if next_mod is not None:
        nmods, k_shift, k_scale = next_mod
        in_specs += [_mod_spec(mod_row, k_shift, d), _mod_spec(mod_row, k_scale, d)]
        args += [nmods, nmods]
        out_specs.append(row)
        out_shape.append(jax.ShapeDtypeStruct((bm, lm, d), BF16))
    out = pl.pallas_call(
        functools.partial(_postnorm_kernel, alpha=alpha, with_next=next_mod is not None),
        grid=(bm, lm // tm),
        in_specs=in_specs,
        out_specs=out_specs,
        out_shape=out_shape,
        compiler_params=_params(("arbitrary", "arbitrary")),
        name="postnorm",
    )(*args)
    return (out[0], out[1]) if next_mod is not None else (out[0], None)


def _mm_kernel(a_ref, w_ref, o_ref):
    o_ref[...] = _dot(a_ref[...], w_ref[...]).astype(o_ref.dtype)


def _mm_call(a, w, out_dtype, name, tm_pref=1024, tn_pref=1024):
    ba, m, k = a.shape
    bw, _, n = w.shape
    nb = max(ba, bw)
    tm = _tile(m, tm_pref)
    tn = _tile(n, tn_pref)
    a_map = (lambda b, i, j: (b, i, 0)) if ba > 1 else (lambda b, i, j: (0, i, 0))
    w_map = (lambda b, i, j: (b, 0, j)) if bw > 1 else (lambda b, i, j: (0, 0, j))
    return pl.pallas_call(
        _mm_kernel,
        grid=(nb, m // tm, n // tn),
        in_specs=[pl.BlockSpec((None, tm, k), a_map), pl.BlockSpec((None, k, tn), w_map)],
        out_specs=pl.BlockSpec((None, tm, tn), lambda b, i, j: (b, i, j)),
        out_shape=jax.ShapeDtypeStruct((nb, m, n), out_dtype),
        compiler_params=_params(("arbitrary", "arbitrary", "arbitrary")),
        name=name,
    )(a, w)


def _mm_w32_kernel(a_ref, w_ref, o_ref):
    o_ref[...] = _dot(a_ref[...], w_ref[...].astype(BF16)).astype(o_ref.dtype)


def _mm_w32_call(a, w, layer, col0, ncols, out_dtype, name):
    m, k = a.shape
    tm = _tile(m, 1024)
    tn = _tile(ncols, 512)
    assert col0 % tn == 0
    j0 = col0 // tn
    return pl.pallas_call(
        _mm_w32_kernel,
        grid=(ncols // tn, m // tm),
        in_specs=[
            pl.BlockSpec((tm, k), lambda j, i: (i, 0)),
            pl.BlockSpec((None, k, tn), lambda j, i: (layer, 0, j0 + j)),
        ],
        out_specs=pl.BlockSpec((tm, tn), lambda j, i: (i, j)),
        out_shape=jax.ShapeDtypeStruct((m, ncols), out_dtype),
        compiler_params=_params(("arbitrary", "arbitrary")),
        name=name,
    )(a, w)


def _chan_dft_kernel(u_ref, cs_ref, o_ref, *, gd):
    r = _dot(u_ref[...], cs_ref[...])
    o_ref[0] = r[:, :gd].astype(o_ref.dtype)
    o_ref[1] = r[:, gd:].astype(o_ref.dtype)


def _chan_dft_call(proj, cs, fw):
    bs, ls, _ = proj.shape
    gd = fw // FOURIER_GROUPS
    tm = _tile(ls, 1024)
    return pl.pallas_call(
        functools.partial(_chan_dft_kernel, gd=gd),
        grid=(bs, ls // tm, FOURIER_GROUPS),
        in_specs=[
            pl.BlockSpec((None, tm, gd), lambda b, i, g: (b, i, g)),
            pl.BlockSpec((gd, 2 * gd), lambda b, i, g: (0, 0)),
        ],
        out_specs=pl.BlockSpec((None, 2, tm, gd), lambda b, i, g: (b, 0, i, g)),
        out_shape=jax.ShapeDtypeStruct((bs, 2, ls, fw), BF16),
        compiler_params=_params(("arbitrary", "arbitrary", "arbitrary")),
        name="chan_dft",
    )(proj, cs)


def _ssm_kernel(uxf_ref, uxb_ref, ucf_ref, ucb_ref, bm_ref, cm_ref, a_ref, yf_ref, yb_ref,
                u_scr, s_scr, h_scr, *, chunk, nbat, half, n_ctx):
    c = pl.program_id(1)
    nslab = half // LANES
    nre = nslab // 2
    rows = SUBLANES
    hw = half // 2

    def col(h, s):
        return (s // nre) * half + h * hw + (s % nre) * LANES

    @pl.when(c == 0)
    def _():
        h_scr[...] = jnp.zeros_like(h_scr)

    @pl.when(c < n_ctx)
    def _():
        u_scr[0] = ucf_ref[...]
        u_scr[1] = ucb_ref[...]

    @pl.when(c >= n_ctx)
    def _():
        u_scr[0] = uxf_ref[...]
        u_scr[1] = uxb_ref[...]

    for d in range(2):
        bm = bm_ref[d]
        for b in range(nbat):
            res = _dot(u_scr[d, b], bm)
            for h in range(2):
                for s in range(nslab):
                    lo = col(h, s)
                    s_scr[d, s, pl.ds(2 * b + h, chunk, stride=rows), :] = res[:, lo:lo + LANES]

    arf, aif = a_ref[0, 0], a_ref[0, 1]
    arb, aib = a_ref[1, 0], a_ref[1, 1]

    def body(t, carry):
        hrf, hif, hrb, hib = carry
        rowf = pl.multiple_of(t * rows, rows)
        rowb = pl.multiple_of((chunk - 1 - t) * rows, rows)
        xf = s_scr[0, :, pl.ds(rowf, rows), :]
        xb = s_scr[1, :, pl.ds(rowb, rows), :]
        nrf = arf * hrf - aif * hif + xf[:nre]
        nif = arf * hif + aif * hrf + xf[nre:]
        nrb = arb * hrb - aib * hib + xb[:nre]
        nib = arb * hib + aib * hrb + xb[nre:]
        s_scr[0, :nre, pl.ds(rowf, rows), :] = nrf
        s_scr[0, nre:, pl.ds(rowf, rows), :] = nif
        s_scr[1, :nre, pl.ds(rowb, rows), :] = nrb
        s_scr[1, nre:, pl.ds(rowb, rows), :] = nib
        return nrf, nif, nrb, nib

    init = (h_scr[0, :nre], h_scr[0, nre:], h_scr[1, :nre], h_scr[1, nre:])
    hrf, hif, hrb, hib = lax.fori_loop(0, chunk, body, init, unroll=8)
    h_scr[0, :nre] = hrf
    h_scr[0, nre:] = hif
    h_scr[1, :nre] = hrb
    h_scr[1, nre:] = hib

    for d, y_ref in enumerate((yf_ref, yb_ref)):
        cm = cm_ref[d]
        for b in range(nbat):
            acc = None
            for h in range(2):
                for part in range(2):
                    pieces = [s_scr[d, s, pl.ds(2 * b + h, chunk, stride=rows), :]
                              for s in range(part * nre, (part + 1) * nre)]
                    hb = jnp.concatenate(pieces, axis=1).astype(BF16)
                    lo = col(h, part * nre)
                    p = _dot(hb, cm[lo:lo + hw])
                    acc = p if acc is None else acc + p
            y_ref[b] = acc.astype(y_ref.dtype)


def _ssm_call(ux, ux_col, uc, uc_col, bmat, cmat, amat, chunk):
    nbat, seq, _ = ux.shape
    clen = uc.shape[1]
    assert 2 * nbat == SUBLANES, "the scan packs 2 column halves x batch on the 8 sublanes"
    ndir, nblk, cin, cst = bmat.shape
    sw = nblk * cin
    half = cst // 2
    nslab = half // LANES
    n_lat, n_ctx = seq // chunk, clen // chunk
    n_all = n_lat + n_ctx
    assert ux_col % cin == 0 and uc_col % cin == 0
    xo, co = ux_col // cin, uc_col // cin

    def lat_f(c):
        return jnp.maximum(c - n_ctx, 0)

    def lat_b(c):
        return n_lat - 1 - jnp.maximum(c - n_ctx, 0)

    def ctx_f(c):
        return jnp.minimum(c, n_ctx - 1)

    def ctx_b(c):
        return jnp.maximum(n_ctx - 1 - c, 0)

    def out_f(c):
        return jnp.where(c < n_ctx, n_lat + c, c - n_ctx)

    def out_b(c):
        return jnp.where(c < n_ctx, n_lat + n_ctx - 1 - c, n_lat - 1 - (c - n_ctx))

    ublk = (nbat, chunk, cin)
    return pl.pallas_call(
        functools.partial(_ssm_kernel, chunk=chunk, nbat=nbat, half=half, n_ctx=n_ctx),
        grid=(nblk, n_all),
        in_specs=[
            pl.BlockSpec(ublk, lambda k, c: (0, lat_f(c), xo + k)),
            pl.BlockSpec(ublk, lambda k, c: (0, lat_b(c), xo + k)),
            pl.BlockSpec(ublk, lambda k, c: (0, ctx_f(c), co + k)),
            pl.BlockSpec(ublk, lambda k, c: (0, ctx_b(c), co + k)),
            pl.BlockSpec((ndir, None, cin, cst), lambda k, c: (0, k, 0, 0)),
            pl.BlockSpec((ndir, None, cst, cin), lambda k, c: (0, k, 0, 0)),
            pl.BlockSpec((ndir, None, 2, nslab // 2, SUBLANES, LANES), lambda k, c: (0, k, 0, 0, 0, 0)),
        ],
        out_specs=[
            pl.BlockSpec(ublk, lambda k, c: (0, out_f(c), k)),
            pl.BlockSpec(ublk, lambda k, c: (0, out_b(c), k)),
        ],
        out_shape=[jax.ShapeDtypeStruct((nbat, seq + clen, sw), BF16)] * 2,
        scratch_shapes=[
            pltpu.VMEM((ndir, nbat, chunk, cin), BF16),
            pltpu.VMEM((ndir, nslab, SUBLANES * chunk, LANES), F32),
            pltpu.VMEM((ndir, nslab, SUBLANES, LANES), F32),
        ],
        compiler_params=_params(("arbitrary", "arbitrary")),
        name="ssm",
    )(ux, ux, uc, uc, bmat, cmat, amat)


def _ssm_tables(a_re, a_im, log_dt, b_re, b_im, c_re, c_im):
    ndir, g, p = a_re.shape
    cg = b_re.shape[-1]
    gb = SSM_BLOCK_GROUPS
    nblk = g // gb
    npair = gb // 2
    assert 2 * p == LANES and g % gb == 0
    lr, li = a_re.astype(F32), a_im.astype(F32)
    dt = jnp.exp(log_dt.astype(F32))[..., None]
    zr, zi = lr * dt, li * dt
    a_bar_r = jnp.exp(zr) * jnp.cos(zi)
    a_bar_i = jnp.exp(zr) * jnp.sin(zi)
    em1_r = jnp.expm1(zr) * jnp.cos(zi) - 2.0 * jnp.square(jnp.sin(0.5 * zi))
    den = lr * lr + li * li
    cf_r = ((em1_r * lr + a_bar_i * li) / den)[..., None]
    cf_i = ((a_bar_i * lr - em1_r * li) / den)[..., None]
    br, bi = b_re.astype(F32), b_im.astype(F32)
    b_bar_r = cf_r * br - cf_i * bi
    b_bar_i = cf_r * bi + cf_i * br

    lane_q = np.arange(LANES) // p
    own = (np.arange(gb)[:, None, None] == 2 * np.arange(npair)[None, :, None] + lane_q[None, None, :])
    own_in = np.repeat(own, cg, axis=0).astype(np.float32)
    own_out = np.ascontiguousarray(own_in.transpose(1, 2, 0))

    def in_mat(bb):
        t = jnp.swapaxes(bb.reshape(ndir, nblk, gb, p, cg), -1, -2).reshape(ndir, nblk, gb * cg, p)
        t = jnp.tile(t, (1, 1, 1, 2))[:, :, :, None, :] * own_in
        return t.reshape(ndir, nblk, gb * cg, npair * LANES)

    def out_mat(cc):
        t = jnp.swapaxes(cc.reshape(ndir, nblk, gb * cg, p), -1, -2)
        t = jnp.tile(t, (1, 1, 2, 1))[:, :, None, :, :] * own_out
        return t.reshape(ndir, nblk, npair * LANES, gb * cg)

    bmat = jnp.concatenate([in_mat(b_bar_r), in_mat(b_bar_i)], axis=-1).astype(BF16)
    cmat = jnp.concatenate([out_mat(c_re.astype(F32)), out_mat(-c_im.astype(F32))], axis=-2).astype(BF16)

    def decay(a):
        t = jnp.swapaxes(a.reshape(ndir, nblk, 2, npair // 2, LANES), 2, 3)
        return jnp.tile(t, (1, 1, 1, SUBLANES // 2, 1))
    aa = jnp.stack([decay(a_bar_r), decay(a_bar_i)], axis=2)
    return bmat, cmat, aa


def _glu_kernel(u_ref, yf_ref, yb_ref, d_ref, wv_ref, wg_ref, gf_ref, gs_ref, fo_ref, o_ref, a_scr):
    @pl.when(pl.program_id(2) == 0)
    def _():
        y = (d_ref[...] * u_ref[...].astype(F32) + yf_ref[...].astype(F32)
             + yb_ref[...].astype(F32))
        a_scr[...] = jax.nn.gelu(y).astype(a_scr.dtype)

    a = a_scr[...]
    tn = o_ref.shape[1]
    sub = min(tn, GLU_SUBTILE)
    for n0 in range(0, tn, sub):
        cols = slice(n0, n0 + sub)
        y_s = (_dot(a, wv_ref[:, cols]) * jax.nn.sigmoid(_dot(a, wg_ref[:, cols]))).astype(BF16)
        o_ref[:, cols] = (jax.nn.sigmoid(gf_ref[:, cols]) * fo_ref[:, cols]
                          + jax.nn.sigmoid(gs_ref[:, cols]) * y_s)


def _glu_call(proj, yf, yb, row_off, ssm_d, glu_w, y_f, fw, sw, d):
    bs, ls, _ = proj.shape
    tm = _tile(ls, 512)
    tn = _tile(d, 1024)
    assert fw % sw == 0 and (fw + sw) % tn == 0 and row_off % tm == 0
    u_blk = fw // sw
    gf_blk = (fw + sw) // tn
    gs_blk = (fw + sw + d) // tn
    r_blk = row_off // tm
    nj = d // tn
    return pl.pallas_call(
        _glu_kernel,
        grid=(bs, ls // tm, nj),
        in_specs=[
            pl.BlockSpec((None, tm, sw), lambda b, i, j: (b, i, u_blk)),
            pl.BlockSpec((None, tm, sw), lambda b, i, j: (b, i + r_blk, 0)),
            pl.BlockSpec((None, tm, sw), lambda b, i, j: (b, i + r_blk, 0)),
            pl.BlockSpec((1, sw), lambda b, i, j: (0, 0)),
            pl.BlockSpec((sw, tn), lambda b, i, j: (0, j)),
            pl.BlockSpec((sw, tn), lambda b, i, j: (0, nj + j)),
            pl.BlockSpec((None, tm, tn), lambda b, i, j: (b, i, gf_blk + j)),
            pl.BlockSpec((None, tm, tn), lambda b, i, j: (b, i, gs_blk + j)),
            pl.BlockSpec((None, tm, tn), lambda b, i, j: (b, i, j)),
        ],
        out_specs=pl.BlockSpec((None, tm, tn), lambda b, i, j: (b, i, j)),
        out_shape=jax.ShapeDtypeStruct((bs, ls, d), BF16),
        scratch_shapes=[pltpu.VMEM((tm, sw), BF16)],
        compiler_params=_params(("arbitrary", "arbitrary", "arbitrary")),
        name="glu_merge",
    )(proj, yf, yb, ssm_d, glu_w, glu_w, proj, proj, y_f)


def _ffn1_kernel(h_ref, wu_ref, wv_ref, cw_ref, cb_ref, o_ref, *, period):
    h = h_ref[...]
    u = _dot(h, wu_ref[...])
    v = _dot(h, wv_ref[...])
    tm = u.shape[0]
    pos = lax.broadcasted_iota(jnp.int32, u.shape, 0) % period
    prev = jnp.where(pos == 0, 0.0, pltpu.roll(u, 1, axis=0))
    nxt = jnp.where(pos == period - 1, 0.0, pltpu.roll(u, tm - 1, axis=0))
    cw = cw_ref[...]
    conv = cb_ref[...] + cw[0:1] * prev + cw[1:2] * u + cw[2:3] * nxt
    o_ref[...] = (jax.nn.gelu(conv) * v).astype(o_ref.dtype)


def _ffn1_call(h, w12, conv_w, conv_b, period):
    bm, lm, d = h.shape
    ffn = w12.shape[1] // 2
    tm = _tile(lm, 1024)
    tn = _tile(ffn, 512)
    nj = ffn // tn
    assert tm % period == 0
    return pl.pallas_call(
        functools.partial(_ffn1_kernel, period=period),
        grid=(bm, lm // tm, nj),
        in_specs=[
            pl.BlockSpec((None, tm, d), lambda b, i, j: (b, i, 0)),
            pl.BlockSpec((d, tn), lambda b, i, j: (0, j)),
            pl.BlockSpec((d, tn), lambda b, i, j: (0, nj + j)),
            pl.BlockSpec((3, tn), lambda b, i, j: (0, j)),
            pl.BlockSpec((1, tn), lambda b, i, j: (0, j)),
        ],
        out_specs=pl.BlockSpec((None, tm, tn), lambda b, i, j: (b, i, j)),
        out_shape=jax.ShapeDtypeStruct((bm, lm, ffn), BF16),
        compiler_params=_params(("arbitrary", "arbitrary", "arbitrary")),
        name="ffn1",
    )(h, w12, w12, conv_w, conv_b)


def _dft_cos_sin(n):
    idx = np.arange(n, dtype=np.int64)
    ang = (2.0 * math.pi / n) * ((idx[:, None] * idx[None, :]) % n)
    scale = 1.0 / math.sqrt(n)
    return np.cos(ang) * scale, np.sin(ang) * scale


def _bf16_const(a):
    return jnp.asarray(np.asarray(a, np.float32).astype(ml_dtypes.bfloat16))


def kernel(x, c, ctx, c_ctx, ada_w, ada_b, w_in, fourier_w, ssm_a_re, ssm_a_im, ssm_log_dt,
           ssm_b_re, ssm_b_im, ssm_c_re, ssm_c_im, ssm_d, glu_w, w_out, ln1_g, ln1_b,
           ffn_w12, ffn_conv_w, ffn_conv_b, ffn_w2, ln2_g, ln2_b):
    nb, seq, d = x.shape
    clen = ctx.shape[1]
    depth = ada_w.shape[0]
    fw = fourier_w.shape[1]
    sw = ssm_d.shape[1]
    ffn = ffn_conv_b.shape[1]
    alpha = (2.0 * depth) ** 0.25
    gd = fw // FOURIER_GROUPS
    chunk = clen
    assert seq % chunk == 0 and seq % GRID_W == 0
    ffn_pad = 256 if (ffn % 512 == 256) else 0

    ctx_row = nb
    n_rows = -(-(nb + 1) // SUBLANES) * SUBLANES
    cond = jnp.zeros((n_rows, d), F32).at[:nb].set(c).at[ctx_row].set(c_ctx)
    mods_all = _adaln_call(cond, ada_w, ada_b).reshape(depth, n_rows, 1, N_MOD * d)

    cos_c, sin_c = _dft_cos_sin(gd)
    cs_chan = _bf16_const(np.concatenate([cos_c, sin_c], axis=1))

    def pos_dft_matrix(n):
        cos_l, sin_l = _dft_cos_sin(n)
        return _bf16_const(np.concatenate([cos_l, -sin_l], axis=1))[None]

    csl_x = pos_dft_matrix(seq)
    csl_c = pos_dft_matrix(clen)

    def token_mix_tail(proj, csl, yf, yb, row_off, lw):
        ls = proj.shape[1]
        ab = _chan_dft_call(proj, cs_chan, fw)
        f = _mm_call(csl, ab.reshape(nb, 2 * ls, fw), BF16, "pos_dft")
        y_f = _mm_call(f.reshape(1, nb * ls, fw), lw["fourier_w"][None], BF16, "fourier_out")
        merged = _glu_call(proj, yf, yb, row_off, lw["ssm_d"], lw["glu_w"],
                           y_f.reshape(nb, ls, d), fw, sw, d)
        out = _mm_call(merged.reshape(1, nb * ls, d), lw["w_out"][None], BF16, "w_out")
        return out.reshape(nb, ls, d)

    def conv_ffn(h, lw, period):
        act = _ffn1_call(h, lw["w12"], lw["conv_w"], lw["conv_b"], period)
        return _mm_call(act, lw["w2"][None], BF16, "ffn2", tm_pref=512, tn_pref=512)

    xs = x
    cs = ctx.reshape(1, nb * clen, d)
    hx = _ln_mod_call(xs, mods_all[0], None, 0, 1)
    hc = _ln_mod_call(cs, mods_all[0], ctx_row, 0, 1)
    for i in range(depth):
        last = i == depth - 1
        mods = mods_all[i]
        nxt = None if last else (mods_all[i + 1], 0, 1)
        lw = {
            "fourier_w": _cast_call(fourier_w, i),
            "ssm_d": ssm_d[i].reshape(1, sw),
            "glu_w": _cast_call(glu_w, i),
            "w_out": _cast_call(w_out, i),
        }
        if ffn_pad:
            lw["w12"] = _cast_pad_halves_call(ffn_w12, i, ffn_pad)
            lw["w2"] = _cast_pad_rows_call(ffn_w2, i, ffn_pad)
        else:
            lw["w12"] = _cast_call(ffn_w12, i)
            lw["w2"] = _cast_call(ffn_w2, i)
        lw["conv_w"] = jnp.pad(ffn_conv_w[i], ((0, 0), (0, ffn_pad)))
        lw["conv_b"] = jnp.pad(ffn_conv_b[i].reshape(1, ffn), ((0, 0), (0, ffn_pad)))
        g1, b1 = ln1_g[i].reshape(1, d), ln1_b[i].reshape(1, d)
        g2, b2 = ln2_g[i].reshape(1, d), ln2_b[i].reshape(1, d)
        bmat, cmat, amat = _ssm_tables(ssm_a_re[i], ssm_a_im[i], ssm_log_dt[i], ssm_b_re[i],
                                       ssm_b_im[i], ssm_c_re[i], ssm_c_im[i])

        n_in = w_in.shape[2]
        proj_x = _mm_w32_call(hx.reshape(nb * seq, d), w_in, i, 0, n_in, BF16, "w_in").reshape(nb, seq, n_in)
        hc2d = hc.reshape(nb * clen, d)
        if last:
            proj_c = _mm_w32_call(hc2d, w_in, i, fw, sw, BF16, "w_in_ctx_ssm").reshape(nb, clen, sw)
            c_col = 0
        else:
            proj_c = _mm_w32_call(hc2d, w_in, i, 0, n_in, BF16, "w_in").reshape(nb, clen, n_in)
            c_col = fw
        yf, yb = _ssm_call(proj_x, fw, proj_c, c_col, bmat, cmat, amat, chunk)

        out_x = token_mix_tail(proj_x, csl_x, yf, yb, 0, lw)
        x1, h2 = _postnorm_call(xs, out_x, mods, None, 2, g1, b1, alpha, (mods, 3, 4))
        ffn_x = conv_ffn(h2, lw, GRID_W)
        xs, hx = _postnorm_call(x1, ffn_x, mods, None, 5, g2, b2, alpha, nxt)

        if not last:
            out_c = token_mix_tail(proj_c, csl_c, yf, yb, seq, lw).reshape(1, nb * clen, d)
            c1, hc2 = _postnorm_call(cs, out_c, mods, ctx_row, 2, g1, b1, alpha, (mods, 3, 4))
            ffn_c = conv_ffn(hc2, lw, clen)
            cs, hc = _postnorm_call(c1, ffn_c, mods, ctx_row, 5, g2, b2, alpha, nxt)
    return xs
```

```python
import functools
import math
from typing import Callable, NamedTuple

import jax
import jax.numpy as jnp
import ml_dtypes
import numpy as np
from jax import lax
from jax.experimental import pallas as pl
from jax.experimental.pallas import tpu as pltpu

GRID_W = 64
FOURIER_GROUPS = 4
N_MOD = 6
LN_EPS = 1e-6
SSM_BLOCK_GROUPS = 16
LANES = 128
SUBLANES = 8
VMEM_LIMIT = 56 * 1024 * 1024
GLU_SUBTILE = 256
CAST_ROWS = 256

F32 = jnp.float32
BF16 = jnp.bfloat16


def _params(sem):
    return pltpu.CompilerParams(dimension_semantics=sem, vmem_limit_bytes=VMEM_LIMIT)


def _tile(n, pref):
    if n <= pref:
        return n
    while n % pref:
        pref //= 2
    assert pref >= LANES, (n, pref)
    return pref


def _lane_tile(n, cap):
    for t in range(cap - cap % LANES, 0, -LANES):
        if n % t == 0:
            return t
    return n


def _dot(a, b):
    return jnp.dot(a, b, preferred_element_type=F32)


def _layer_norm(x):
    mu = jnp.mean(x, axis=-1, keepdims=True)
    xc = x - mu
    var = jnp.mean(xc * xc, axis=-1, keepdims=True)
    return xc * lax.rsqrt(var + LN_EPS)


def _mod_spec(mod_row, k, d):
    if mod_row is None:
        return pl.BlockSpec((None, 1, d), lambda b, *_: (b, 0, k))
    return pl.BlockSpec((None, 1, d), lambda b, *_: (mod_row, 0, k))


def _cast_kernel(w_ref, o_ref):
    o_ref[...] = w_ref[...].astype(o_ref.dtype)


def _cast_call(w, layer):
    _, r, c = w.shape
    tr = _tile(r, 512)
    tc = _lane_tile(c, 6144)
    return pl.pallas_call(
        _cast_kernel,
        grid=(r // tr, c // tc),
        in_specs=[pl.BlockSpec((None, tr, tc), lambda i, j: (layer, i, j))],
        out_specs=pl.BlockSpec((tr, tc), lambda i, j: (i, j)),
        out_shape=jax.ShapeDtypeStruct((r, c), BF16),
        compiler_params=_params(("arbitrary", "arbitrary")),
        name="cast_w",
    )(w)


class _PadCast(NamedTuple):
    layer: int
    in_block: tuple
    in_index: Callable
    out_block: tuple
    out_index: Callable
    out_shape: tuple
    nblk: int
    is_data: Callable


def _rows_cast(w, layer, tr):
    _, r, c = w.shape
    assert r % tr == 0
    return _PadCast(layer, (None, tr, c), lambda s: (s, 0), (tr, c), lambda s: (s, 0), (r, c),
                    r // tr, lambda s: s >= 0)


def _pad_rows_cast(w, layer, tr):
    _, r, c = w.shape
    nvalid = r // tr
    assert r % tr == 0
    return _PadCast(layer, (None, tr, c), lambda s: (jnp.minimum(s, nvalid - 1), 0),
                    (tr, c), lambda s: (s, 0), (r + tr, c), nvalid + 1, lambda s: s < nvalid)


def _pad_halves_cast(w, layer, tc):
    _, r, c2 = w.shape
    f = c2 // 2
    nvalid = f // tc
    assert f % tc == 0
    per = nvalid + 1
    return _PadCast(layer, (None, r, tc),
                    lambda s: (0, (s // per) * nvalid + jnp.minimum(s % per, nvalid - 1)),
                    (r, tc), lambda s: (0, s), (r, 2 * (f + tc)), 2 * per, lambda s: s % per < nvalid)


def _cast_specs(job, step_of):
    def clamped(*g):
        return jnp.minimum(step_of(*g), job.nblk - 1)

    in_spec = pl.BlockSpec(job.in_block, lambda *g: (job.layer,) + tuple(job.in_index(clamped(*g))))
    out_spec = pl.BlockSpec(job.out_block, lambda *g: tuple(job.out_index(clamped(*g))))
    return in_spec, out_spec


def _cast_step(job, step, w_ref, o_ref):
    @pl.when(step < job.nblk)
    def _():
        o_ref[...] = jnp.where(job.is_data(step), w_ref[...], 0.0).astype(o_ref.dtype)


def _rider_fits(rider, nsteps):
    return rider[0].nblk <= nsteps


def _attach_rider(rider, step_of, in_specs, out_specs, out_shape, args):
    job, rw = rider
    r_in, r_out = _cast_specs(job, step_of)
    in_specs.append(r_in)
    out_specs.append(r_out)
    out_shape.append(jax.ShapeDtypeStruct(job.out_shape, BF16))
    args.append(rw)
    return job


def _pad_cast_kernel(w_ref, o_ref, *, job):
    _cast_step(job, pl.program_id(0), w_ref, o_ref)


def _pad_cast_call(job, w):
    in_spec, out_spec = _cast_specs(job, lambda s: s)
    return pl.pallas_call(
        functools.partial(_pad_cast_kernel, job=job),
        grid=(job.nblk,),
        in_specs=[in_spec],
        out_specs=out_spec,
        out_shape=jax.ShapeDtypeStruct(job.out_shape, BF16),
        compiler_params=_params(("arbitrary",)),
        name="cast_w_pad",
    )(w)


def _adaln_kernel(c_ref, w_ref, b_ref, o_ref):
    c = c_ref[...]
    s = (c * jax.nn.sigmoid(c)).astype(BF16)
    o_ref[...] = _dot(s, w_ref[...].astype(BF16)) + b_ref[...]


def _adaln_call(cond, ada_w, ada_b):
    depth, d, n = ada_w.shape
    r = cond.shape[0]
    tn = _tile(n, 512)
    return pl.pallas_call(
        _adaln_kernel,
        grid=(depth, n // tn),
        in_specs=[
            pl.BlockSpec((r, d), lambda l, j: (0, 0)),
            pl.BlockSpec((None, d, tn), lambda l, j: (l, 0, j)),
            pl.BlockSpec((None, 1, tn), lambda l, j: (l, 0, j)),
        ],
        out_specs=pl.BlockSpec((None, r, tn), lambda l, j: (l, 0, j)),
        out_shape=jax.ShapeDtypeStruct((depth, r, n), F32),
        compiler_params=_params(("arbitrary", "arbitrary")),
        name="adaln",
    )(cond, ada_w, ada_b.reshape(depth, 1, n))


def _ln_mod_kernel(x_ref, sh_ref, sc_ref, o_ref):
    y = _layer_norm(x_ref[...])
    o_ref[...] = (y * (1.0 + sc_ref[...]) + sh_ref[...]).astype(o_ref.dtype)


def _ln_mod_call(x, mods, mod_row, k_shift, k_scale):
    bm, lm, d = x.shape
    tm = _tile(lm, 256)
    return pl.pallas_call(
        _ln_mod_kernel,
        grid=(bm, lm // tm),
        in_specs=[
            pl.BlockSpec((None, tm, d), lambda b, i: (b, i, 0)),
            _mod_spec(mod_row, k_shift, d),
            _mod_spec(mod_row, k_scale, d),
        ],
        out_specs=pl.BlockSpec((None, tm, d), lambda b, i: (b, i, 0)),
        out_shape=jax.ShapeDtypeStruct((bm, lm, d), BF16),
        compiler_params=_params(("arbitrary", "arbitrary")),
        name="ln_mod",
    )(x, mods, mods)


def _postnorm_kernel(res_ref, pre_ref, gate_ref, g_ref, b_ref, *rest, alpha, with_next):
    z = alpha * res_ref[...] + gate_ref[...] * pre_ref[...].astype(F32)
    x_new = _layer_norm(z) * g_ref[...] + b_ref[...]
    if with_next:
        sh_ref, sc_ref, o_ref, h_ref = rest
        o_ref[...] = x_new
        h_ref[...] = (_layer_norm(x_new) * (1.0 + sc_ref[...]) + sh_ref[...]).astype(h_ref.dtype)
    else:
        (o_ref,) = rest
        o_ref[...] = x_new


def _postnorm_call(res, pre, mods, mod_row, k_gate, ln_g, ln_b, alpha, next_mod=None):
    bm, lm, d = res.shape
    tm = _tile(lm, 256)
    row = pl.BlockSpec((None, tm, d), lambda b, i: (b, i, 0))
    vec = pl.BlockSpec((1, d), lambda b, i: (0, 0))
    in_specs = [row, row, _mod_spec(mod_row, k_gate, d), vec, vec]
    args = [res, pre, mods, ln_g, ln_b]
    out_specs = [row]
    out_shape = [jax.ShapeDtypeStruct((bm, lm, d), F32)]
    if next_mod is not None:
        nmods, k_shift, k_scale = next_mod
        in_specs += [_mod_spec(mod_row, k_shift, d), _mod_spec(mod_row, k_scale, d)]
        args += [nmods, nmods]
        out_specs.append(row)
        out_shape.append(jax.ShapeDtypeStruct((bm, lm, d), BF16))
    out = pl.pallas_call(
        functools.partial(_postnorm_kernel, alpha=alpha, with_next=next_mod is not None),
        grid=(bm, lm // tm),
        in_specs=in_specs,
        out_specs=out_specs,
        out_shape=out_shape,
        compiler_params=_params(("arbitrary", "arbitrary")),
        name="postnorm",
    )(*args)
    return (out[0], out[1]) if next_mod is not None else (out[0], None)


def _mm_kernel(a_ref, w_ref, *rest, ni, nj, job):
    if job is None:
        (o_ref,) = rest
    else:
        rw_ref, o_ref, ro_ref = rest
        step = (pl.program_id(0) * ni + pl.program_id(1)) * nj + pl.program_id(2)
        _cast_step(job, step, rw_ref, ro_ref)
    o_ref[...] = _dot(a_ref[...], w_ref[...]).astype(o_ref.dtype)


def _mm_call(a, w, out_dtype, name, tm_pref=1024, tn_pref=1024, rider=None):
    ba, m, k = a.shape
    bw, _, n = w.shape
    nb = max(ba, bw)
    tm = _tile(m, tm_pref)
    tn = _tile(n, tn_pref)
    ni, nj = m // tm, n // tn
    if rider is not None and not _rider_fits(rider, nb * ni * nj):
        return _mm_call(a, w, out_dtype, name, tm_pref, tn_pref), _pad_cast_call(*rider)
    a_map = (lambda b, i, j: (b, i, 0)) if ba > 1 else (lambda b, i, j: (0, i, 0))
    w_map = (lambda b, i, j: (b, 0, j)) if bw > 1 else (lambda b, i, j: (0, 0, j))
    in_specs = [pl.BlockSpec((None, tm, k), a_map), pl.BlockSpec((None, k, tn), w_map)]
    out_specs = [pl.BlockSpec((None, tm, tn), lambda b, i, j: (b, i, j))]
    out_shape = [jax.ShapeDtypeStruct((nb, m, n), out_dtype)]
    args = [a, w]
    job = None
    if rider is not None:
        job = _attach_rider(rider, lambda b, i, j: (b * ni + i) * nj + j,
                            in_specs, out_specs, out_shape, args)
    out = pl.pallas_call(
        functools.partial(_mm_kernel, ni=ni, nj=nj, job=job),
        grid=(nb, ni, nj),
        in_specs=in_specs,
        out_specs=out_specs,
        out_shape=out_shape,
        compiler_params=_params(("arbitrary", "arbitrary", "arbitrary")),
        name=name,
    )(*args)
    return out if rider is not None else out[0]


def _mm_w32_kernel(a_ref, w_ref, *rest, ni, job):
    if job is None:
        o_ref, w_scr = rest
    else:
        rw_ref, o_ref, ro_ref, w_scr = rest
    i = pl.program_id(1)

    @pl.when(i == 0)
    def _():
        w_scr[...] = w_ref[...].astype(w_scr.dtype)

    if job is not None:
        _cast_step(job, pl.program_id(0) * ni + i, rw_ref, ro_ref)
    o_ref[...] = _dot(a_ref[...], w_scr[...]).astype(o_ref.dtype)


def _mm_w32_call(a, w, layer, col0, ncols, out_dtype, name, rider=None):
    m, k = a.shape
    tm = _tile(m, 1024)
    tn = _tile(ncols, 512)
    assert col0 % tn == 0
    j0 = col0 // tn
    ni = m // tm
    in_specs = [
        pl.BlockSpec((tm, k), lambda j, i: (i, 0)),
        pl.BlockSpec((None, k, tn), lambda j, i: (layer, 0, j0 + j)),
    ]
    out_specs = [pl.BlockSpec((tm, tn), lambda j, i: (i, j))]
    out_shape = [jax.ShapeDtypeStruct((m, ncols), out_dtype)]
    args = [a, w]
    job = None
    if rider is not None and not _rider_fits(rider, (ncols // tn) * ni):
        return _mm_w32_call(a, w, layer, col0, ncols, out_dtype, name), _pad_cast_call(*rider)
    if rider is not None:
        job = _attach_rider(rider, lambda j, i: j * ni + i, in_specs, out_specs, out_shape, args)
    out = pl.pallas_call(
        functools.partial(_mm_w32_kernel, ni=ni, job=job),
        grid=(ncols // tn, ni),
        in_specs=in_specs,
        out_specs=out_specs,
        out_shape=out_shape,
        scratch_shapes=[pltpu.VMEM((k, tn), BF16)],
        compiler_params=_params(("arbitrary", "arbitrary")),
        name=name,
    )(*args)
    return out if rider is not None else out[0]


def _chan_dft_kernel(u_ref, cs_ref, o_ref, *, gd):
    r = _dot(u_ref[...], cs_ref[...])
    o_ref[0] = r[:, :gd].astype(o_ref.dtype)
    o_ref[1] = r[:, gd:].astype(o_ref.dtype)


def _chan_dft_call(proj, cs, fw):
    bs, ls, _ = proj.shape
    gd = fw // FOURIER_GROUPS
    tm = _tile(ls, 1024)
    return pl.pallas_call(
        functools.partial(_chan_dft_kernel, gd=gd),
        grid=(bs, ls // tm, FOURIER_GROUPS),
        in_specs=[
            pl.BlockSpec((None, tm, gd), lambda b, i, g: (b, i, g)),
            pl.BlockSpec((gd, 2 * gd), lambda b, i, g: (0, 0)),
        ],
        out_specs=pl.BlockSpec((None, 2, tm, gd), lambda b, i, g: (b, 0, i, g)),
        out_shape=jax.ShapeDtypeStruct((bs, 2, ls, fw), BF16),
        compiler_params=_params(("arbitrary", "arbitrary", "arbitrary")),
        name="chan_dft",
    )(proj, cs)


def _ssm_kernel(uxf_ref, uxb_ref, ucf_ref, ucb_ref, bm_ref, cm_ref, a_ref, *rest,
                chunk, nbat, half, n_ctx, n_all, job):
    side_work = None
    if job is None:
        yf_ref, yb_ref, u_scr, s_scr, h_scr = rest
    else:
        rw_ref, yf_ref, yb_ref, ro_ref, u_scr, s_scr, h_scr = rest
        step = pl.program_id(0) * n_all + pl.program_id(1)
        side_work = functools.partial(_cast_step, job, step, rw_ref, ro_ref)
    _ssm_step(uxf_ref, uxb_ref, ucf_ref, ucb_ref, bm_ref, cm_ref, a_ref, yf_ref, yb_ref,
              u_scr, s_scr, h_scr, chunk=chunk, nbat=nbat, half=half, n_ctx=n_ctx,
              side_work=side_work)


def _ssm_step(uxf_ref, uxb_ref, ucf_ref, ucb_ref, bm_ref, cm_ref, a_ref, yf_ref, yb_ref,
              u_scr, s_scr, h_scr, *, chunk, nbat, half, n_ctx, side_work):
    c = pl.program_id(1)
    nslab = half // LANES
    nre = nslab // 2
    rows = SUBLANES
    hw = half // 2

    def col(h, s):
        return (s // nre) * half + h * hw + (s % nre) * LANES

    @pl.when(c == 0)
    def _():
        h_scr[...] = jnp.zeros_like(h_scr)

    @pl.when(c < n_ctx)
    def _():
        u_scr[0] = ucf_ref[...]
        u_scr[1] = ucb_ref[...]

    @pl.when(c >= n_ctx)
    def _():
        u_scr[0] = uxf_ref[...]
        u_scr[1] = uxb_ref[...]

    if side_work is not None:
        side_work()
    for d in range(2):
        bm = bm_ref[d]
        for b in range(nbat):
            res = _dot(u_scr[d, b], bm)
            for h in range(2):
                for s in range(nslab):
                    lo = col(h, s)
                    s_scr[d, s, pl.ds(2 * b + h, chunk, stride=rows), :] = res[:, lo:lo + LANES]

    arf, aif = a_ref[0, 0], a_ref[0, 1]
    arb, aib = a_ref[1, 0], a_ref[1, 1]

    def body(t, carry):
        hrf, hif, hrb, hib = carry
        rowf = pl.multiple_of(t * rows, rows)
        rowb = pl.multiple_of((chunk - 1 - t) * rows, rows)
        xf = s_scr[0, :, pl.ds(rowf, rows), :]
        xb = s_scr[1, :, pl.ds(rowb, rows), :]
        nrf = arf * hrf - aif * hif + xf[:nre]
        nif = arf * hif + aif * hrf + xf[nre:]
        nrb = arb * hrb - aib * hib + xb[:nre]
        nib = arb * hib + aib * hrb + xb[nre:]
        s_scr[0, :nre, pl.ds(rowf, rows), :] = nrf
        s_scr[0, nre:, pl.ds(rowf, rows), :] = nif
        s_scr[1, :nre, pl.ds(rowb, rows), :] = nrb
        s_scr[1, nre:, pl.ds(rowb, rows), :] = nib
        return nrf, nif, nrb, nib

    init = (h_scr[0, :nre], h_scr[0, nre:], h_scr[1, :nre], h_scr[1, nre:])
    hrf, hif, hrb, hib = lax.fori_loop(0, chunk, body, init, unroll=8)
    h_scr[0, :nre] = hrf
    h_scr[0, nre:] = hif
    h_scr[1, :nre] = hrb
    h_scr[1, nre:] = hib

    for d, y_ref in enumerate((yf_ref, yb_ref)):
        cm = cm_ref[d]
        for b in range(nbat):
            acc = None
            for h in range(2):
                for part in range(2):
                    pieces = [s_scr[d, s, pl.ds(2 * b + h, chunk, stride=rows), :]
                              for s in range(part * nre, (part + 1) * nre)]
                    hb = jnp.concatenate(pieces, axis=1).astype(BF16)
                    lo = col(h, part * nre)
                    p = _dot(hb, cm[lo:lo + hw])
                    acc = p if acc is None else acc + p
            y_ref[b] = acc.astype(y_ref.dtype)


def _ssm_call(ux, ux_col, uc, uc_col, bmat, cmat, amat, chunk, rider=None):
    nbat, seq, _ = ux.shape
    clen = uc.shape[1]
    assert 2 * nbat == SUBLANES, "the scan packs 2 column halves x batch on the 8 sublanes"
    ndir, nblk, cin, cst = bmat.shape
    sw = nblk * cin
    half = cst // 2
    nslab = half // LANES
    n_lat, n_ctx = seq // chunk, clen // chunk
    n_all = n_lat + n_ctx
    assert ux_col % cin == 0 and uc_col % cin == 0
    xo, co = ux_col // cin, uc_col // cin

    def lat_f(c):
        return jnp.maximum(c - n_ctx, 0)

    def lat_b(c):
        return n_lat - 1 - jnp.maximum(c - n_ctx, 0)

    def ctx_f(c):
        return jnp.minimum(c, n_ctx - 1)

    def ctx_b(c):
        return jnp.maximum(n_ctx - 1 - c, 0)

    def out_f(c):
        return jnp.where(c < n_ctx, n_lat + c, c - n_ctx)

    def out_b(c):
        return jnp.where(c < n_ctx, n_lat + n_ctx - 1 - c, n_lat - 1 - (c - n_ctx))

    ublk = (nbat, chunk, cin)
    in_specs = [
        pl.BlockSpec(ublk, lambda k, c: (0, lat_f(c), xo + k)),
        pl.BlockSpec(ublk, lambda k, c: (0, lat_b(c), xo + k)),
        pl.BlockSpec(ublk, lambda k, c: (0, ctx_f(c), co + k)),
        pl.BlockSpec(ublk, lambda k, c: (0, ctx_b(c), co + k)),
        pl.BlockSpec((ndir, None, cin, cst), lambda k, c: (0, k, 0, 0)),
        pl.BlockSpec((ndir, None, cst, cin), lambda k, c: (0, k, 0, 0)),
        pl.BlockSpec((ndir, None, 2, nslab // 2, SUBLANES, LANES), lambda k, c: (0, k, 0, 0, 0, 0)),
    ]
    out_specs = [
        pl.BlockSpec(ublk, lambda k, c: (0, out_f(c), k)),
        pl.BlockSpec(ublk, lambda k, c: (0, out_b(c), k)),
    ]
    out_shape = [jax.ShapeDtypeStruct((nbat, seq + clen, sw), BF16)] * 2
    args = [ux, ux, uc, uc, bmat, cmat, amat]
    job = None
    if rider is not None and not _rider_fits(rider, nblk * n_all):
        yf, yb = _ssm_call(ux, ux_col, uc, uc_col, bmat, cmat, amat, chunk)
        return yf, yb, _pad_cast_call(*rider)
    if rider is not None:
        job = _attach_rider(rider, lambda k, c: k * n_all + c, in_specs, out_specs, out_shape, args)
    return pl.pallas_call(
        functools.partial(_ssm_kernel, chunk=chunk, nbat=nbat, half=half, n_ctx=n_ctx,
                          n_all=n_all, job=job),
        grid=(nblk, n_all),
        in_specs=in_specs,
        out_specs=out_specs,
        out_shape=out_shape,
        scratch_shapes=[
            pltpu.VMEM((ndir, nbat, chunk, cin), BF16),
            pltpu.VMEM((ndir, nslab, SUBLANES * chunk, LANES), F32),
            pltpu.VMEM((ndir, nslab, SUBLANES, LANES), F32),
        ],
        compiler_params=_params(("arbitrary", "arbitrary")),
        name="ssm",
    )(*args)


def _ssm_tables(a_re, a_im, log_dt, b_re, b_im, c_re, c_im):
    ndir, g, p = a_re.shape
    cg = b_re.shape[-1]
    gb = SSM_BLOCK_GROUPS
    nblk = g // gb
    npair = gb // 2
    assert 2 * p == LANES and g % gb == 0
    lr, li = a_re.astype(F32), a_im.astype(F32)
    dt = jnp.exp(log_dt.astype(F32))[..., None]
    zr, zi = lr * dt, li * dt
    a_bar_r = jnp.exp(zr) * jnp.cos(zi)
    a_bar_i = jnp.exp(zr) * jnp.sin(zi)
    em1_r = jnp.expm1(zr) * jnp.cos(zi) - 2.0 * jnp.square(jnp.sin(0.5 * zi))
    den = lr * lr + li * li
    cf_r = ((em1_r * lr + a_bar_i * li) / den)[..., None]
    cf_i = ((a_bar_i * lr - em1_r * li) / den)[..., None]
    br, bi = b_re.astype(F32), b_im.astype(F32)
    b_bar_r = cf_r * br - cf_i * bi
    b_bar_i = cf_r * bi + cf_i * br

    lane_q = np.arange(LANES) // p
    own = (np.arange(gb)[:, None, None] == 2 * np.arange(npair)[None, :, None] + lane_q[None, None, :])
    own_in = np.repeat(own, cg, axis=0).astype(np.float32)
    own_out = np.ascontiguousarray(own_in.transpose(1, 2, 0))

    def in_mat(bb):
        t = jnp.swapaxes(bb.reshape(ndir, nblk, gb, p, cg), -1, -2).reshape(ndir, nblk, gb * cg, p)
        t = jnp.tile(t, (1, 1, 1, 2))[:, :, :, None, :] * own_in
        return t.reshape(ndir, nblk, gb * cg, npair * LANES)

    def out_mat(cc):
        t = jnp.swapaxes(cc.reshape(ndir, nblk, gb * cg, p), -1, -2)
        t = jnp.tile(t, (1, 1, 2, 1))[:, :, None, :, :] * own_out
        return t.reshape(ndir, nblk, npair * LANES, gb * cg)

    bmat = jnp.concatenate([in_mat(b_bar_r), in_mat(b_bar_i)], axis=-1).astype(BF16)
    cmat = jnp.concatenate([out_mat(c_re.astype(F32)), out_mat(-c_im.astype(F32))], axis=-2).astype(BF16)

    def decay(a):
        t = jnp.swapaxes(a.reshape(ndir, nblk, 2, npair // 2, LANES), 2, 3)
        return jnp.tile(t, (1, 1, 1, SUBLANES // 2, 1))
    aa = jnp.stack([decay(a_bar_r), decay(a_bar_i)], axis=2)
    return bmat, cmat, aa


def _glu_kernel(u_ref, yf_ref, yb_ref, d_ref, wv_ref, wg_ref, gf_ref, gs_ref, fo_ref, *rest,
                ni, nj, job):
    if job is None:
        o_ref, a_scr = rest
    else:
        rw_ref, o_ref, ro_ref, a_scr = rest
        step = (pl.program_id(0) * ni + pl.program_id(1)) * nj + pl.program_id(2)
        _cast_step(job, step, rw_ref, ro_ref)

    @pl.when(pl.program_id(2) == 0)
    def _():
        y = (d_ref[...] * u_ref[...].astype(F32) + yf_ref[...].astype(F32)
             + yb_ref[...].astype(F32))
        a_scr[...] = jax.nn.gelu(y).astype(a_scr.dtype)

    a = a_scr[...]
    tn = o_ref.shape[1]
    sub = min(tn, GLU_SUBTILE)
    for n0 in range(0, tn, sub):
        cols = slice(n0, n0 + sub)
        y_s = (_dot(a, wv_ref[:, cols]) * jax.nn.sigmoid(_dot(a, wg_ref[:, cols]))).astype(BF16)
        o_ref[:, cols] = (jax.nn.sigmoid(gf_ref[:, cols]) * fo_ref[:, cols]
                          + jax.nn.sigmoid(gs_ref[:, cols]) * y_s)


def _glu_call(proj, yf, yb, row_off, ssm_d, glu_w, y_f, fw, sw, d, rider=None):
    bs, ls, _ = proj.shape
    tm = _tile(ls, 512)
    tn = _tile(d, 1024)
    assert fw % sw == 0 and (fw + sw) % tn == 0 and row_off % tm == 0
    u_blk = fw // sw
    gf_blk = (fw + sw) // tn
    gs_blk = (fw + sw + d) // tn
    r_blk = row_off // tm
    ni, nj = ls // tm, d // tn
    if rider is not None and not _rider_fits(rider, bs * ni * nj):
        return (_glu_call(proj, yf, yb, row_off, ssm_d, glu_w, y_f, fw, sw, d),
                _pad_cast_call(*rider))
    in_specs = [
        pl.BlockSpec((None, tm, sw), lambda b, i, j: (b, i, u_blk)),
        pl.BlockSpec((None, tm, sw), lambda b, i, j: (b, i + r_blk, 0)),
        pl.BlockSpec((None, tm, sw), lambda b, i, j: (b, i + r_blk, 0)),
        pl.BlockSpec((1, sw), lambda b, i, j: (0, 0)),
        pl.BlockSpec((sw, tn), lambda b, i, j: (0, j)),
        pl.BlockSpec((sw, tn), lambda b, i, j: (0, nj + j)),
        pl.BlockSpec((None, tm, tn), lambda b, i, j: (b, i, gf_blk + j)),
        pl.BlockSpec((None, tm, tn), lambda b, i, j: (b, i, gs_blk + j)),
        pl.BlockSpec((None, tm, tn), lambda b, i, j: (b, i, j)),
    ]
    out_specs = [pl.BlockSpec((None, tm, tn), lambda b, i, j: (b, i, j))]
    out_shape = [jax.ShapeDtypeStruct((bs, ls, d), BF16)]
    args = [proj, yf, yb, ssm_d, glu_w, glu_w, proj, proj, y_f]
    job = None
    if rider is not None:
        job = _attach_rider(rider, lambda b, i, j: (b * ni + i) * nj + j,
                            in_specs, out_specs, out_shape, args)
    out = pl.pallas_call(
        functools.partial(_glu_kernel, ni=ni, nj=nj, job=job),
        grid=(bs, ni, nj),
        in_specs=in_specs,
        out_specs=out_specs,
        out_shape=out_shape,
        scratch_shapes=[pltpu.VMEM((tm, sw), BF16)],
        compiler_params=_params(("arbitrary", "arbitrary", "arbitrary")),
        name="glu_merge",
    )(*args)
    return out if rider is not None else out[0]


def _ffn1_kernel(h_ref, wu_ref, wv_ref, cw_ref, cb_ref, o_ref, *, period):
    h = h_ref[...]
    u = _dot(h, wu_ref[...])
    v = _dot(h, wv_ref[...])
    tm = u.shape[0]
    pos = lax.broadcasted_iota(jnp.int32, u.shape, 0) % period
    prev = jnp.where(pos == 0, 0.0, pltpu.roll(u, 1, axis=0))
    nxt = jnp.where(pos == period - 1, 0.0, pltpu.roll(u, tm - 1, axis=0))
    cw = cw_ref[...]
    conv = cb_ref[...] + cw[0:1] * prev + cw[1:2] * u + cw[2:3] * nxt
    o_ref[...] = (jax.nn.gelu(conv) * v).astype(o_ref.dtype)


def _ffn1_call(h, w12, conv_w, conv_b, period):
    bm, lm, d = h.shape
    ffn = w12.shape[1] // 2
    tm = _tile(lm, 1024)
    tn = _tile(ffn, 512)
    nj = ffn // tn
    assert tm % period == 0
    return pl.pallas_call(
        functools.partial(_ffn1_kernel, period=period),
        grid=(bm, lm // tm, nj),
        in_specs=[
            pl.BlockSpec((None, tm, d), lambda b, i, j: (b, i, 0)),
            pl.BlockSpec((d, tn), lambda b, i, j: (0, j)),
            pl.BlockSpec((d, tn), lambda b, i, j: (0, nj + j)),
            pl.BlockSpec((3, tn), lambda b, i, j: (0, j)),
            pl.BlockSpec((1, tn), lambda b, i, j: (0, j)),
        ],
        out_specs=pl.BlockSpec((None, tm, tn), lambda b, i, j: (b, i, j)),
        out_shape=jax.ShapeDtypeStruct((bm, lm, ffn), BF16),
        compiler_params=_params(("arbitrary", "arbitrary", "arbitrary")),
        name="ffn1",
    )(h, w12, w12, conv_w, conv_b)


def _dft_cos_sin(n):
    idx = np.arange(n, dtype=np.int64)
    ang = (2.0 * math.pi / n) * ((idx[:, None] * idx[None, :]) % n)
    scale = 1.0 / math.sqrt(n)
    return np.cos(ang) * scale, np.sin(ang) * scale


def _bf16_const(a):
    return jnp.asarray(np.asarray(a, np.float32).astype(ml_dtypes.bfloat16))


def kernel(x, c, ctx, c_ctx, ada_w, ada_b, w_in, fourier_w, ssm_a_re, ssm_a_im, ssm_log_dt,
           ssm_b_re, ssm_b_im, ssm_c_re, ssm_c_im, ssm_d, glu_w, w_out, ln1_g, ln1_b,
           ffn_w12, ffn_conv_w, ffn_conv_b, ffn_w2, ln2_g, ln2_b):
    nb, seq, d = x.shape
    clen = ctx.shape[1]
    depth = ada_w.shape[0]
    fw = fourier_w.shape[1]
    sw = ssm_d.shape[1]
    ffn = ffn_conv_b.shape[1]
    alpha = (2.0 * depth) ** 0.25
    gd = fw // FOURIER_GROUPS
    chunk = clen
    assert seq % chunk == 0 and seq % GRID_W == 0
    ffn_pad = 256 if (ffn % 512 == 256) else 0

    ctx_row = nb
    n_rows = -(-(nb + 1) // SUBLANES) * SUBLANES
    cond = jnp.zeros((n_rows, d), F32).at[:nb].set(c).at[ctx_row].set(c_ctx)
    mods_all = _adaln_call(cond, ada_w, ada_b).reshape(depth, n_rows, 1, N_MOD * d)

    cos_c, sin_c = _dft_cos_sin(gd)
    cs_chan = _bf16_const(np.concatenate([cos_c, sin_c], axis=1))

    def pos_dft_matrix(n):
        cos_l, sin_l = _dft_cos_sin(n)
        return _bf16_const(np.concatenate([cos_l, -sin_l], axis=1))[None]

    csl_x = pos_dft_matrix(seq)
    csl_c = pos_dft_matrix(clen)

    def token_mix_tail(proj, csl, yf, yb, row_off, lw):
        ls = proj.shape[1]
        ab = _chan_dft_call(proj, cs_chan, fw)
        ab = ab.reshape(nb, 2 * ls, fw)
        if "fourier_w" not in lw:
            f, lw["fourier_w"] = _mm_call(csl, ab, BF16, "pos_dft", rider=lw.pop("ride_fourier_w"))
        else:
            f = _mm_call(csl, ab, BF16, "pos_dft")
        f = f.reshape(1, nb * ls, fw)
        if "glu_w" not in lw:
            y_f, lw["glu_w"] = _mm_call(f, lw["fourier_w"][None], BF16, "fourier_out",
                                        rider=lw.pop("ride_glu_w"))
        else:
            y_f = _mm_call(f, lw["fourier_w"][None], BF16, "fourier_out")
        y_f = y_f.reshape(nb, ls, d)
        if "w_out" not in lw:
            merged, lw["w_out"] = _glu_call(proj, yf, yb, row_off, lw["ssm_d"], lw["glu_w"], y_f,
                                            fw, sw, d, rider=lw.pop("ride_w_out"))
        else:
            merged = _glu_call(proj, yf, yb, row_off, lw["ssm_d"], lw["glu_w"], y_f, fw, sw, d)
        out = _mm_call(merged.reshape(1, nb * ls, d), lw["w_out"][None], BF16, "w_out")
        return out.reshape(nb, ls, d)

    def conv_ffn(h, lw, period):
        act = _ffn1_call(h, lw["w12"], lw["conv_w"], lw["conv_b"], period)
        return _mm_call(act, lw["w2"][None], BF16, "ffn2", tm_pref=512, tn_pref=512)

    xs = x
    cs = ctx.reshape(1, nb * clen, d)
    hx = _ln_mod_call(xs, mods_all[0], None, 0, 1)
    hc = _ln_mod_call(cs, mods_all[0], ctx_row, 0, 1)
    for i in range(depth):
        last = i == depth - 1
        mods = mods_all[i]
        nxt = None if last else (mods_all[i + 1], 0, 1)
        lw = {
            "ssm_d": ssm_d[i].reshape(1, sw),
            "ride_fourier_w": (_rows_cast(fourier_w, i, _tile(fw, CAST_ROWS)), fourier_w),
            "ride_glu_w": (_rows_cast(glu_w, i, _tile(sw, CAST_ROWS // 2)), glu_w),
            "ride_w_out": (_rows_cast(w_out, i, _tile(d, CAST_ROWS)), w_out),
        }
        ride_w12 = ride_w2 = None
        if ffn_pad:
            ride_w12 = (_pad_halves_cast(ffn_w12, i, ffn_pad), ffn_w12)
            ride_w2 = (_pad_rows_cast(ffn_w2, i, ffn_pad), ffn_w2)
        else:
            lw["w12"] = _cast_call(ffn_w12, i)
            lw["w2"] = _cast_call(ffn_w2, i)
        lw["conv_w"] = jnp.pad(ffn_conv_w[i], ((0, 0), (0, ffn_pad)))
        lw["conv_b"] = jnp.pad(ffn_conv_b[i].reshape(1, ffn), ((0, 0), (0, ffn_pad)))
        g1, b1 = ln1_g[i].reshape(1, d), ln1_b[i].reshape(1, d)
        g2, b2 = ln2_g[i].reshape(1, d), ln2_b[i].reshape(1, d)
        bmat, cmat, amat = _ssm_tables(ssm_a_re[i], ssm_a_im[i], ssm_log_dt[i], ssm_b_re[i],
                                       ssm_b_im[i], ssm_c_re[i], ssm_c_im[i])

        n_in = w_in.shape[2]
        proj_x = _mm_w32_call(hx.reshape(nb * seq, d), w_in, i, 0, n_in, BF16, "w_in", ride_w12)
        if ride_w12 is not None:
            proj_x, lw["w12"] = proj_x
        proj_x = proj_x.reshape(nb, seq, n_in)
        hc2d = hc.reshape(nb * clen, d)
        if last:
            proj_c = _mm_w32_call(hc2d, w_in, i, fw, sw, BF16, "w_in_ctx_ssm").reshape(nb, clen, sw)
            c_col = 0
        else:
            proj_c = _mm_w32_call(hc2d, w_in, i, 0, n_in, BF16, "w_in").reshape(nb, clen, n_in)
            c_col = fw
        ssm_out = _ssm_call(proj_x, fw, proj_c, c_col, bmat, cmat, amat, chunk, ride_w2)
        yf, yb = ssm_out[0], ssm_out[1]
        if ride_w2 is not None:
            lw["w2"] = ssm_out[2]

        out_x = token_mix_tail(proj_x, csl_x, yf, yb, 0, lw)
        x1, h2 = _postnorm_call(xs, out_x, mods, None, 2, g1, b1, alpha, (mods, 3, 4))
        ffn_x = conv_ffn(h2, lw, GRID_W)
        xs, hx = _postnorm_call(x1, ffn_x, mods, None, 5, g2, b2, alpha, nxt)

        if not last:
            out_c = token_mix_tail(proj_c, csl_c, yf, yb, seq, lw).reshape(1, nb * clen, d)
            c1, hc2 = _postnorm_call(cs, out_c, mods, ctx_row, 2, g1, b1, alpha, (mods, 3, 4))
            ffn_c = conv_ffn(hc2, lw, clen)
            cs, hc = _postnorm_call(c1, ffn_c, mods, ctx_row, 5, g2, b2, alpha, nxt)
    return xs
```

```python
import functools
import math
from typing import Callable, NamedTuple

import jax
import jax.numpy as jnp
import ml_dtypes
import numpy as np
from jax import lax
from jax.experimental import pallas as pl
from jax.experimental.pallas import tpu as pltpu

GRID_W = 64
FOURIER_GROUPS = 4
N_MOD = 6
LN_EPS = 1e-6
SSM_BLOCK_GROUPS = 16
LANES = 128
SUBLANES = 8
VMEM_LIMIT = 56 * 1024 * 1024
GLU_SUBTILE = 256
CAST_ROWS = 256
SSM_MM_ROWS = 512

F32 = jnp.float32
BF16 = jnp.bfloat16


def _params(sem):
    return pltpu.CompilerParams(dimension_semantics=sem, vmem_limit_bytes=VMEM_LIMIT)


def _tile(n, pref):
    if n <= pref:
        return n
    while n % pref:
        pref //= 2
    assert pref >= SUBLANES, (n, pref)
    return pref


def _lane_tile(n, cap):
    for t in range(cap - cap % LANES, 0, -LANES):
        if n % t == 0:
            return t
    return n


def _dot(a, b):
    return jnp.dot(a, b, preferred_element_type=F32)


def _layer_norm(x):
    mu = jnp.mean(x, axis=-1, keepdims=True)
    xc = x - mu
    var = jnp.mean(xc * xc, axis=-1, keepdims=True)
    return xc * lax.rsqrt(var + LN_EPS)


def _mod_spec(mod_row, k, d):
    if mod_row is None:
        return pl.BlockSpec((None, 1, d), lambda b, *_: (b, 0, k))
    return pl.BlockSpec((None, 1, d), lambda b, *_: (mod_row, 0, k))


def _cast_kernel(w_ref, o_ref):
    o_ref[...] = w_ref[...].astype(o_ref.dtype)


def _cast_call(w, layer):
    _, r, c = w.shape
    tr = _tile(r, 512)
    tc = _lane_tile(c, 6144)
    return pl.pallas_call(
        _cast_kernel,
        grid=(r // tr, c // tc),
        in_specs=[pl.BlockSpec((None, tr, tc), lambda i, j: (layer, i, j))],
        out_specs=pl.BlockSpec((tr, tc), lambda i, j: (i, j)),
        out_shape=jax.ShapeDtypeStruct((r, c), BF16),
        compiler_params=_params(("arbitrary", "arbitrary")),
        name="cast_w",
    )(w)


class _PadCast(NamedTuple):
    layer: int
    in_block: tuple
    in_index: Callable
    out_block: tuple
    out_index: Callable
    out_shape: tuple
    nblk: int
    is_data: Callable


def _rows_cast(w, layer, tr):
    _, r, c = w.shape
    assert r % tr == 0
    return _PadCast(layer, (None, tr, c), lambda s: (s, 0), (tr, c), lambda s: (s, 0), (r, c),
                    r // tr, lambda s: s >= 0)


def _pad_rows_cast(w, layer, tr):
    _, r, c = w.shape
    nvalid = r // tr
    assert r % tr == 0
    return _PadCast(layer, (None, tr, c), lambda s: (jnp.minimum(s, nvalid - 1), 0),
                    (tr, c), lambda s: (s, 0), (r + tr, c), nvalid + 1, lambda s: s < nvalid)


def _pad_halves_cast(w, layer, tc):
    _, r, c2 = w.shape
    f = c2 // 2
    nvalid = f // tc
    assert f % tc == 0
    per = nvalid + 1
    return _PadCast(layer, (None, r, tc),
                    lambda s: (0, (s // per) * nvalid + jnp.minimum(s % per, nvalid - 1)),
                    (r, tc), lambda s: (0, s), (r, 2 * (f + tc)), 2 * per, lambda s: s % per < nvalid)


def _cast_specs(job, step_of):
    def clamped(*g):
        return jnp.minimum(step_of(*g), job.nblk - 1)

    in_spec = pl.BlockSpec(job.in_block, lambda *g: (job.layer,) + tuple(job.in_index(clamped(*g))))
    out_spec = pl.BlockSpec(job.out_block, lambda *g: tuple(job.out_index(clamped(*g))))
    return in_spec, out_spec


def _cast_step(job, step, w_ref, o_ref):
    @pl.when(step < job.nblk)
    def _():
        o_ref[...] = jnp.where(job.is_data(step), w_ref[...], 0.0).astype(o_ref.dtype)


def _rider_fits(rider, nsteps):
    return rider[0].nblk <= nsteps


def _attach_rider(rider, step_of, in_specs, out_specs, out_shape, args):
    job, rw = rider
    r_in, r_out = _cast_specs(job, step_of)
    in_specs.append(r_in)
    out_specs.append(r_out)
    out_shape.append(jax.ShapeDtypeStruct(job.out_shape, BF16))
    args.append(rw)
    return job


def _pad_cast_kernel(w_ref, o_ref, *, job):
    _cast_step(job, pl.program_id(0), w_ref, o_ref)


def _pad_cast_call(job, w):
    in_spec, out_spec = _cast_specs(job, lambda s: s)
    return pl.pallas_call(
        functools.partial(_pad_cast_kernel, job=job),
        grid=(job.nblk,),
        in_specs=[in_spec],
        out_specs=out_spec,
        out_shape=jax.ShapeDtypeStruct(job.out_shape, BF16),
        compiler_params=_params(("arbitrary",)),
        name="cast_w_pad",
    )(w)


def _adaln_kernel(c_ref, w_ref, b_ref, o_ref):
    c = c_ref[...]
    s = (c * jax.nn.sigmoid(c)).astype(BF16)
    o_ref[...] = _dot(s, w_ref[...].astype(BF16)) + b_ref[...]


def _adaln_call(cond, ada_w, ada_b):
    depth, d, n = ada_w.shape
    r = cond.shape[0]
    tn = _tile(n, 512)
    return pl.pallas_call(
        _adaln_kernel,
        grid=(depth, n // tn),
        in_specs=[
            pl.BlockSpec((r, d), lambda l, j: (0, 0)),
            pl.BlockSpec((None, d, tn), lambda l, j: (l, 0, j)),
            pl.BlockSpec((None, 1, tn), lambda l, j: (l, 0, j)),
        ],
        out_specs=pl.BlockSpec((None, r, tn), lambda l, j: (l, 0, j)),
        out_shape=jax.ShapeDtypeStruct((depth, r, n), F32),
        compiler_params=_params(("arbitrary", "arbitrary")),
        name="adaln",
    )(cond, ada_w, ada_b.reshape(depth, 1, n))


def _ln_mod_kernel(x_ref, sh_ref, sc_ref, o_ref):
    y = _layer_norm(x_ref[...])
    o_ref[...] = (y * (1.0 + sc_ref[...]) + sh_ref[...]).astype(o_ref.dtype)


def _ln_mod_call(x, mods, mod_row, k_shift, k_scale):
    bm, lm, d = x.shape
    tm = _tile(lm, 256)
    return pl.pallas_call(
        _ln_mod_kernel,
        grid=(bm, lm // tm),
        in_specs=[
            pl.BlockSpec((None, tm, d), lambda b, i: (b, i, 0)),
            _mod_spec(mod_row, k_shift, d),
            _mod_spec(mod_row, k_scale, d),
        ],
        out_specs=pl.BlockSpec((None, tm, d), lambda b, i: (b, i, 0)),
        out_shape=jax.ShapeDtypeStruct((bm, lm, d), BF16),
        compiler_params=_params(("arbitrary", "arbitrary")),
        name="ln_mod",
    )(x, mods, mods)


def _postnorm_kernel(res_ref, pre_ref, gate_ref, g_ref, b_ref, *rest, alpha, with_next):
    z = alpha * res_ref[...] + gate_ref[...] * pre_ref[...].astype(F32)
    x_new = _layer_norm(z) * g_ref[...] + b_ref[...]
    if with_next:
        sh_ref, sc_ref, o_ref, h_ref = rest
        o_ref[...] = x_new
        h_ref[...] = (_layer_norm(x_new) * (1.0 + sc_ref[...]) + sh_ref[...]).astype(h_ref.dtype)
    else:
        (o_ref,) = rest
        o_ref[...] = x_new


def _postnorm_call(res, pre, mods, mod_row, k_gate, ln_g, ln_b, alpha, next_mod=None):
    bm, lm, d = res.shape
    tm = _tile(lm, 256)
    row = pl.BlockSpec((None, tm, d), lambda b, i: (b, i, 0))
    vec = pl.BlockSpec((1, d), lambda b, i: (0, 0))
    in_specs = [row, row, _mod_spec(mod_row, k_gate, d), vec, vec]
    args = [res, pre, mods, ln_g, ln_b]
    out_specs = [row]
    out_shape = [jax.ShapeDtypeStruct((bm, lm, d), F32)]
    if next_mod is not None:
        nmods, k_shift, k_scale = next_mod
        in_specs += [_mod_spec(mod_row, k_shift, d), _mod_spec(mod_row, k_scale, d)]
        args += [nmods, nmods]
        out_specs.append(row)
        out_shape.append(jax.ShapeDtypeStruct((bm, lm, d), BF16))
    out = pl.pallas_call(
        functools.partial(_postnorm_kernel, alpha=alpha, with_next=next_mod is not None),
        grid=(bm, lm // tm),
        in_specs=in_specs,
        out_specs=out_specs,
        out_shape=out_shape,
        compiler_params=_params(("arbitrary", "arbitrary")),
        name="postnorm",
    )(*args)
    return (out[0], out[1]) if next_mod is not None else (out[0], None)


def _mm_kernel(a_ref, w_ref, *rest, ni, nj, job):
    if job is None:
        (o_ref,) = rest
    else:
        rw_ref, o_ref, ro_ref = rest
        step = (pl.program_id(0) * ni + pl.program_id(1)) * nj + pl.program_id(2)
        _cast_step(job, step, rw_ref, ro_ref)
    o_ref[...] = _dot(a_ref[...], w_ref[...]).astype(o_ref.dtype)


def _mm_call(a, w, out_dtype, name, tm_pref=1024, tn_pref=1024, rider=None):
    ba, m, k = a.shape
    bw, _, n = w.shape
    nb = max(ba, bw)
    tm = _tile(m, tm_pref)
    tn = _tile(n, tn_pref)
    ni, nj = m // tm, n // tn
    if rider is not None and not _rider_fits(rider, nb * ni * nj):
        return _mm_call(a, w, out_dtype, name, tm_pref, tn_pref), _pad_cast_call(*rider)
    a_map = (lambda b, i, j: (b, i, 0)) if ba > 1 else (lambda b, i, j: (0, i, 0))
    w_map = (lambda b, i, j: (b, 0, j)) if bw > 1 else (lambda b, i, j: (0, 0, j))
    in_specs = [pl.BlockSpec((None, tm, k), a_map), pl.BlockSpec((None, k, tn), w_map)]
    out_specs = [pl.BlockSpec((None, tm, tn), lambda b, i, j: (b, i, j))]
    out_shape = [jax.ShapeDtypeStruct((nb, m, n), out_dtype)]
    args = [a, w]
    job = None
    if rider is not None:
        job = _attach_rider(rider, lambda b, i, j: (b * ni + i) * nj + j,
                            in_specs, out_specs, out_shape, args)
    out = pl.pallas_call(
        functools.partial(_mm_kernel, ni=ni, nj=nj, job=job),
        grid=(nb, ni, nj),
        in_specs=in_specs,
        out_specs=out_specs,
        out_shape=out_shape,
        compiler_params=_params(("arbitrary", "arbitrary", "arbitrary")),
        name=name,
    )(*args)
    return out if rider is not None else out[0]


def _mm_w32_kernel(a_ref, w_ref, *rest, ni, job):
    if job is None:
        o_ref, w_scr = rest
    else:
        rw_ref, o_ref, ro_ref, w_scr = rest
    i = pl.program_id(1)

    @pl.when(i == 0)
    def _():
        w_scr[...] = w_ref[...].astype(w_scr.dtype)

    if job is not None:
        _cast_step(job, pl.program_id(0) * ni + i, rw_ref, ro_ref)
    o_ref[...] = _dot(a_ref[...], w_scr[...]).astype(o_ref.dtype)


def _mm_w32_call(a, w, layer, col0, ncols, out_dtype, name, rider=None):
    m, k = a.shape
    tm = _tile(m, 1024)
    tn = _tile(ncols, 512)
    assert col0 % tn == 0
    j0 = col0 // tn
    ni = m // tm
    in_specs = [
        pl.BlockSpec((tm, k), lambda j, i: (i, 0)),
        pl.BlockSpec((None, k, tn), lambda j, i: (layer, 0, j0 + j)),
    ]
    out_specs = [pl.BlockSpec((tm, tn), lambda j, i: (i, j))]
    out_shape = [jax.ShapeDtypeStruct((m, ncols), out_dtype)]
    args = [a, w]
    job = None
    if rider is not None and not _rider_fits(rider, (ncols // tn) * ni):
        return _mm_w32_call(a, w, layer, col0, ncols, out_dtype, name), _pad_cast_call(*rider)
    if rider is not None:
        job = _attach_rider(rider, lambda j, i: j * ni + i, in_specs, out_specs, out_shape, args)
    out = pl.pallas_call(
        functools.partial(_mm_w32_kernel, ni=ni, job=job),
        grid=(ncols // tn, ni),
        in_specs=in_specs,
        out_specs=out_specs,
        out_shape=out_shape,
        scratch_shapes=[pltpu.VMEM((k, tn), BF16)],
        compiler_params=_params(("arbitrary", "arbitrary")),
        name=name,
    )(*args)
    return out if rider is not None else out[0]


def _chan_dft_kernel(u_ref, cs_ref, o_ref, *, gd):
    r = _dot(u_ref[...], cs_ref[...])
    o_ref[0] = r[:, :gd].astype(o_ref.dtype)
    o_ref[1] = r[:, gd:].astype(o_ref.dtype)


def _chan_dft_call(proj, cs, fw):
    bs, ls, _ = proj.shape
    gd = fw // FOURIER_GROUPS
    tm = _tile(ls, 1024)
    return pl.pallas_call(
        functools.partial(_chan_dft_kernel, gd=gd),
        grid=(bs, ls // tm, FOURIER_GROUPS),
        in_specs=[
            pl.BlockSpec((None, tm, gd), lambda b, i, g: (b, i, g)),
            pl.BlockSpec((gd, 2 * gd), lambda b, i, g: (0, 0)),
        ],
        out_specs=pl.BlockSpec((None, 2, tm, gd), lambda b, i, g: (b, 0, i, g)),
        out_shape=jax.ShapeDtypeStruct((bs, 2, ls, fw), BF16),
        compiler_params=_params(("arbitrary", "arbitrary", "arbitrary")),
        name="chan_dft",
    )(proj, cs)


def _ssm_kernel(uxf_ref, uxb_ref, ucf_ref, ucb_ref, bm_ref, cm_ref, a_ref, *rest,
                chunk, nbat, half, n_ctx, n_all, job):
    side_work = None
    if job is None:
        yf_ref, yb_ref, *scratch = rest
    else:
        rw_ref, yf_ref, yb_ref, ro_ref, *scratch = rest
        step = pl.program_id(0) * n_all + pl.program_id(1)
        side_work = functools.partial(_cast_step, job, step, rw_ref, ro_ref)
    _ssm_step(uxf_ref, uxb_ref, ucf_ref, ucb_ref, bm_ref, cm_ref, a_ref, yf_ref, yb_ref,
              *scratch, chunk=chunk, nbat=nbat, half=half, n_ctx=n_ctx, side_work=side_work)


def _ssm_step(uxf_ref, uxb_ref, ucf_ref, ucb_ref, bm_ref, cm_ref, a_ref, yf_ref, yb_ref,
              u_scr, uin_scr, s_scr, y_scr, h_scr, *, chunk, nbat, half, n_ctx, side_work):
    k = pl.program_id(0)
    c = pl.program_id(1)
    rows = SUBLANES
    hw = half // 2
    mm_rows = min(chunk * rows, SSM_MM_ROWS)

    @pl.when(jnp.logical_and(k == 0, c == 0))
    def _():
        uin_scr[...] = jnp.zeros_like(uin_scr)

    @pl.when(c == 0)
    def _():
        h_scr[...] = jnp.zeros_like(h_scr)

    @pl.when(c < n_ctx)
    def _():
        u_scr[0] = ucf_ref[...]
        u_scr[1] = ucb_ref[...]

    @pl.when(c >= n_ctx)
    def _():
        u_scr[0] = uxf_ref[...]
        u_scr[1] = uxb_ref[...]

    if side_work is not None:
        side_work()
    for d in range(2):
        for b in range(nbat):
            ub = u_scr[d, b].astype(F32)
            for h in range(2):
                uin_scr[d, h, pl.ds(2 * b + h, chunk, stride=rows), :] = ub[:, h * LANES:(h + 1) * LANES]
        bm = bm_ref[d]
        for r0 in range(0, chunk * rows, mm_rows):
            rs = slice(r0, r0 + mm_rows)
            lhs = jnp.concatenate([uin_scr[d, 0, rs], uin_scr[d, 1, rs]], axis=1).astype(BF16)
            s_scr[d, rs] = _dot(lhs, bm)

    arf, aif = a_ref[0, 0], a_ref[0, 1]
    arb, aib = a_ref[1, 0], a_ref[1, 1]

    def body(t, carry):
        hrf, hif, hrb, hib = carry
        rowf = pl.multiple_of(t * rows, rows)
        rowb = pl.multiple_of((chunk - 1 - t) * rows, rows)
        xf = s_scr[0, pl.ds(rowf, rows), :]
        xb = s_scr[1, pl.ds(rowb, rows), :]
        nrf = arf * hrf - aif * hif + xf[:, :hw]
        nif = arf * hif + aif * hrf + xf[:, hw:]
        nrb = arb * hrb - aib * hib + xb[:, :hw]
        nib = arb * hib + aib * hrb + xb[:, hw:]
        s_scr[0, pl.ds(rowf, rows), :hw] = nrf
        s_scr[0, pl.ds(rowf, rows), hw:] = nif
        s_scr[1, pl.ds(rowb, rows), :hw] = nrb
        s_scr[1, pl.ds(rowb, rows), hw:] = nib
        return nrf, nif, nrb, nib

    init = (h_scr[0, :, :hw], h_scr[0, :, hw:], h_scr[1, :, :hw], h_scr[1, :, hw:])
    hrf, hif, hrb, hib = lax.fori_loop(0, chunk, body, init, unroll=8)
    h_scr[0, :, :hw] = hrf
    h_scr[0, :, hw:] = hif
    h_scr[1, :, :hw] = hrb
    h_scr[1, :, hw:] = hib

    for d, y_ref in enumerate((yf_ref, yb_ref)):
        cm = cm_ref[d]
        for r0 in range(0, chunk * rows, mm_rows):
            rs = slice(r0, r0 + mm_rows)
            out = _dot(s_scr[d, rs].astype(BF16), cm)
            y_scr[0, rs] = out[:, :LANES]
            y_scr[1, rs] = out[:, LANES:]
        for b in range(nbat):
            for h in range(2):
                piece = y_scr[h, pl.ds(2 * b + h, chunk, stride=rows), :]
                y_ref[b, :, h * LANES:(h + 1) * LANES] = piece.astype(y_ref.dtype)


def _ssm_call(ux, ux_col, uc, uc_col, bmat, cmat, amat, chunk, rider=None):
    nbat, seq, _ = ux.shape
    clen = uc.shape[1]
    assert 2 * nbat == SUBLANES, "the scan packs 2 column halves x batch on the 8 sublanes"
    ndir, nblk, cin, cst = bmat.shape
    assert cin == 2 * LANES, "each column half of a group block is one 128-lane input slab"
    sw = nblk * cin
    n_lat, n_ctx = seq // chunk, clen // chunk
    n_all = n_lat + n_ctx
    assert ux_col % cin == 0 and uc_col % cin == 0
    xo, co = ux_col // cin, uc_col // cin

    def lat_f(c):
        return jnp.maximum(c - n_ctx, 0)

    def lat_b(c):
        return n_lat - 1 - jnp.maximum(c - n_ctx, 0)

    def ctx_f(c):
        return jnp.minimum(c, n_ctx - 1)

    def ctx_b(c):
        return jnp.maximum(n_ctx - 1 - c, 0)

    def out_f(c):
        return jnp.where(c < n_ctx, n_lat + c, c - n_ctx)

    def out_b(c):
        return jnp.where(c < n_ctx, n_lat + n_ctx - 1 - c, n_lat - 1 - (c - n_ctx))

    ublk = (nbat, chunk, cin)
    in_specs = [
        pl.BlockSpec(ublk, lambda k, c: (0, lat_f(c), xo + k)),
        pl.BlockSpec(ublk, lambda k, c: (0, lat_b(c), xo + k)),
        pl.BlockSpec(ublk, lambda k, c: (0, ctx_f(c), co + k)),
        pl.BlockSpec(ublk, lambda k, c: (0, ctx_b(c), co + k)),
        pl.BlockSpec((ndir, None, cin, cst), lambda k, c: (0, k, 0, 0)),
        pl.BlockSpec((ndir, None, cst, cin), lambda k, c: (0, k, 0, 0)),
        pl.BlockSpec((ndir, None, 2, SUBLANES, cst // 2), lambda k, c: (0, k, 0, 0, 0)),
    ]
    out_specs = [
        pl.BlockSpec(ublk, lambda k, c: (0, out_f(c), k)),
        pl.BlockSpec(ublk, lambda k, c: (0, out_b(c), k)),
    ]
    out_shape = [jax.ShapeDtypeStruct((nbat, seq + clen, sw), BF16)] * 2
    args = [ux, ux, uc, uc, bmat, cmat, amat]
    job = None
    if rider is not None and not _rider_fits(rider, nblk * n_all):
        yf, yb = _ssm_call(ux, ux_col, uc, uc_col, bmat, cmat, amat, chunk)
        return yf, yb, _pad_cast_call(*rider)
    if rider is not None:
        job = _attach_rider(rider, lambda k, c: k * n_all + c, in_specs, out_specs, out_shape, args)
    return pl.pallas_call(
        functools.partial(_ssm_kernel, chunk=chunk, nbat=nbat, half=cst, n_ctx=n_ctx,
                          n_all=n_all, job=job),
        grid=(nblk, n_all),
        in_specs=in_specs,
        out_specs=out_specs,
        out_shape=out_shape,
        scratch_shapes=[
            pltpu.VMEM((ndir, nbat, chunk, cin), BF16),
            pltpu.VMEM((ndir, 2, SUBLANES * chunk, LANES), F32),
            pltpu.VMEM((ndir, SUBLANES * chunk, cst), F32),
            pltpu.VMEM((2, SUBLANES * chunk, LANES), F32),
            pltpu.VMEM((ndir, SUBLANES, cst), F32),
        ],
        compiler_params=_params(("arbitrary", "arbitrary")),
        name="ssm",
    )(*args)


def _ssm_tables(a_re, a_im, log_dt, b_re, b_im, c_re, c_im):
    ndir, g, p = a_re.shape
    cg = b_re.shape[-1]
    gb = SSM_BLOCK_GROUPS
    nblk = g // gb
    hg = gb // 2
    npair = hg // 2
    assert 2 * p == LANES and g % gb == 0 and hg * cg == LANES
    lr, li = a_re.astype(F32), a_im.astype(F32)
    dt = jnp.exp(log_dt.astype(F32))[..., None]
    zr, zi = lr * dt, li * dt
    a_bar_r = jnp.exp(zr) * jnp.cos(zi)
    a_bar_i = jnp.exp(zr) * jnp.sin(zi)
    em1_r = jnp.expm1(zr) * jnp.cos(zi) - 2.0 * jnp.square(jnp.sin(0.5 * zi))
    den = lr * lr + li * li
    cf_r = ((em1_r * lr + a_bar_i * li) / den)[..., None]
    cf_i = ((a_bar_i * lr - em1_r * li) / den)[..., None]
    br, bi = b_re.astype(F32), b_im.astype(F32)
    b_bar_r = cf_r * br - cf_i * bi
    b_bar_i = cf_r * bi + cf_i * br

    lane_q = np.arange(LANES) // p
    own = (np.arange(gb)[:, None, None] % hg
           == 2 * np.arange(npair)[None, :, None] + lane_q[None, None, :])
    own_in = np.repeat(own, cg, axis=0).astype(np.float32)
    own_out = np.ascontiguousarray(own_in.transpose(1, 2, 0))

    def in_mat(bb):
        t = jnp.swapaxes(bb.reshape(ndir, nblk, gb, p, cg), -1, -2).reshape(ndir, nblk, gb * cg, p)
        t = jnp.tile(t, (1, 1, 1, 2))[:, :, :, None, :] * own_in
        return t.reshape(ndir, nblk, gb * cg, npair * LANES)

    def out_mat(cc):
        t = jnp.swapaxes(cc.reshape(ndir, nblk, gb * cg, p), -1, -2)
        t = jnp.tile(t, (1, 1, 2, 1))[:, :, None, :, :] * own_out
        return t.reshape(ndir, nblk, npair * LANES, gb * cg)

    bmat = jnp.concatenate([in_mat(b_bar_r), in_mat(b_bar_i)], axis=-1).astype(BF16)
    cmat = jnp.concatenate([out_mat(c_re.astype(F32)), out_mat(-c_im.astype(F32))], axis=-2).astype(BF16)

    def decay(a):
        t = a.reshape(ndir, nblk, 2, npair * LANES)
        return jnp.tile(t, (1, 1, SUBLANES // 2, 1))
    aa = jnp.stack([decay(a_bar_r), decay(a_bar_i)], axis=2)
    return bmat, cmat, aa


def _glu_kernel(u_ref, yf_ref, yb_ref, d_ref, wv_ref, wg_ref, gf_ref, gs_ref, fo_ref, *rest,
                ni, nj, job):
    if job is None:
        o_ref, a_scr = rest
    else:
        rw_ref, o_ref, ro_ref, a_scr = rest
        step = (pl.program_id(0) * ni + pl.program_id(1)) * nj + pl.program_id(2)
        _cast_step(job, step, rw_ref, ro_ref)

    @pl.when(pl.program_id(2) == 0)
    def _():
        y = (d_ref[...] * u_ref[...].astype(F32) + yf_ref[...].astype(F32)
             + yb_ref[...].astype(F32))
        a_scr[...] = jax.nn.gelu(y).astype(a_scr.dtype)

    a = a_scr[...]
    tn = o_ref.shape[1]
    sub = min(tn, GLU_SUBTILE)
    for n0 in range(0, tn, sub):
        cols = slice(n0, n0 + sub)
        y_s = (_dot(a, wv_ref[:, cols]) * jax.nn.sigmoid(_dot(a, wg_ref[:, cols]))).astype(BF16)
        o_ref[:, cols] = (jax.nn.sigmoid(gf_ref[:, cols]) * fo_ref[:, cols]
                          + jax.nn.sigmoid(gs_ref[:, cols]) * y_s)


def _glu_call(proj, yf, yb, row_off, ssm_d, glu_w, y_f, fw, sw, d, rider=None):
    bs, ls, _ = proj.shape
    tm = _tile(ls, 512)
    tn = _tile(d, 1024)
    assert fw % sw == 0 and (fw + sw) % tn == 0 and row_off % tm == 0
    u_blk = fw // sw
    gf_blk = (fw + sw) // tn
    gs_blk = (fw + sw + d) // tn
    r_blk = row_off // tm
    ni, nj = ls // tm, d // tn
    if rider is not None and not _rider_fits(rider, bs * ni * nj):
        return (_glu_call(proj, yf, yb, row_off, ssm_d, glu_w, y_f, fw, sw, d),
                _pad_cast_call(*rider))
    in_specs = [
        pl.BlockSpec((None, tm, sw), lambda b, i, j: (b, i, u_blk)),
        pl.BlockSpec((None, tm, sw), lambda b, i, j: (b, i + r_blk, 0)),
        pl.BlockSpec((None, tm, sw), lambda b, i, j: (b, i + r_blk, 0)),
        pl.BlockSpec((1, sw), lambda b, i, j: (0, 0)),
        pl.BlockSpec((sw, tn), lambda b, i, j: (0, j)),
        pl.BlockSpec((sw, tn), lambda b, i, j: (0, nj + j)),
        pl.BlockSpec((None, tm, tn), lambda b, i, j: (b, i, gf_blk + j)),
        pl.BlockSpec((None, tm, tn), lambda b, i, j: (b, i, gs_blk + j)),
        pl.BlockSpec((None, tm, tn), lambda b, i, j: (b, i, j)),
    ]
    out_specs = [pl.BlockSpec((None, tm, tn), lambda b, i, j: (b, i, j))]
    out_shape = [jax.ShapeDtypeStruct((bs, ls, d), BF16)]
    args = [proj, yf, yb, ssm_d, glu_w, glu_w, proj, proj, y_f]
    job = None
    if rider is not None:
        job = _attach_rider(rider, lambda b, i, j: (b * ni + i) * nj + j,
                            in_specs, out_specs, out_shape, args)
    out = pl.pallas_call(
        functools.partial(_glu_kernel, ni=ni, nj=nj, job=job),
        grid=(bs, ni, nj),
        in_specs=in_specs,
        out_specs=out_specs,
        out_shape=out_shape,
        scratch_shapes=[pltpu.VMEM((tm, sw), BF16)],
        compiler_params=_params(("arbitrary", "arbitrary", "arbitrary")),
        name="glu_merge",
    )(*args)
    return out if rider is not None else out[0]


def _ffn1_kernel(h_ref, wu_ref, wv_ref, cw_ref, cb_ref, *rest, period, ni, nj, job):
    if job is None:
        (o_ref,) = rest
    else:
        rw_ref, o_ref, ro_ref = rest
        step = (pl.program_id(0) * ni + pl.program_id(1)) * nj + pl.program_id(2)
        _cast_step(job, step, rw_ref, ro_ref)
    h = h_ref[...]
    u = _dot(h, wu_ref[...])
    v = _dot(h, wv_ref[...])
    tm = u.shape[0]
    pos = lax.broadcasted_iota(jnp.int32, u.shape, 0) % period
    prev = jnp.where(pos == 0, 0.0, pltpu.roll(u, 1, axis=0))
    nxt = jnp.where(pos == period - 1, 0.0, pltpu.roll(u, tm - 1, axis=0))
    cw = cw_ref[...]
    conv = cb_ref[...] + cw[0:1] * prev + cw[1:2] * u + cw[2:3] * nxt
    o_ref[...] = (jax.nn.gelu(conv) * v).astype(o_ref.dtype)


def _ffn1_call(h, w12, conv_w, conv_b, period, rider=None):
    bm, lm, d = h.shape
    ffn = w12.shape[1] // 2
    tm = _tile(lm, 1024)
    tn = _tile(ffn, 512)
    ni, nj = lm // tm, ffn // tn
    assert tm % period == 0
    if rider is not None and not _rider_fits(rider, bm * ni * nj):
        return _ffn1_call(h, w12, conv_w, conv_b, period), _pad_cast_call(*rider)
    in_specs = [
        pl.BlockSpec((None, tm, d), lambda b, i, j: (b, i, 0)),
        pl.BlockSpec((d, tn), lambda b, i, j: (0, j)),
        pl.BlockSpec((d, tn), lambda b, i, j: (0, nj + j)),
        pl.BlockSpec((3, tn), lambda b, i, j: (0, j)),
        pl.BlockSpec((1, tn), lambda b, i, j: (0, j)),
    ]
    out_specs = [pl.BlockSpec((None, tm, tn), lambda b, i, j: (b, i, j))]
    out_shape = [jax.ShapeDtypeStruct((bm, lm, ffn), BF16)]
    args = [h, w12, w12, conv_w, conv_b]
    job = None
    if rider is not None:
        job = _attach_rider(rider, lambda b, i, j: (b * ni + i) * nj + j,
                            in_specs, out_specs, out_shape, args)
    out = pl.pallas_call(
        functools.partial(_ffn1_kernel, period=period, ni=ni, nj=nj, job=job),
        grid=(bm, ni, nj),
        in_specs=in_specs,
        out_specs=out_specs,
        out_shape=out_shape,
        compiler_params=_params(("arbitrary", "arbitrary", "arbitrary")),
        name="ffn1",
    )(*args)
    return out if rider is not None else out[0]


def _dft_cos_sin(n):
    idx = np.arange(n, dtype=np.int64)
    ang = (2.0 * math.pi / n) * ((idx[:, None] * idx[None, :]) % n)
    scale = 1.0 / math.sqrt(n)
    return np.cos(ang) * scale, np.sin(ang) * scale


def _bf16_const(a):
    return jnp.asarray(np.asarray(a, np.float32).astype(ml_dtypes.bfloat16))


def kernel(x, c, ctx, c_ctx, ada_w, ada_b, w_in, fourier_w, ssm_a_re, ssm_a_im, ssm_log_dt,
           ssm_b_re, ssm_b_im, ssm_c_re, ssm_c_im, ssm_d, glu_w, w_out, ln1_g, ln1_b,
           ffn_w12, ffn_conv_w, ffn_conv_b, ffn_w2, ln2_g, ln2_b):
    nb, seq, d = x.shape
    clen = ctx.shape[1]
    depth = ada_w.shape[0]
    fw = fourier_w.shape[1]
    sw = ssm_d.shape[1]
    ffn = ffn_conv_b.shape[1]
    alpha = (2.0 * depth) ** 0.25
    gd = fw // FOURIER_GROUPS
    chunk = clen
    assert seq % chunk == 0 and seq % GRID_W == 0
    ffn_pad = 256 if (ffn % 512 == 256) else 0

    ctx_row = nb
    n_rows = -(-(nb + 1) // SUBLANES) * SUBLANES
    cond = jnp.zeros((n_rows, d), F32).at[:nb].set(c).at[ctx_row].set(c_ctx)
    mods_all = _adaln_call(cond, ada_w, ada_b).reshape(depth, n_rows, 1, N_MOD * d)

    cos_c, sin_c = _dft_cos_sin(gd)
    cs_chan = _bf16_const(np.concatenate([cos_c, sin_c], axis=1))

    def pos_dft_matrix(n):
        cos_l, sin_l = _dft_cos_sin(n)
        return _bf16_const(np.concatenate([cos_l, -sin_l], axis=1))[None]

    csl_x = pos_dft_matrix(seq)
    csl_c = pos_dft_matrix(clen)

    def token_mix_tail(proj, csl, yf, yb, row_off, lw):
        ls = proj.shape[1]
        ab = _chan_dft_call(proj, cs_chan, fw)
        ab = ab.reshape(nb, 2 * ls, fw)
        if "fourier_w" not in lw:
            f, lw["fourier_w"] = _mm_call(csl, ab, BF16, "pos_dft", rider=lw.pop("ride_fourier_w"))
        else:
            f = _mm_call(csl, ab, BF16, "pos_dft")
        f = f.reshape(1, nb * ls, fw)
        if "glu_w" not in lw:
            y_f, lw["glu_w"] = _mm_call(f, lw["fourier_w"][None], BF16, "fourier_out",
                                        rider=lw.pop("ride_glu_w"))
        else:
            y_f = _mm_call(f, lw["fourier_w"][None], BF16, "fourier_out")
        y_f = y_f.reshape(nb, ls, d)
        if "w_out" not in lw:
            merged, lw["w_out"] = _glu_call(proj, yf, yb, row_off, lw["ssm_d"], lw["glu_w"], y_f,
                                            fw, sw, d, rider=lw.pop("ride_w_out"))
        else:
            merged = _glu_call(proj, yf, yb, row_off, lw["ssm_d"], lw["glu_w"], y_f, fw, sw, d)
        merged = merged.reshape(1, nb * ls, d)
        if "ride_next_w_in" in lw:
            out, nxt_w["w_in"] = _mm_call(merged, lw["w_out"][None], BF16, "w_out", tn_pref=512,
                                          rider=lw.pop("ride_next_w_in"))
        else:
            out = _mm_call(merged, lw["w_out"][None], BF16, "w_out")
        return out.reshape(nb, ls, d)

    def conv_ffn(h, lw, period):
        if "ride_next_w12" in lw:
            act, nxt_w["w12"] = _ffn1_call(h, lw["w12"], lw["conv_w"], lw["conv_b"], period,
                                           rider=lw.pop("ride_next_w12"))
        else:
            act = _ffn1_call(h, lw["w12"], lw["conv_w"], lw["conv_b"], period)
        return _mm_call(act, lw["w2"][None], BF16, "ffn2", tm_pref=512, tn_pref=512)

    xs = x
    cs = ctx.reshape(1, nb * clen, d)
    nxt_w = {}
    hx = _ln_mod_call(xs, mods_all[0], None, 0, 1)
    hc = _ln_mod_call(cs, mods_all[0], ctx_row, 0, 1)
    for i in range(depth):
        last = i == depth - 1
        mods = mods_all[i]
        nxt = None if last else (mods_all[i + 1], 0, 1)
        lw = {
            "ssm_d": ssm_d[i].reshape(1, sw),
            "ride_fourier_w": (_rows_cast(fourier_w, i, _tile(fw, CAST_ROWS)), fourier_w),
            "ride_glu_w": (_rows_cast(glu_w, i, _tile(sw, CAST_ROWS // 2)), glu_w),
            "ride_w_out": (_rows_cast(w_out, i, _tile(d, CAST_ROWS)), w_out),
        }
        cur_w, nxt_w = nxt_w, {}
        ride_w12 = ride_w2 = None
        if ffn_pad:
            if "w12" in cur_w:
                lw["w12"] = cur_w["w12"]
            else:
                ride_w12 = (_pad_halves_cast(ffn_w12, i, ffn_pad), ffn_w12)
            ride_w2 = (_pad_rows_cast(ffn_w2, i, ffn_pad), ffn_w2)
            if not last:
                lw["ride_next_w_in"] = (_rows_cast(w_in, i + 1, _tile(d, CAST_ROWS // 4)), w_in)
                lw["ride_next_w12"] = (_pad_halves_cast(ffn_w12, i + 1, ffn_pad), ffn_w12)
        else:
            lw["w12"] = _cast_call(ffn_w12, i)
            lw["w2"] = _cast_call(ffn_w2, i)
        lw["conv_w"] = jnp.pad(ffn_conv_w[i], ((0, 0), (0, ffn_pad)))
        lw["conv_b"] = jnp.pad(ffn_conv_b[i].reshape(1, ffn), ((0, 0), (0, ffn_pad)))
        g1, b1 = ln1_g[i].reshape(1, d), ln1_b[i].reshape(1, d)
        g2, b2 = ln2_g[i].reshape(1, d), ln2_b[i].reshape(1, d)
        bmat, cmat, amat = _ssm_tables(ssm_a_re[i], ssm_a_im[i], ssm_log_dt[i], ssm_b_re[i],
                                       ssm_b_im[i], ssm_c_re[i], ssm_c_im[i])

        n_in = w_in.shape[2]
        hc2d = hc.reshape(nb * clen, d)
        c_col = 0 if last else fw
        if "w_in" in cur_w:
            w_in_b = cur_w["w_in"]
            proj_x = _mm_call(hx.reshape(1, nb * seq, d), w_in_b[None], BF16, "w_in", rider=ride_w12)
            w_in_c = w_in_b[None, :, fw:fw + sw] if last else w_in_b[None]
            proj_c = _mm_call(hc2d[None], w_in_c, BF16, "w_in_ctx")
        else:
            proj_x = _mm_w32_call(hx.reshape(nb * seq, d), w_in, i, 0, n_in, BF16, "w_in", ride_w12)
            if last:
                proj_c = _mm_w32_call(hc2d, w_in, i, fw, sw, BF16, "w_in_ctx")
            else:
                proj_c = _mm_w32_call(hc2d, w_in, i, 0, n_in, BF16, "w_in_ctx")
        if ride_w12 is not None:
            proj_x, lw["w12"] = proj_x
        proj_x = proj_x.reshape(nb, seq, n_in)
        proj_c = proj_c.reshape(nb, clen, -1)
        ssm_out = _ssm_call(proj_x, fw, proj_c, c_col, bmat, cmat, amat, chunk, ride_w2)
        yf, yb = ssm_out[0], ssm_out[1]
        if ride_w2 is not None:
            lw["w2"] = ssm_out[2]

        out_x = token_mix_tail(proj_x, csl_x, yf, yb, 0, lw)
        x1, h2 = _postnorm_call(xs, out_x, mods, None, 2, g1, b1, alpha, (mods, 3, 4))
        ffn_x = conv_ffn(h2, lw, GRID_W)
        xs, hx = _postnorm_call(x1, ffn_x, mods, None, 5, g2, b2, alpha, nxt)

        if not last:
            out_c = token_mix_tail(proj_c, csl_c, yf, yb, seq, lw).reshape(1, nb * clen, d)
            c1, hc2 = _postnorm_call(cs, out_c, mods, ctx_row, 2, g1, b1, alpha, (mods, 3, 4))
            ffn_c = conv_ffn(hc2, lw, clen)
            cs, hc = _postnorm_call(c1, ffn_c, mods, ctx_row, 5, g2, b2, alpha, nxt)
    return xs
```

```python
import functools
import math
from typing import Callable, NamedTuple

import jax
import jax.numpy as jnp
import ml_dtypes
import numpy as np
from jax import lax
from jax.experimental import pallas as pl
from jax.experimental.pallas import tpu as pltpu

GRID_W = 64
FOURIER_GROUPS = 4
N_MOD = 6
LN_EPS = 1e-6
SSM_BLOCK_GROUPS = 16
LANES = 128
SUBLANES = 8
VMEM_LIMIT = 56 * 1024 * 1024
GLU_SUBTILE = 256
CAST_ROWS = 256
SSM_MM_ROWS = 512

F32 = jnp.float32
BF16 = jnp.bfloat16


def _params(sem):
    return pltpu.CompilerParams(dimension_semantics=sem, vmem_limit_bytes=VMEM_LIMIT)


def _tile(n, pref):
    if n <= pref:
        return n
    while n % pref:
        pref //= 2
    assert pref >= SUBLANES, (n, pref)
    return pref


def _lane_tile(n, cap):
    for t in range(cap - cap % LANES, 0, -LANES):
        if n % t == 0:
            return t
    return n


def _dot(a, b):
    return jnp.dot(a, b, preferred_element_type=F32)


def _layer_norm(x):
    mu = jnp.mean(x, axis=-1, keepdims=True)
    xc = x - mu
    var = jnp.mean(xc * xc, axis=-1, keepdims=True)
    return xc * lax.rsqrt(var + LN_EPS)


def _mod_spec(mod_row, k, d):
    if mod_row is None:
        return pl.BlockSpec((None, 1, d), lambda b, *_: (b, 0, k))
    return pl.BlockSpec((None, 1, d), lambda b, *_: (mod_row, 0, k))


def _cast_kernel(w_ref, o_ref):
    o_ref[...] = w_ref[...].astype(o_ref.dtype)


def _cast_call(w, layer):
    _, r, c = w.shape
    tr = _tile(r, 512)
    tc = _lane_tile(c, 6144)
    return pl.pallas_call(
        _cast_kernel,
        grid=(r // tr, c // tc),
        in_specs=[pl.BlockSpec((None, tr, tc), lambda i, j: (layer, i, j))],
        out_specs=pl.BlockSpec((tr, tc), lambda i, j: (i, j)),
        out_shape=jax.ShapeDtypeStruct((r, c), BF16),
        compiler_params=_params(("arbitrary", "arbitrary")),
        name="cast_w",
    )(w)


class _PadCast(NamedTuple):
    layer: int
    in_block: tuple
    in_index: Callable
    out_block: tuple
    out_index: Callable
    out_shape: tuple
    nblk: int
    is_data: Callable


def _rows_cast(w, layer, tr):
    _, r, c = w.shape
    assert r % tr == 0
    return _PadCast(layer, (None, tr, c), lambda s: (s, 0), (tr, c), lambda s: (s, 0), (r, c),
                    r // tr, lambda s: s >= 0)


def _pad_rows_cast(w, layer, tr):
    _, r, c = w.shape
    nvalid = r // tr
    assert r % tr == 0
    return _PadCast(layer, (None, tr, c), lambda s: (jnp.minimum(s, nvalid - 1), 0),
                    (tr, c), lambda s: (s, 0), (r + tr, c), nvalid + 1, lambda s: s < nvalid)


def _pad_halves_cast(w, layer, tc):
    _, r, c2 = w.shape
    f = c2 // 2
    nvalid = f // tc
    assert f % tc == 0
    per = nvalid + 1
    return _PadCast(layer, (None, r, tc),
                    lambda s: (0, (s // per) * nvalid + jnp.minimum(s % per, nvalid - 1)),
                    (r, tc), lambda s: (0, s), (r, 2 * (f + tc)), 2 * per, lambda s: s % per < nvalid)


def _cast_specs(job, step_of):
    def clamped(*g):
        return jnp.minimum(step_of(*g), job.nblk - 1)

    in_spec = pl.BlockSpec(job.in_block, lambda *g: (job.layer,) + tuple(job.in_index(clamped(*g))))
    out_spec = pl.BlockSpec(job.out_block, lambda *g: tuple(job.out_index(clamped(*g))))
    return in_spec, out_spec


def _cast_step(job, step, w_ref, o_ref):
    @pl.when(step < job.nblk)
    def _():
        o_ref[...] = jnp.where(job.is_data(step), w_ref[...], 0.0).astype(o_ref.dtype)


def _rider_fits(rider, nsteps):
    return rider[0].nblk <= nsteps


def _attach_rider(rider, step_of, in_specs, out_specs, out_shape, args):
    job, rw = rider
    r_in, r_out = _cast_specs(job, step_of)
    in_specs.append(r_in)
    out_specs.append(r_out)
    out_shape.append(jax.ShapeDtypeStruct(job.out_shape, BF16))
    args.append(rw)
    return job


def _pad_cast_kernel(w_ref, o_ref, *, job):
    _cast_step(job, pl.program_id(0), w_ref, o_ref)


def _pad_cast_call(job, w):
    in_spec, out_spec = _cast_specs(job, lambda s: s)
    return pl.pallas_call(
        functools.partial(_pad_cast_kernel, job=job),
        grid=(job.nblk,),
        in_specs=[in_spec],
        out_specs=out_spec,
        out_shape=jax.ShapeDtypeStruct(job.out_shape, BF16),
        compiler_params=_params(("arbitrary",)),
        name="cast_w_pad",
    )(w)


def _adaln_kernel(c_ref, w_ref, b_ref, o_ref):
    c = c_ref[...]
    s = (c * jax.nn.sigmoid(c)).astype(BF16)
    o_ref[...] = _dot(s, w_ref[...].astype(BF16)) + b_ref[...]


def _adaln_call(cond, ada_w, ada_b):
    depth, d, n = ada_w.shape
    r = cond.shape[0]
    tn = _tile(n, 512)
    return pl.pallas_call(
        _adaln_kernel,
        grid=(depth, n // tn),
        in_specs=[
            pl.BlockSpec((r, d), lambda l, j: (0, 0)),
            pl.BlockSpec((None, d, tn), lambda l, j: (l, 0, j)),
            pl.BlockSpec((None, 1, tn), lambda l, j: (l, 0, j)),
        ],
        out_specs=pl.BlockSpec((None, r, tn), lambda l, j: (l, 0, j)),
        out_shape=jax.ShapeDtypeStruct((depth, r, n), F32),
        compiler_params=_params(("arbitrary", "arbitrary")),
        name="adaln",
    )(cond, ada_w, ada_b.reshape(depth, 1, n))


def _ln_mod_kernel(x_ref, sh_ref, sc_ref, o_ref):
    y = _layer_norm(x_ref[...])
    o_ref[...] = (y * (1.0 + sc_ref[...]) + sh_ref[...]).astype(o_ref.dtype)


def _ln_mod_call(x, mods, mod_row, k_shift, k_scale):
    bm, lm, d = x.shape
    tm = _tile(lm, 256)
    return pl.pallas_call(
        _ln_mod_kernel,
        grid=(bm, lm // tm),
        in_specs=[
            pl.BlockSpec((None, tm, d), lambda b, i: (b, i, 0)),
            _mod_spec(mod_row, k_shift, d),
            _mod_spec(mod_row, k_scale, d),
        ],
        out_specs=pl.BlockSpec((None, tm, d), lambda b, i: (b, i, 0)),
        out_shape=jax.ShapeDtypeStruct((bm, lm, d), BF16),
        compiler_params=_params(("arbitrary", "arbitrary")),
        name="ln_mod",
    )(x, mods, mods)


def _postnorm_kernel(res_ref, pre_ref, gate_ref, g_ref, b_ref, *rest, alpha, with_next):
    z = alpha * res_ref[...] + gate_ref[...] * pre_ref[...].astype(F32)
    x_new = _layer_norm(z) * g_ref[...] + b_ref[...]
    if with_next:
        sh_ref, sc_ref, o_ref, h_ref = rest
        o_ref[...] = x_new
        h_ref[...] = (_layer_norm(x_new) * (1.0 + sc_ref[...]) + sh_ref[...]).astype(h_ref.dtype)
    else:
        (o_ref,) = rest
        o_ref[...] = x_new


def _postnorm_call(res, pre, mods, mod_row, k_gate, ln_g, ln_b, alpha, next_mod=None):
    bm, lm, d = res.shape
    tm = _tile(lm, 256)
    row = pl.BlockSpec((None, tm, d), lambda b, i: (b, i, 0))
    vec = pl.BlockSpec((1, d), lambda b, i: (0, 0))
    in_specs = [row, row, _mod_spec(mod_row, k_gate, d), vec, vec]
    args = [res, pre, mods, ln_g, ln_b]
    out_specs = [row]
    out_shape = [jax.ShapeDtypeStruct((bm, lm, d), F32)]
    if next_mod is not None:
        nmods, k_shift, k_scale = next_mod
        in_specs += [_mod_spec(mod_row, k_shift, d), _mod_spec(mod_row, k_scale, d)]
        args += [nmods, nmods]
        out_specs.append(row)
        out_shape.append(jax.ShapeDtypeStruct((bm, lm, d), BF16))
    out = pl.pallas_call(
        functools.partial(_postnorm_kernel, alpha=alpha, with_next=next_mod is not None),
        grid=(bm, lm // tm),
        in_specs=in_specs,
        out_specs=out_specs,
        out_shape=out_shape,
        compiler_params=_params(("arbitrary", "arbitrary")),
        name="postnorm",
    )(*args)
    return (out[0], out[1]) if next_mod is not None else (out[0], None)


def _mm_kernel(a_ref, w_ref, *rest, ni, nj, job):
    if job is None:
        (o_ref,) = rest
    else:
        rw_ref, o_ref, ro_ref = rest
        step = (pl.program_id(0) * ni + pl.program_id(1)) * nj + pl.program_id(2)
        _cast_step(job, step, rw_ref, ro_ref)
    o_ref[...] = _dot(a_ref[...], w_ref[...]).astype(o_ref.dtype)


def _mm_call(a, w, out_dtype, name, tm_pref=1024, tn_pref=1024, rider=None):
    ba, m, k = a.shape
    bw, _, n = w.shape
    nb = max(ba, bw)
    tm = _tile(m, tm_pref)
    tn = _tile(n, tn_pref)
    ni, nj = m // tm, n // tn
    if rider is not None and not _rider_fits(rider, nb * ni * nj):
        return _mm_call(a, w, out_dtype, name, tm_pref, tn_pref), _pad_cast_call(*rider)
    a_map = (lambda b, i, j: (b, i, 0)) if ba > 1 else (lambda b, i, j: (0, i, 0))
    w_map = (lambda b, i, j: (b, 0, j)) if bw > 1 else (lambda b, i, j: (0, 0, j))
    in_specs = [pl.BlockSpec((None, tm, k), a_map), pl.BlockSpec((None, k, tn), w_map)]
    out_specs = [pl.BlockSpec((None, tm, tn), lambda b, i, j: (b, i, j))]
    out_shape = [jax.ShapeDtypeStruct((nb, m, n), out_dtype)]
    args = [a, w]
    job = None
    if rider is not None:
        job = _attach_rider(rider, lambda b, i, j: (b * ni + i) * nj + j,
                            in_specs, out_specs, out_shape, args)
    out = pl.pallas_call(
        functools.partial(_mm_kernel, ni=ni, nj=nj, job=job),
        grid=(nb, ni, nj),
        in_specs=in_specs,
        out_specs=out_specs,
        out_shape=out_shape,
        compiler_params=_params(("arbitrary", "arbitrary", "arbitrary")),
        name=name,
    )(*args)
    return out if rider is not None else out[0]


def _mm_w32_kernel(a_ref, w_ref, *rest, ni, job):
    if job is None:
        o_ref, w_scr = rest
    else:
        rw_ref, o_ref, ro_ref, w_scr = rest
    i = pl.program_id(1)

    @pl.when(i == 0)
    def _():
        w_scr[...] = w_ref[...].astype(w_scr.dtype)

    if job is not None:
        _cast_step(job, pl.program_id(0) * ni + i, rw_ref, ro_ref)
    o_ref[...] = _dot(a_ref[...], w_scr[...]).astype(o_ref.dtype)


def _mm_w32_call(a, w, layer, col0, ncols, out_dtype, name, rider=None):
    m, k = a.shape
    tm = _tile(m, 1024)
    tn = _tile(ncols, 512)
    assert col0 % tn == 0
    j0 = col0 // tn
    ni = m // tm
    in_specs = [
        pl.BlockSpec((tm, k), lambda j, i: (i, 0)),
        pl.BlockSpec((None, k, tn), lambda j, i: (layer, 0, j0 + j)),
    ]
    out_specs = [pl.BlockSpec((tm, tn), lambda j, i: (i, j))]
    out_shape = [jax.ShapeDtypeStruct((m, ncols), out_dtype)]
    args = [a, w]
    job = None
    if rider is not None and not _rider_fits(rider, (ncols // tn) * ni):
        return _mm_w32_call(a, w, layer, col0, ncols, out_dtype, name), _pad_cast_call(*rider)
    if rider is not None:
        job = _attach_rider(rider, lambda j, i: j * ni + i, in_specs, out_specs, out_shape, args)
    out = pl.pallas_call(
        functools.partial(_mm_w32_kernel, ni=ni, job=job),
        grid=(ncols // tn, ni),
        in_specs=in_specs,
        out_specs=out_specs,
        out_shape=out_shape,
        scratch_shapes=[pltpu.VMEM((k, tn), BF16)],
        compiler_params=_params(("arbitrary", "arbitrary")),
        name=name,
    )(*args)
    return out if rider is not None else out[0]


def _chan_dft_kernel(u_ref, cs_ref, o_ref, *, gd):
    r = _dot(u_ref[...], cs_ref[...])
    o_ref[0] = r[:, :gd].astype(o_ref.dtype)
    o_ref[1] = r[:, gd:].astype(o_ref.dtype)


def _chan_dft_call(proj, cs, fw):
    bs, ls, _ = proj.shape
    gd = fw // FOURIER_GROUPS
    tm = _tile(ls, 1024)
    return pl.pallas_call(
        functools.partial(_chan_dft_kernel, gd=gd),
        grid=(bs, ls // tm, FOURIER_GROUPS),
        in_specs=[
            pl.BlockSpec((None, tm, gd), lambda b, i, g: (b, i, g)),
            pl.BlockSpec((gd, 2 * gd), lambda b, i, g: (0, 0)),
        ],
        out_specs=pl.BlockSpec((None, 2, tm, gd), lambda b, i, g: (b, 0, i, g)),
        out_shape=jax.ShapeDtypeStruct((bs, 2, ls, fw), BF16),
        compiler_params=_params(("arbitrary", "arbitrary", "arbitrary")),
        name="chan_dft",
    )(proj, cs)


def _ssm_kernel(uxf_ref, uxb_ref, ucf_ref, ucb_ref, bm_ref, cm_ref, a_ref, *rest,
                chunk, nbat, half, n_ctx, n_all, job):
    side_work = None
    if job is None:
        yf_ref, yb_ref, *scratch = rest
    else:
        rw_ref, yf_ref, yb_ref, ro_ref, *scratch = rest
        step = pl.program_id(0) * n_all + pl.program_id(1)
        side_work = functools.partial(_cast_step, job, step, rw_ref, ro_ref)
    _ssm_step(uxf_ref, uxb_ref, ucf_ref, ucb_ref, bm_ref, cm_ref, a_ref, yf_ref, yb_ref,
              *scratch, chunk=chunk, nbat=nbat, half=half, n_ctx=n_ctx, side_work=side_work)


def _ssm_step(uxf_ref, uxb_ref, ucf_ref, ucb_ref, bm_ref, cm_ref, a_ref, yf_ref, yb_ref,
              u_scr, uin_scr, s_scr, y_scr, h_scr, *, chunk, nbat, half, n_ctx, side_work):
    k = pl.program_id(0)
    c = pl.program_id(1)
    rows = SUBLANES
    hw = half // 2
    mm_rows = min(chunk * rows, SSM_MM_ROWS)

    @pl.when(jnp.logical_and(k == 0, c == 0))
    def _():
        uin_scr[...] = jnp.zeros_like(uin_scr)

    @pl.when(c == 0)
    def _():
        h_scr[...] = jnp.zeros_like(h_scr)

    @pl.when(c < n_ctx)
    def _():
        u_scr[0] = ucf_ref[...]
        u_scr[1] = ucb_ref[...]

    @pl.when(c >= n_ctx)
    def _():
        u_scr[0] = uxf_ref[...]
        u_scr[1] = uxb_ref[...]

    if side_work is not None:
        side_work()
    for d in range(2):
        for b in range(nbat):
            ub = u_scr[d, b].astype(F32)
            for h in range(2):
                uin_scr[d, h, pl.ds(2 * b + h, chunk, stride=rows), :] = ub[:, h * LANES:(h + 1) * LANES]
        bm = bm_ref[d]
        for r0 in range(0, chunk * rows, mm_rows):
            rs = slice(r0, r0 + mm_rows)
            lhs = jnp.concatenate([uin_scr[d, 0, rs], uin_scr[d, 1, rs]], axis=1).astype(BF16)
            s_scr[d, rs] = _dot(lhs, bm)

    arf, aif = a_ref[0, 0], a_ref[0, 1]
    arb, aib = a_ref[1, 0], a_ref[1, 1]

    def body(t, carry):
        hrf, hif, hrb, hib = carry
        rowf = pl.multiple_of(t * rows, rows)
        rowb = pl.multiple_of((chunk - 1 - t) * rows, rows)
        xf = s_scr[0, pl.ds(rowf, rows), :]
        xb = s_scr[1, pl.ds(rowb, rows), :]
        nrf = arf * hrf - aif * hif + xf[:, :hw]
        nif = arf * hif + aif * hrf + xf[:, hw:]
        nrb = arb * hrb - aib * hib + xb[:, :hw]
        nib = arb * hib + aib * hrb + xb[:, hw:]
        s_scr[0, pl.ds(rowf, rows), :hw] = nrf
        s_scr[0, pl.ds(rowf, rows), hw:] = nif
        s_scr[1, pl.ds(rowb, rows), :hw] = nrb
        s_scr[1, pl.ds(rowb, rows), hw:] = nib
        return nrf, nif, nrb, nib

    init = (h_scr[0, :, :hw], h_scr[0, :, hw:], h_scr[1, :, :hw], h_scr[1, :, hw:])
    hrf, hif, hrb, hib = lax.fori_loop(0, chunk, body, init, unroll=8)
    h_scr[0, :, :hw] = hrf
    h_scr[0, :, hw:] = hif
    h_scr[1, :, :hw] = hrb
    h_scr[1, :, hw:] = hib

    for d, y_ref in enumerate((yf_ref, yb_ref)):
        cm = cm_ref[d]
        for r0 in range(0, chunk * rows, mm_rows):
            rs = slice(r0, r0 + mm_rows)
            out = _dot(s_scr[d, rs].astype(BF16), cm)
            y_scr[0, rs] = out[:, :LANES]
            y_scr[1, rs] = out[:, LANES:]
        for b in range(nbat):
            for h in range(2):
                piece = y_scr[h, pl.ds(2 * b + h, chunk, stride=rows), :]
                y_ref[b, :, h * LANES:(h + 1) * LANES] = piece.astype(y_ref.dtype)


def _ssm_call(ux, ux_col, uc, uc_col, bmat, cmat, amat, chunk, rider=None):
    nbat, seq, _ = ux.shape
    clen = uc.shape[1]
    assert 2 * nbat == SUBLANES, "the scan packs 2 column halves x batch on the 8 sublanes"
    ndir, nblk, cin, cst = bmat.shape
    assert cin == 2 * LANES, "each column half of a group block is one 128-lane input slab"
    sw = nblk * cin
    n_lat, n_ctx = seq // chunk, clen // chunk
    n_all = n_lat + n_ctx
    assert ux_col % cin == 0 and uc_col % cin == 0
    xo, co = ux_col // cin, uc_col // cin

    def lat_f(c):
        return jnp.maximum(c - n_ctx, 0)

    def lat_b(c):
        return n_lat - 1 - jnp.maximum(c - n_ctx, 0)

    def ctx_f(c):
        return jnp.minimum(c, n_ctx - 1)

    def ctx_b(c):
        return jnp.maximum(n_ctx - 1 - c, 0)

    def out_f(c):
        return jnp.where(c < n_ctx, n_lat + c, c - n_ctx)

    def out_b(c):
        return jnp.where(c < n_ctx, n_lat + n_ctx - 1 - c, n_lat - 1 - (c - n_ctx))

    ublk = (nbat, chunk, cin)
    in_specs = [
        pl.BlockSpec(ublk, lambda k, c: (0, lat_f(c), xo + k)),
        pl.BlockSpec(ublk, lambda k, c: (0, lat_b(c), xo + k)),
        pl.BlockSpec(ublk, lambda k, c: (0, ctx_f(c), co + k)),
        pl.BlockSpec(ublk, lambda k, c: (0, ctx_b(c), co + k)),
        pl.BlockSpec((ndir, None, cin, cst), lambda k, c: (0, k, 0, 0)),
        pl.BlockSpec((ndir, None, cst, cin), lambda k, c: (0, k, 0, 0)),
        pl.BlockSpec((ndir, None, 2, SUBLANES, cst // 2), lambda k, c: (0, k, 0, 0, 0)),
    ]
    out_specs = [
        pl.BlockSpec(ublk, lambda k, c: (0, out_f(c), k)),
        pl.BlockSpec(ublk, lambda k, c: (0, out_b(c), k)),
    ]
    out_shape = [jax.ShapeDtypeStruct((nbat, seq + clen, sw), BF16)] * 2
    args = [ux, ux, uc, uc, bmat, cmat, amat]
    job = None
    if rider is not None and not _rider_fits(rider, nblk * n_all):
        yf, yb = _ssm_call(ux, ux_col, uc, uc_col, bmat, cmat, amat, chunk)
        return yf, yb, _pad_cast_call(*rider)
    if rider is not None:
        job = _attach_rider(rider, lambda k, c: k * n_all + c, in_specs, out_specs, out_shape, args)
    return pl.pallas_call(
        functools.partial(_ssm_kernel, chunk=chunk, nbat=nbat, half=cst, n_ctx=n_ctx,
                          n_all=n_all, job=job),
        grid=(nblk, n_all),
        in_specs=in_specs,
        out_specs=out_specs,
        out_shape=out_shape,
        scratch_shapes=[
            pltpu.VMEM((ndir, nbat, chunk, cin), BF16),
            pltpu.VMEM((ndir, 2, SUBLANES * chunk, LANES), F32),
            pltpu.VMEM((ndir, SUBLANES * chunk, cst), F32),
            pltpu.VMEM((2, SUBLANES * chunk, LANES), F32),
            pltpu.VMEM((ndir, SUBLANES, cst), F32),
        ],
        compiler_params=_params(("arbitrary", "arbitrary")),
        name="ssm",
    )(*args)


def _ssm_tables(a_re, a_im, log_dt, b_re, b_im, c_re, c_im):
    ndir, g, p = a_re.shape
    cg = b_re.shape[-1]
    gb = SSM_BLOCK_GROUPS
    nblk = g // gb
    hg = gb // 2
    npair = hg // 2
    assert 2 * p == LANES and g % gb == 0 and hg * cg == LANES
    lr, li = a_re.astype(F32), a_im.astype(F32)
    dt = jnp.exp(log_dt.astype(F32))[..., None]
    zr, zi = lr * dt, li * dt
    a_bar_r = jnp.exp(zr) * jnp.cos(zi)
    a_bar_i = jnp.exp(zr) * jnp.sin(zi)
    em1_r = jnp.expm1(zr) * jnp.cos(zi) - 2.0 * jnp.square(jnp.sin(0.5 * zi))
    den = lr * lr + li * li
    cf_r = ((em1_r * lr + a_bar_i * li) / den)[..., None]
    cf_i = ((a_bar_i * lr - em1_r * li) / den)[..., None]
    br, bi = b_re.astype(F32), b_im.astype(F32)
    b_bar_r = cf_r * br - cf_i * bi
    b_bar_i = cf_r * bi + cf_i * br

    lane_q = np.arange(LANES) // p
    own = (np.arange(gb)[:, None, None] % hg
           == 2 * np.arange(npair)[None, :, None] + lane_q[None, None, :])
    own_in = np.repeat(own, cg, axis=0).astype(np.float32)
    own_out = np.ascontiguousarray(own_in.transpose(1, 2, 0))

    def in_mat(bb):
        t = jnp.swapaxes(bb.reshape(ndir, nblk, gb, p, cg), -1, -2).reshape(ndir, nblk, gb * cg, p)
        t = jnp.tile(t, (1, 1, 1, 2))[:, :, :, None, :] * own_in
        return t.reshape(ndir, nblk, gb * cg, npair * LANES)

    def out_mat(cc):
        t = jnp.swapaxes(cc.reshape(ndir, nblk, gb * cg, p), -1, -2)
        t = jnp.tile(t, (1, 1, 2, 1))[:, :, None, :, :] * own_out
        return t.reshape(ndir, nblk, npair * LANES, gb * cg)

    bmat = jnp.concatenate([in_mat(b_bar_r), in_mat(b_bar_i)], axis=-1).astype(BF16)
    cmat = jnp.concatenate([out_mat(c_re.astype(F32)), out_mat(-c_im.astype(F32))], axis=-2).astype(BF16)

    def decay(a):
        t = a.reshape(ndir, nblk, 2, npair * LANES)
        return jnp.tile(t, (1, 1, SUBLANES // 2, 1))
    aa = jnp.stack([decay(a_bar_r), decay(a_bar_i)], axis=2)
    return bmat, cmat, aa


def _ssm_act_kernel(u_ref, yf_ref, yb_ref, d_ref, o_ref):
    y = d_ref[...] * u_ref[...].astype(F32) + yf_ref[...].astype(F32) + yb_ref[...].astype(F32)
    o_ref[...] = jax.nn.gelu(y).astype(o_ref.dtype)


def _ssm_act_call(proj, yf, yb, row_off, ssm_d, fw, sw):
    bs, ls, _ = proj.shape
    tm = _tile(ls, 512)
    assert fw % sw == 0 and row_off % tm == 0
    u_blk, r_blk = fw // sw, row_off // tm
    return pl.pallas_call(
        _ssm_act_kernel,
        grid=(bs, ls // tm),
        in_specs=[
            pl.BlockSpec((None, tm, sw), lambda b, i: (b, i, u_blk)),
            pl.BlockSpec((None, tm, sw), lambda b, i: (b, i + r_blk, 0)),
            pl.BlockSpec((None, tm, sw), lambda b, i: (b, i + r_blk, 0)),
            pl.BlockSpec((1, sw), lambda b, i: (0, 0)),
        ],
        out_specs=pl.BlockSpec((None, tm, sw), lambda b, i: (b, i, 0)),
        out_shape=jax.ShapeDtypeStruct((bs, ls, sw), BF16),
        compiler_params=_params(("arbitrary", "arbitrary")),
        name="ssm_act",
    )(proj, yf, yb, ssm_d)


def _glu_kernel(a_ref, wv_ref, wg_ref, gf_ref, gs_ref, fo_ref, *rest, nb, ni, job):
    if job is None:
        (o_ref,) = rest
    else:
        rw_ref, o_ref, ro_ref = rest
        step = (pl.program_id(0) * nb + pl.program_id(1)) * ni + pl.program_id(2)
        _cast_step(job, step, rw_ref, ro_ref)
    a = a_ref[...]
    tn = o_ref.shape[1]
    sub = min(tn, GLU_SUBTILE)
    for n0 in range(0, tn, sub):
        cols = slice(n0, n0 + sub)
        y_s = (_dot(a, wv_ref[:, cols]) * jax.nn.sigmoid(_dot(a, wg_ref[:, cols]))).astype(BF16)
        o_ref[:, cols] = (jax.nn.sigmoid(gf_ref[:, cols]) * fo_ref[:, cols]
                          + jax.nn.sigmoid(gs_ref[:, cols]) * y_s)


def _glu_call(act, proj, glu_w, y_f, fw, sw, d, rider=None):
    bs, ls, _ = act.shape
    tm = _tile(ls, 512)
    tn = _tile(d, 1024)
    assert (fw + sw) % tn == 0
    gf_blk = (fw + sw) // tn
    gs_blk = (fw + sw + d) // tn
    ni, nj = ls // tm, d // tn
    if rider is not None and not _rider_fits(rider, bs * ni * nj):
        return _glu_call(act, proj, glu_w, y_f, fw, sw, d), _pad_cast_call(*rider)
    in_specs = [
        pl.BlockSpec((None, tm, sw), lambda j, b, i: (b, i, 0)),
        pl.BlockSpec((sw, tn), lambda j, b, i: (0, j)),
        pl.BlockSpec((sw, tn), lambda j, b, i: (0, nj + j)),
        pl.BlockSpec((None, tm, tn), lambda j, b, i: (b, i, gf_blk + j)),
        pl.BlockSpec((None, tm, tn), lambda j, b, i: (b, i, gs_blk + j)),
        pl.BlockSpec((None, tm, tn), lambda j, b, i: (b, i, j)),
    ]
    out_specs = [pl.BlockSpec((None, tm, tn), lambda j, b, i: (b, i, j))]
    out_shape = [jax.ShapeDtypeStruct((bs, ls, d), BF16)]
    args = [act, glu_w, glu_w, proj, proj, y_f]
    job = None
    if rider is not None:
        job = _attach_rider(rider, lambda j, b, i: (j * bs + b) * ni + i,
                            in_specs, out_specs, out_shape, args)
    out = pl.pallas_call(
        functools.partial(_glu_kernel, nb=bs, ni=ni, job=job),
        grid=(nj, bs, ni),
        in_specs=in_specs,
        out_specs=out_specs,
        out_shape=out_shape,
        compiler_params=_params(("arbitrary", "arbitrary", "arbitrary")),
        name="glu_merge",
    )(*args)
    return out if rider is not None else out[0]


def _ffn1_kernel(h_ref, wu_ref, wv_ref, cw_ref, cb_ref, *rest, period, ni, nj, job):
    if job is None:
        (o_ref,) = rest
    else:
        rw_ref, o_ref, ro_ref = rest
        step = (pl.program_id(0) * ni + pl.program_id(1)) * nj + pl.program_id(2)
        _cast_step(job, step, rw_ref, ro_ref)
    h = h_ref[...]
    u = _dot(h, wu_ref[...])
    v = _dot(h, wv_ref[...])
    tm = u.shape[0]
    pos = lax.broadcasted_iota(jnp.int32, u.shape, 0) % period
    prev = jnp.where(pos == 0, 0.0, pltpu.roll(u, 1, axis=0))
    nxt = jnp.where(pos == period - 1, 0.0, pltpu.roll(u, tm - 1, axis=0))
    cw = cw_ref[...]
    conv = cb_ref[...] + cw[0:1] * prev + cw[1:2] * u + cw[2:3] * nxt
    o_ref[...] = (jax.nn.gelu(conv) * v).astype(o_ref.dtype)


def _ffn1_call(h, w12, conv_w, conv_b, period, rider=None):
    bm, lm, d = h.shape
    ffn = w12.shape[1] // 2
    tm = _tile(lm, 1024)
    tn = _tile(ffn, 512)
    ni, nj = lm // tm, ffn // tn
    assert tm % period == 0
    if rider is not None and not _rider_fits(rider, bm * ni * nj):
        return _ffn1_call(h, w12, conv_w, conv_b, period), _pad_cast_call(*rider)
    in_specs = [
        pl.BlockSpec((None, tm, d), lambda b, i, j: (b, i, 0)),
        pl.BlockSpec((d, tn), lambda b, i, j: (0, j)),
        pl.BlockSpec((d, tn), lambda b, i, j: (0, nj + j)),
        pl.BlockSpec((3, tn), lambda b, i, j: (0, j)),
        pl.BlockSpec((1, tn), lambda b, i, j: (0, j)),
    ]
    out_specs = [pl.BlockSpec((None, tm, tn), lambda b, i, j: (b, i, j))]
    out_shape = [jax.ShapeDtypeStruct((bm, lm, ffn), BF16)]
    args = [h, w12, w12, conv_w, conv_b]
    job = None
    if rider is not None:
        job = _attach_rider(rider, lambda b, i, j: (b * ni + i) * nj + j,
                            in_specs, out_specs, out_shape, args)
    out = pl.pallas_call(
        functools.partial(_ffn1_kernel, period=period, ni=ni, nj=nj, job=job),
        grid=(bm, ni, nj),
        in_specs=in_specs,
        out_specs=out_specs,
        out_shape=out_shape,
        compiler_params=_params(("arbitrary", "arbitrary", "arbitrary")),
        name="ffn1",
    )(*args)
    return out if rider is not None else out[0]


def _dft_cos_sin(n):
    idx = np.arange(n, dtype=np.int64)
    ang = (2.0 * math.pi / n) * ((idx[:, None] * idx[None, :]) % n)
    scale = 1.0 / math.sqrt(n)
    return np.cos(ang) * scale, np.sin(ang) * scale


def _bf16_const(a):
    return jnp.asarray(np.asarray(a, np.float32).astype(ml_dtypes.bfloat16))


def kernel(x, c, ctx, c_ctx, ada_w, ada_b, w_in, fourier_w, ssm_a_re, ssm_a_im, ssm_log_dt,
           ssm_b_re, ssm_b_im, ssm_c_re, ssm_c_im, ssm_d, glu_w, w_out, ln1_g, ln1_b,
           ffn_w12, ffn_conv_w, ffn_conv_b, ffn_w2, ln2_g, ln2_b):
    nb, seq, d = x.shape
    clen = ctx.shape[1]
    depth = ada_w.shape[0]
    fw = fourier_w.shape[1]
    sw = ssm_d.shape[1]
    ffn = ffn_conv_b.shape[1]
    alpha = (2.0 * depth) ** 0.25
    gd = fw // FOURIER_GROUPS
    chunk = clen
    assert seq % chunk == 0 and seq % GRID_W == 0
    ffn_pad = 256 if (ffn % 512 == 256) else 0

    ctx_row = nb
    n_rows = -(-(nb + 1) // SUBLANES) * SUBLANES
    cond = jnp.zeros((n_rows, d), F32).at[:nb].set(c).at[ctx_row].set(c_ctx)
    mods_all = _adaln_call(cond, ada_w, ada_b).reshape(depth, n_rows, 1, N_MOD * d)

    cos_c, sin_c = _dft_cos_sin(gd)
    cs_chan = _bf16_const(np.concatenate([cos_c, sin_c], axis=1))

    def pos_dft_matrix(n):
        cos_l, sin_l = _dft_cos_sin(n)
        return _bf16_const(np.concatenate([cos_l, -sin_l], axis=1))[None]

    csl_x = pos_dft_matrix(seq)
    csl_c = pos_dft_matrix(clen)

    def token_mix_tail(proj, csl, yf, yb, row_off, lw):
        ls = proj.shape[1]
        ab = _chan_dft_call(proj, cs_chan, fw)
        ab = ab.reshape(nb, 2 * ls, fw)
        if "fourier_w" not in lw:
            f, lw["fourier_w"] = _mm_call(csl, ab, BF16, "pos_dft", rider=lw.pop("ride_fourier_w"))
        else:
            f = _mm_call(csl, ab, BF16, "pos_dft")
        f = f.reshape(1, nb * ls, fw)
        if "glu_w" not in lw:
            y_f, lw["glu_w"] = _mm_call(f, lw["fourier_w"][None], BF16, "fourier_out",
                                        rider=lw.pop("ride_glu_w"))
        else:
            y_f = _mm_call(f, lw["fourier_w"][None], BF16, "fourier_out")
        y_f = y_f.reshape(nb, ls, d)
        act = _ssm_act_call(proj, yf, yb, row_off, lw["ssm_d"], fw, sw)
        if "w_out" not in lw:
            merged, lw["w_out"] = _glu_call(act, proj, lw["glu_w"], y_f, fw, sw, d,
                                            rider=lw.pop("ride_w_out"))
        else:
            merged = _glu_call(act, proj, lw["glu_w"], y_f, fw, sw, d)
        merged = merged.reshape(1, nb * ls, d)
        if "ride_next_w_in" in lw:
            out, nxt_w["w_in"] = _mm_call(merged, lw["w_out"][None], BF16, "w_out", tn_pref=512,
                                          rider=lw.pop("ride_next_w_in"))
        else:
            out = _mm_call(merged, lw["w_out"][None], BF16, "w_out")
        return out.reshape(nb, ls, d)

    def conv_ffn(h, lw, period):
        if "ride_next_w12" in lw:
            act, nxt_w["w12"] = _ffn1_call(h, lw["w12"], lw["conv_w"], lw["conv_b"], period,
                                           rider=lw.pop("ride_next_w12"))
        else:
            act = _ffn1_call(h, lw["w12"], lw["conv_w"], lw["conv_b"], period)
        return _mm_call(act, lw["w2"][None], BF16, "ffn2", tm_pref=512, tn_pref=512)

    xs = x
    cs = ctx.reshape(1, nb * clen, d)
    nxt_w = {}
    hx = _ln_mod_call(xs, mods_all[0], None, 0, 1)
    hc = _ln_mod_call(cs, mods_all[0], ctx_row, 0, 1)
    for i in range(depth):
        last = i == depth - 1
        mods = mods_all[i]
        nxt = None if last else (mods_all[i + 1], 0, 1)
        lw = {
            "ssm_d": ssm_d[i].reshape(1, sw),
            "ride_fourier_w": (_rows_cast(fourier_w, i, _tile(fw, CAST_ROWS)), fourier_w),
            "ride_glu_w": (_rows_cast(glu_w, i, _tile(sw, CAST_ROWS // 2)), glu_w),
            "ride_w_out": (_rows_cast(w_out, i, _tile(d, CAST_ROWS)), w_out),
        }
        cur_w, nxt_w = nxt_w, {}
        ride_w12 = ride_w2 = None
        if ffn_pad:
            if "w12" in cur_w:
                lw["w12"] = cur_w["w12"]
            else:
                ride_w12 = (_pad_halves_cast(ffn_w12, i, ffn_pad), ffn_w12)
            ride_w2 = (_pad_rows_cast(ffn_w2, i, ffn_pad), ffn_w2)
            if not last:
                lw["ride_next_w_in"] = (_rows_cast(w_in, i + 1, _tile(d, CAST_ROWS // 4)), w_in)
                lw["ride_next_w12"] = (_pad_halves_cast(ffn_w12, i + 1, ffn_pad), ffn_w12)
        else:
            lw["w12"] = _cast_call(ffn_w12, i)
            lw["w2"] = _cast_call(ffn_w2, i)
        lw["conv_w"] = jnp.pad(ffn_conv_w[i], ((0, 0), (0, ffn_pad)))
        lw["conv_b"] = jnp.pad(ffn_conv_b[i].reshape(1, ffn), ((0, 0), (0, ffn_pad)))
        g1, b1 = ln1_g[i].reshape(1, d), ln1_b[i].reshape(1, d)
        g2, b2 = ln2_g[i].reshape(1, d), ln2_b[i].reshape(1, d)
        bmat, cmat, amat = _ssm_tables(ssm_a_re[i], ssm_a_im[i], ssm_log_dt[i], ssm_b_re[i],
                                       ssm_b_im[i], ssm_c_re[i], ssm_c_im[i])

        n_in = w_in.shape[2]
        hc2d = hc.reshape(nb * clen, d)
        c_col = 0 if last else fw
        if "w_in" in cur_w:
            w_in_b = cur_w["w_in"]
            proj_x = _mm_call(hx.reshape(1, nb * seq, d), w_in_b[None], BF16, "w_in", rider=ride_w12)
            w_in_c = w_in_b[None, :, fw:fw + sw] if last else w_in_b[None]
            proj_c = _mm_call(hc2d[None], w_in_c, BF16, "w_in_ctx")
        else:
            proj_x = _mm_w32_call(hx.reshape(nb * seq, d), w_in, i, 0, n_in, BF16, "w_in", ride_w12)
            if last:
                proj_c = _mm_w32_call(hc2d, w_in, i, fw, sw, BF16, "w_in_ctx")
            else:
                proj_c = _mm_w32_call(hc2d, w_in, i, 0, n_in, BF16, "w_in_ctx")
        if ride_w12 is not None:
            proj_x, lw["w12"] = proj_x
        proj_x = proj_x.reshape(nb, seq, n_in)
        proj_c = proj_c.reshape(nb, clen, -1)
        ssm_out = _ssm_call(proj_x, fw, proj_c, c_col, bmat, cmat, amat, chunk, ride_w2)
        yf, yb = ssm_out[0], ssm_out[1]
        if ride_w2 is not None:
            lw["w2"] = ssm_out[2]

        out_x = token_mix_tail(proj_x, csl_x, yf, yb, 0, lw)
        x1, h2 = _postnorm_call(xs, out_x, mods, None, 2, g1, b1, alpha, (mods, 3, 4))
        ffn_x = conv_ffn(h2, lw, GRID_W)
        xs, hx = _postnorm_call(x1, ffn_x, mods, None, 5, g2, b2, alpha, nxt)

        if not last:
            out_c = token_mix_tail(proj_c, csl_c, yf, yb, seq, lw).reshape(1, nb * clen, d)
            c1, hc2 = _postnorm_call(cs, out_c, mods, ctx_row, 2, g1, b1, alpha, (mods, 3, 4))
            ffn_c = conv_ffn(hc2, lw, clen)
            cs, hc = _postnorm_call(c1, ffn_c, mods, ctx_row, 5, g2, b2, alpha, nxt)
    return xs
```

```python
import functools
import math
from typing import Callable, NamedTuple

import jax
import jax.numpy as jnp
import ml_dtypes
import numpy as np
from jax import lax
from jax.experimental import pallas as pl
from jax.experimental.pallas import tpu as pltpu

GRID_W = 64
FOURIER_GROUPS = 4
N_MOD = 6
LN_EPS = 1e-6
SSM_BLOCK_GROUPS = 16
LANES = 128
SUBLANES = 8
VMEM_LIMIT = 56 * 1024 * 1024
GLU_SUBTILE = 256
CAST_ROWS = 256
SSM_MM_ROWS = 256

F32 = jnp.float32
BF16 = jnp.bfloat16


def _params(sem):
    return pltpu.CompilerParams(dimension_semantics=sem, vmem_limit_bytes=VMEM_LIMIT)


def _tile(n, pref):
    if n <= pref:
        return n
    while n % pref:
        pref //= 2
    assert pref >= SUBLANES, (n, pref)
    return pref


def _lane_tile(n, cap):
    for t in range(cap - cap % LANES, 0, -LANES):
        if n % t == 0:
            return t
    return n


def _dot(a, b):
    return jnp.dot(a, b, preferred_element_type=F32)


def _layer_norm(x):
    mu = jnp.mean(x, axis=-1, keepdims=True)
    xc = x - mu
    var = jnp.mean(xc * xc, axis=-1, keepdims=True)
    return xc * lax.rsqrt(var + LN_EPS)


def _mod_spec(mod_row, k, d):
    if mod_row is None:
        return pl.BlockSpec((None, 1, d), lambda b, *_: (b, 0, k))
    return pl.BlockSpec((None, 1, d), lambda b, *_: (mod_row, 0, k))


def _cast_kernel(w_ref, o_ref):
    o_ref[...] = w_ref[...].astype(o_ref.dtype)


def _cast_call(w, layer):
    _, r, c = w.shape
    tr = _tile(r, 512)
    tc = _lane_tile(c, 6144)
    return pl.pallas_call(
        _cast_kernel,
        grid=(r // tr, c // tc),
        in_specs=[pl.BlockSpec((None, tr, tc), lambda i, j: (layer, i, j))],
        out_specs=pl.BlockSpec((tr, tc), lambda i, j: (i, j)),
        out_shape=jax.ShapeDtypeStruct((r, c), BF16),
        compiler_params=_params(("arbitrary", "arbitrary")),
        name="cast_w",
    )(w)


class _PadCast(NamedTuple):
    layer: int
    in_block: tuple
    in_index: Callable
    out_block: tuple
    out_index: Callable
    out_shape: tuple
    nblk: int
    is_data: Callable


def _rows_cast(w, layer, tr):
    _, r, c = w.shape
    assert r % tr == 0
    return _PadCast(layer, (None, tr, c), lambda s: (s, 0), (tr, c), lambda s: (s, 0), (r, c),
                    r // tr, lambda s: s >= 0)


def _pad_rows_cast(w, layer, tr):
    _, r, c = w.shape
    nvalid = r // tr
    assert r % tr == 0
    return _PadCast(layer, (None, tr, c), lambda s: (jnp.minimum(s, nvalid - 1), 0),
                    (tr, c), lambda s: (s, 0), (r + tr, c), nvalid + 1, lambda s: s < nvalid)


def _pad_halves_cast(w, layer, tc):
    _, r, c2 = w.shape
    f = c2 // 2
    nvalid = f // tc
    assert f % tc == 0
    per = nvalid + 1
    return _PadCast(layer, (None, r, tc),
                    lambda s: (0, (s // per) * nvalid + jnp.minimum(s % per, nvalid - 1)),
                    (r, tc), lambda s: (0, s), (r, 2 * (f + tc)), 2 * per, lambda s: s % per < nvalid)


def _cast_specs(job, step_of):
    def clamped(*g):
        return jnp.minimum(step_of(*g), job.nblk - 1)

    in_spec = pl.BlockSpec(job.in_block, lambda *g: (job.layer,) + tuple(job.in_index(clamped(*g))))
    out_spec = pl.BlockSpec(job.out_block, lambda *g: tuple(job.out_index(clamped(*g))))
    return in_spec, out_spec


def _cast_step(job, step, w_ref, o_ref):
    @pl.when(step < job.nblk)
    def _():
        o_ref[...] = jnp.where(job.is_data(step), w_ref[...], 0.0).astype(o_ref.dtype)


def _rider_fits(rider, nsteps):
    return rider[0].nblk <= nsteps


def _attach_rider(rider, step_of, in_specs, out_specs, out_shape, args):
    job, rw = rider
    r_in, r_out = _cast_specs(job, step_of)
    in_specs.append(r_in)
    out_specs.append(r_out)
    out_shape.append(jax.ShapeDtypeStruct(job.out_shape, BF16))
    args.append(rw)
    return job


def _pad_cast_kernel(w_ref, o_ref, *, job):
    _cast_step(job, pl.program_id(0), w_ref, o_ref)


def _pad_cast_call(job, w):
    in_spec, out_spec = _cast_specs(job, lambda s: s)
    return pl.pallas_call(
        functools.partial(_pad_cast_kernel, job=job),
        grid=(job.nblk,),
        in_specs=[in_spec],
        out_specs=out_spec,
        out_shape=jax.ShapeDtypeStruct(job.out_shape, BF16),
        compiler_params=_params(("arbitrary",)),
        name="cast_w_pad",
    )(w)


def _adaln_kernel(c_ref, w_ref, b_ref, o_ref):
    c = c_ref[...]
    s = (c * jax.nn.sigmoid(c)).astype(BF16)
    o_ref[...] = _dot(s, w_ref[...].astype(BF16)) + b_ref[...]


def _adaln_call(cond, ada_w, ada_b):
    depth, d, n = ada_w.shape
    r = cond.shape[0]
    tn = _tile(n, 512)
    return pl.pallas_call(
        _adaln_kernel,
        grid=(depth, n // tn),
        in_specs=[
            pl.BlockSpec((r, d), lambda l, j: (0, 0)),
            pl.BlockSpec((None, d, tn), lambda l, j: (l, 0, j)),
            pl.BlockSpec((None, 1, tn), lambda l, j: (l, 0, j)),
        ],
        out_specs=pl.BlockSpec((None, r, tn), lambda l, j: (l, 0, j)),
        out_shape=jax.ShapeDtypeStruct((depth, r, n), F32),
        compiler_params=_params(("arbitrary", "arbitrary")),
        name="adaln",
    )(cond, ada_w, ada_b.reshape(depth, 1, n))


def _ln_mod_kernel(x_ref, sh_ref, sc_ref, o_ref):
    y = _layer_norm(x_ref[...])
    o_ref[...] = (y * (1.0 + sc_ref[...]) + sh_ref[...]).astype(o_ref.dtype)


def _ln_mod_call(x, mods, mod_row, k_shift, k_scale):
    bm, lm, d = x.shape
    tm = _tile(lm, 256)
    return pl.pallas_call(
        _ln_mod_kernel,
        grid=(bm, lm // tm),
        in_specs=[
            pl.BlockSpec((None, tm, d), lambda b, i: (b, i, 0)),
            _mod_spec(mod_row, k_shift, d),
            _mod_spec(mod_row, k_scale, d),
        ],
        out_specs=pl.BlockSpec((None, tm, d), lambda b, i: (b, i, 0)),
        out_shape=jax.ShapeDtypeStruct((bm, lm, d), BF16),
        compiler_params=_params(("arbitrary", "arbitrary")),
        name="ln_mod",
    )(x, mods, mods)


def _postnorm_kernel(res_ref, pre_ref, gate_ref, g_ref, b_ref, *rest, alpha, with_next):
    z = alpha * res_ref[...] + gate_ref[...] * pre_ref[...].astype(F32)
    x_new = _layer_norm(z) * g_ref[...] + b_ref[...]
    if with_next:
        sh_ref, sc_ref, o_ref, h_ref = rest
        o_ref[...] = x_new
        h_ref[...] = (_layer_norm(x_new) * (1.0 + sc_ref[...]) + sh_ref[...]).astype(h_ref.dtype)
    else:
        (o_ref,) = rest
        o_ref[...] = x_new


def _postnorm_call(res, pre, mods, mod_row, k_gate, ln_g, ln_b, alpha, next_mod=None):
    bm, lm, d = res.shape
    tm = _tile(lm, 256)
    row = pl.BlockSpec((None, tm, d), lambda b, i: (b, i, 0))
    vec = pl.BlockSpec((1, d), lambda b, i: (0, 0))
    in_specs = [row, row, _mod_spec(mod_row, k_gate, d), vec, vec]
    args = [res, pre, mods, ln_g, ln_b]
    out_specs = [row]
    out_shape = [jax.ShapeDtypeStruct((bm, lm, d), F32)]
    if next_mod is not None:
        nmods, k_shift, k_scale = next_mod
        in_specs += [_mod_spec(mod_row, k_shift, d), _mod_spec(mod_row, k_scale, d)]
        args += [nmods, nmods]
        out_specs.append(row)
        out_shape.append(jax.ShapeDtypeStruct((bm, lm, d), BF16))
    out = pl.pallas_call(
        functools.partial(_postnorm_kernel, alpha=alpha, with_next=next_mod is not None),
        grid=(bm, lm // tm),
        in_specs=in_specs,
        out_specs=out_specs,
        out_shape=out_shape,
        compiler_params=_params(("arbitrary", "arbitrary")),
        name="postnorm",
    )(*args)
    return (out[0], out[1]) if next_mod is not None else (out[0], None)


def _mm_kernel(a_ref, w_ref, *rest, ni, nj, job):
    if job is None:
        (o_ref,) = rest
    else:
        rw_ref, o_ref, ro_ref = rest
        step = (pl.program_id(0) * ni + pl.program_id(1)) * nj + pl.program_id(2)
        _cast_step(job, step, rw_ref, ro_ref)
    o_ref[...] = _dot(a_ref[...], w_ref[...]).astype(o_ref.dtype)


def _mm_call(a, w, out_dtype, name, tm_pref=1024, tn_pref=1024, rider=None):
    ba, m, k = a.shape
    bw, _, n = w.shape
    nb = max(ba, bw)
    tm = _tile(m, tm_pref)
    tn = _tile(n, tn_pref)
    ni, nj = m // tm, n // tn
    if rider is not None and not _rider_fits(rider, nb * ni * nj):
        return _mm_call(a, w, out_dtype, name, tm_pref, tn_pref), _pad_cast_call(*rider)
    a_map = (lambda b, i, j: (b, i, 0)) if ba > 1 else (lambda b, i, j: (0, i, 0))
    w_map = (lambda b, i, j: (b, 0, j)) if bw > 1 else (lambda b, i, j: (0, 0, j))
    in_specs = [pl.BlockSpec((None, tm, k), a_map), pl.BlockSpec((None, k, tn), w_map)]
    out_specs = [pl.BlockSpec((None, tm, tn), lambda b, i, j: (b, i, j))]
    out_shape = [jax.ShapeDtypeStruct((nb, m, n), out_dtype)]
    args = [a, w]
    job = None
    if rider is not None:
        job = _attach_rider(rider, lambda b, i, j: (b * ni + i) * nj + j,
                            in_specs, out_specs, out_shape, args)
    out = pl.pallas_call(
        functools.partial(_mm_kernel, ni=ni, nj=nj, job=job),
        grid=(nb, ni, nj),
        in_specs=in_specs,
        out_specs=out_specs,
        out_shape=out_shape,
        compiler_params=_params(("arbitrary", "arbitrary", "arbitrary")),
        name=name,
    )(*args)
    return out if rider is not None else out[0]


def _mm_w32_kernel(a_ref, w_ref, *rest, ni, job):
    if job is None:
        o_ref, w_scr = rest
    else:
        rw_ref, o_ref, ro_ref, w_scr = rest
    i = pl.program_id(1)

    @pl.when(i == 0)
    def _():
        w_scr[...] = w_ref[...].astype(w_scr.dtype)

    if job is not None:
        _cast_step(job, pl.program_id(0) * ni + i, rw_ref, ro_ref)
    o_ref[...] = _dot(a_ref[...], w_scr[...]).astype(o_ref.dtype)


def _mm_w32_call(a, w, layer, col0, ncols, out_dtype, name, rider=None):
    m, k = a.shape
    tm = _tile(m, 1024)
    tn = _tile(ncols, 512)
    assert col0 % tn == 0
    j0 = col0 // tn
    ni = m // tm
    in_specs = [
        pl.BlockSpec((tm, k), lambda j, i: (i, 0)),
        pl.BlockSpec((None, k, tn), lambda j, i: (layer, 0, j0 + j)),
    ]
    out_specs = [pl.BlockSpec((tm, tn), lambda j, i: (i, j))]
    out_shape = [jax.ShapeDtypeStruct((m, ncols), out_dtype)]
    args = [a, w]
    job = None
    if rider is not None and not _rider_fits(rider, (ncols // tn) * ni):
        return _mm_w32_call(a, w, layer, col0, ncols, out_dtype, name), _pad_cast_call(*rider)
    if rider is not None:
        job = _attach_rider(rider, lambda j, i: j * ni + i, in_specs, out_specs, out_shape, args)
    out = pl.pallas_call(
        functools.partial(_mm_w32_kernel, ni=ni, job=job),
        grid=(ncols // tn, ni),
        in_specs=in_specs,
        out_specs=out_specs,
        out_shape=out_shape,
        scratch_shapes=[pltpu.VMEM((k, tn), BF16)],
        compiler_params=_params(("arbitrary", "arbitrary")),
        name=name,
    )(*args)
    return out if rider is not None else out[0]


def _chan_dft_kernel(u_ref, cs_ref, o_ref, *, gd):
    r = _dot(u_ref[...], cs_ref[...])
    o_ref[0] = r[:, :gd].astype(o_ref.dtype)
    o_ref[1] = r[:, gd:].astype(o_ref.dtype)


def _chan_dft_call(proj, cs, fw):
    bs, ls, _ = proj.shape
    gd = fw // FOURIER_GROUPS
    tm = _tile(ls, 1024)
    return pl.pallas_call(
        functools.partial(_chan_dft_kernel, gd=gd),
        grid=(bs, ls // tm, FOURIER_GROUPS),
        in_specs=[
            pl.BlockSpec((None, tm, gd), lambda b, i, g: (b, i, g)),
            pl.BlockSpec((gd, 2 * gd), lambda b, i, g: (0, 0)),
        ],
        out_specs=pl.BlockSpec((None, 2, tm, gd), lambda b, i, g: (b, 0, i, g)),
        out_shape=jax.ShapeDtypeStruct((bs, 2, ls, fw), BF16),
        compiler_params=_params(("arbitrary", "arbitrary", "arbitrary")),
        name="chan_dft",
    )(proj, cs)


def _ssm_kernel(uxf_ref, uxb_ref, ucf_ref, ucb_ref, bm_ref, cm_ref, a_ref, *rest,
                chunk, nbat, half, n_ctx, n_all, job):
    side_work = None
    if job is None:
        yf_ref, yb_ref, *scratch = rest
    else:
        rw_ref, yf_ref, yb_ref, ro_ref, *scratch = rest
        step = pl.program_id(0) * n_all + pl.program_id(1)
        side_work = functools.partial(_cast_step, job, step, rw_ref, ro_ref)
    _ssm_step(uxf_ref, uxb_ref, ucf_ref, ucb_ref, bm_ref, cm_ref, a_ref, yf_ref, yb_ref,
              *scratch, chunk=chunk, nbat=nbat, half=half, n_ctx=n_ctx, side_work=side_work)


def _ssm_step(uxf_ref, uxb_ref, ucf_ref, ucb_ref, bm_ref, cm_ref, a_ref, yf_ref, yb_ref,
              u_scr, uin_scr, s_scr, y_scr, h_scr, *, chunk, nbat, half, n_ctx, side_work):
    k = pl.program_id(0)
    c = pl.program_id(1)
    rows = SUBLANES
    hw = half // 2
    mm_rows = min(chunk * rows, SSM_MM_ROWS)
    fwd_starts = list(range(0, chunk * rows, mm_rows))
    row_starts = (fwd_starts, fwd_starts[::-1])

    @pl.when(jnp.logical_and(k == 0, c == 0))
    def _():
        uin_scr[...] = jnp.zeros_like(uin_scr)

    @pl.when(c == 0)
    def _():
        h_scr[...] = jnp.zeros_like(h_scr)

    @pl.when(c < n_ctx)
    def _():
        u_scr[0] = ucf_ref[...]
        u_scr[1] = ucb_ref[...]

    @pl.when(c >= n_ctx)
    def _():
        u_scr[0] = uxf_ref[...]
        u_scr[1] = uxb_ref[...]

    if side_work is not None:
        side_work()
    for d in range(2):
        for b in range(nbat):
            ub = u_scr[d, b].astype(F32)
            for h in range(2):
                uin_scr[d, h, pl.ds(2 * b + h, chunk, stride=rows), :] = ub[:, h * LANES:(h + 1) * LANES]
        bm = bm_ref[d]
        for r0 in row_starts[d]:
            rs = slice(r0, r0 + mm_rows)
            lhs = jnp.concatenate([uin_scr[d, 0, rs], uin_scr[d, 1, rs]], axis=1).astype(BF16)
            s_scr[d, rs] = _dot(lhs, bm)

    arf, aif = a_ref[0, 0], a_ref[0, 1]
    arb, aib = a_ref[1, 0], a_ref[1, 1]

    def body(t, carry):
        hrf, hif, hrb, hib = carry
        rowf = t * rows
        rowb = (chunk - 1 - t) * rows
        xf = s_scr[0, pl.ds(rowf, rows), :]
        xb = s_scr[1, pl.ds(rowb, rows), :]
        nrf = arf * hrf - aif * hif + xf[:, :hw]
        nif = arf * hif + aif * hrf + xf[:, hw:]
        nrb = arb * hrb - aib * hib + xb[:, :hw]
        nib = arb * hib + aib * hrb + xb[:, hw:]
        s_scr[0, pl.ds(rowf, rows), :hw] = nrf
        s_scr[0, pl.ds(rowf, rows), hw:] = nif
        s_scr[1, pl.ds(rowb, rows), :hw] = nrb
        s_scr[1, pl.ds(rowb, rows), hw:] = nib
        return nrf, nif, nrb, nib

    init = (h_scr[0, :, :hw], h_scr[0, :, hw:], h_scr[1, :, :hw], h_scr[1, :, hw:])
    carry = init
    for t in range(chunk):
        carry = body(t, carry)
    hrf, hif, hrb, hib = carry
    h_scr[0, :, :hw] = hrf
    h_scr[0, :, hw:] = hif
    h_scr[1, :, :hw] = hrb
    h_scr[1, :, hw:] = hib

    for d, y_ref in enumerate((yf_ref, yb_ref)):
        cm = cm_ref[d]
        for r0 in row_starts[d]:
            rs = slice(r0, r0 + mm_rows)
            out = _dot(s_scr[d, rs].astype(BF16), cm)
            y_scr[d, 0, rs] = out[:, :LANES]
            y_scr[d, 1, rs] = out[:, LANES:]
        for b in range(nbat):
            for h in range(2):
                piece = y_scr[d, h, pl.ds(2 * b + h, chunk, stride=rows), :]
                y_ref[b, :, h * LANES:(h + 1) * LANES] = piece.astype(y_ref.dtype)


def _ssm_call(ux, ux_col, uc, uc_col, bmat, cmat, amat, chunk, rider=None):
    nbat, seq, _ = ux.shape
    clen = uc.shape[1]
    assert 2 * nbat == SUBLANES, "the scan packs 2 column halves x batch on the 8 sublanes"
    ndir, nblk, cin, cst = bmat.shape
    assert cin == 2 * LANES, "each column half of a group block is one 128-lane input slab"
    sw = nblk * cin
    n_lat, n_ctx = seq // chunk, clen // chunk
    n_all = n_lat + n_ctx
    assert ux_col % cin == 0 and uc_col % cin == 0
    xo, co = ux_col // cin, uc_col // cin

    def lat_f(c):
        return jnp.maximum(c - n_ctx, 0)

    def lat_b(c):
        return n_lat - 1 - jnp.maximum(c - n_ctx, 0)

    def ctx_f(c):
        return jnp.minimum(c, n_ctx - 1)

    def ctx_b(c):
        return jnp.maximum(n_ctx - 1 - c, 0)

    def out_f(c):
        return jnp.where(c < n_ctx, n_lat + c, c - n_ctx)

    def out_b(c):
        return jnp.where(c < n_ctx, n_lat + n_ctx - 1 - c, n_lat - 1 - (c - n_ctx))

    ublk = (nbat, chunk, cin)
    in_specs = [
        pl.BlockSpec(ublk, lambda k, c: (0, lat_f(c), xo + k)),
        pl.BlockSpec(ublk, lambda k, c: (0, lat_b(c), xo + k)),
        pl.BlockSpec(ublk, lambda k, c: (0, ctx_f(c), co + k)),
        pl.BlockSpec(ublk, lambda k, c: (0, ctx_b(c), co + k)),
        pl.BlockSpec((ndir, None, cin, cst), lambda k, c: (0, k, 0, 0)),
        pl.BlockSpec((ndir, None, cst, cin), lambda k, c: (0, k, 0, 0)),
        pl.BlockSpec((ndir, None, 2, SUBLANES, cst // 2), lambda k, c: (0, k, 0, 0, 0)),
    ]
    out_specs = [
        pl.BlockSpec(ublk, lambda k, c: (0, out_f(c), k)),
        pl.BlockSpec(ublk, lambda k, c: (0, out_b(c), k)),
    ]
    out_shape = [jax.ShapeDtypeStruct((nbat, seq + clen, sw), BF16)] * 2
    args = [ux, ux, uc, uc, bmat, cmat, amat]
    job = None
    if rider is not None and not _rider_fits(rider, nblk * n_all):
        yf, yb = _ssm_call(ux, ux_col, uc, uc_col, bmat, cmat, amat, chunk)
        return yf, yb, _pad_cast_call(*rider)
    if rider is not None:
        job = _attach_rider(rider, lambda k, c: k * n_all + c, in_specs, out_specs, out_shape, args)
    return pl.pallas_call(
        functools.partial(_ssm_kernel, chunk=chunk, nbat=nbat, half=cst, n_ctx=n_ctx,
                          n_all=n_all, job=job),
        grid=(nblk, n_all),
        in_specs=in_specs,
        out_specs=out_specs,
        out_shape=out_shape,
        scratch_shapes=[
            pltpu.VMEM((ndir, nbat, chunk, cin), BF16),
            pltpu.VMEM((ndir, 2, SUBLANES * chunk, LANES), F32),
            pltpu.VMEM((ndir, SUBLANES * chunk, cst), F32),
            pltpu.VMEM((ndir, 2, SUBLANES * chunk, LANES), F32),
            pltpu.VMEM((ndir, SUBLANES, cst), F32),
        ],
        compiler_params=_params(("arbitrary", "arbitrary")),
        name="ssm",
    )(*args)


def _ssm_tables(a_re, a_im, log_dt, b_re, b_im, c_re, c_im):
    ndir, g, p = a_re.shape
    cg = b_re.shape[-1]
    gb = SSM_BLOCK_GROUPS
    nblk = g // gb
    hg = gb // 2
    npair = hg // 2
    assert 2 * p == LANES and g % gb == 0 and hg * cg == LANES
    lr, li = a_re.astype(F32), a_im.astype(F32)
    dt = jnp.exp(log_dt.astype(F32))[..., None]
    zr, zi = lr * dt, li * dt
    a_bar_r = jnp.exp(zr) * jnp.cos(zi)
    a_bar_i = jnp.exp(zr) * jnp.sin(zi)
    em1_r = jnp.expm1(zr) * jnp.cos(zi) - 2.0 * jnp.square(jnp.sin(0.5 * zi))
    den = lr * lr + li * li
    cf_r = ((em1_r * lr + a_bar_i * li) / den)[..., None]
    cf_i = ((a_bar_i * lr - em1_r * li) / den)[..., None]
    br, bi = b_re.astype(F32), b_im.astype(F32)
    b_bar_r = cf_r * br - cf_i * bi
    b_bar_i = cf_r * bi + cf_i * br

    lane_q = np.arange(LANES) // p
    own = (np.arange(gb)[:, None, None] % hg
           == 2 * np.arange(npair)[None, :, None] + lane_q[None, None, :])
    own_in = np.repeat(own, cg, axis=0).astype(np.float32)
    own_out = np.ascontiguousarray(own_in.transpose(1, 2, 0))

    def in_mat(bb):
        t = jnp.swapaxes(bb.reshape(ndir, nblk, gb, p, cg), -1, -2).reshape(ndir, nblk, gb * cg, p)
        t = jnp.tile(t, (1, 1, 1, 2))[:, :, :, None, :] * own_in
        return t.reshape(ndir, nblk, gb * cg, npair * LANES)

    def out_mat(cc):
        t = jnp.swapaxes(cc.reshape(ndir, nblk, gb * cg, p), -1, -2)
        t = jnp.tile(t, (1, 1, 2, 1))[:, :, None, :, :] * own_out
        return t.reshape(ndir, nblk, npair * LANES, gb * cg)

    bmat = jnp.concatenate([in_mat(b_bar_r), in_mat(b_bar_i)], axis=-1).astype(BF16)
    cmat = jnp.concatenate([out_mat(c_re.astype(F32)), out_mat(-c_im.astype(F32))], axis=-2).astype(BF16)

    def decay(a):
        t = a.reshape(ndir, nblk, 2, npair * LANES)
        return jnp.tile(t, (1, 1, SUBLANES // 2, 1))
    aa = jnp.stack([decay(a_bar_r), decay(a_bar_i)], axis=2)
    return bmat, cmat, aa


def _ssm_act_kernel(u_ref, yf_ref, yb_ref, d_ref, o_ref):
    y = d_ref[...] * u_ref[...].astype(F32) + yf_ref[...].astype(F32) + yb_ref[...].astype(F32)
    o_ref[...] = jax.nn.gelu(y).astype(o_ref.dtype)


def _ssm_act_call(proj, yf, yb, row_off, ssm_d, fw, sw):
    bs, ls, _ = proj.shape
    tm = _tile(ls, 512)
    assert fw % sw == 0 and row_off % tm == 0
    u_blk, r_blk = fw // sw, row_off // tm
    return pl.pallas_call(
        _ssm_act_kernel,
        grid=(bs, ls // tm),
        in_specs=[
            pl.BlockSpec((None, tm, sw), lambda b, i: (b, i, u_blk)),
            pl.BlockSpec((None, tm, sw), lambda b, i: (b, i + r_blk, 0)),
            pl.BlockSpec((None, tm, sw), lambda b, i: (b, i + r_blk, 0)),
            pl.BlockSpec((1, sw), lambda b, i: (0, 0)),
        ],
        out_specs=pl.BlockSpec((None, tm, sw), lambda b, i: (b, i, 0)),
        out_shape=jax.ShapeDtypeStruct((bs, ls, sw), BF16),
        compiler_params=_params(("arbitrary", "arbitrary")),
        name="ssm_act",
    )(proj, yf, yb, ssm_d)


def _glu_kernel(a_ref, wv_ref, wg_ref, gf_ref, gs_ref, fo_ref, *rest, nb, ni, job):
    if job is None:
        (o_ref,) = rest
    else:
        rw_ref, o_ref, ro_ref = rest
        step = (pl.program_id(0) * nb + pl.program_id(1)) * ni + pl.program_id(2)
        _cast_step(job, step, rw_ref, ro_ref)
    a = a_ref[...]
    tn = o_ref.shape[1]
    sub = min(tn, GLU_SUBTILE)
    for n0 in range(0, tn, sub):
        cols = slice(n0, n0 + sub)
        y_s = (_dot(a, wv_ref[:, cols]) * jax.nn.sigmoid(_dot(a, wg_ref[:, cols]))).astype(BF16)
        o_ref[:, cols] = (jax.nn.sigmoid(gf_ref[:, cols]) * fo_ref[:, cols]
                          + jax.nn.sigmoid(gs_ref[:, cols]) * y_s)


def _glu_call(act, proj, glu_w, y_f, fw, sw, d, rider=None):
    bs, ls, _ = act.shape
    tm = _tile(ls, 512)
    tn = _tile(d, 1024)
    assert (fw + sw) % tn == 0
    gf_blk = (fw + sw) // tn
    gs_blk = (fw + sw + d) // tn
    ni, nj = ls // tm, d // tn
    if rider is not None and not _rider_fits(rider, bs * ni * nj):
        return _glu_call(act, proj, glu_w, y_f, fw, sw, d), _pad_cast_call(*rider)
    in_specs = [
        pl.BlockSpec((None, tm, sw), lambda j, b, i: (b, i, 0)),
        pl.BlockSpec((sw, tn), lambda j, b, i: (0, j)),
        pl.BlockSpec((sw, tn), lambda j, b, i: (0, nj + j)),
        pl.BlockSpec((None, tm, tn), lambda j, b, i: (b, i, gf_blk + j)),
        pl.BlockSpec((None, tm, tn), lambda j, b, i: (b, i, gs_blk + j)),
        pl.BlockSpec((None, tm, tn), lambda j, b, i: (b, i, j)),
    ]
    out_specs = [pl.BlockSpec((None, tm, tn), lambda j, b, i: (b, i, j))]
    out_shape = [jax.ShapeDtypeStruct((bs, ls, d), BF16)]
    args = [act, glu_w, glu_w, proj, proj, y_f]
    job = None
    if rider is not None:
        job = _attach_rider(rider, lambda j, b, i: (j * bs + b) * ni + i,
                            in_specs, out_specs, out_shape, args)
    out = pl.pallas_call(
        functools.partial(_glu_kernel, nb=bs, ni=ni, job=job),
        grid=(nj, bs, ni),
        in_specs=in_specs,
        out_specs=out_specs,
        out_shape=out_shape,
        compiler_params=_params(("arbitrary", "arbitrary", "arbitrary")),
        name="glu_merge",
    )(*args)
    return out if rider is not None else out[0]


def _ffn1_kernel(h_ref, wu_ref, wv_ref, cw_ref, cb_ref, *rest, period, ni, nj, job):
    if job is None:
        (o_ref,) = rest
    else:
        rw_ref, o_ref, ro_ref = rest
        step = (pl.program_id(0) * ni + pl.program_id(1)) * nj + pl.program_id(2)
        _cast_step(job, step, rw_ref, ro_ref)
    h = h_ref[...]
    u = _dot(h, wu_ref[...])
    v = _dot(h, wv_ref[...])
    tm = u.shape[0]
    pos = lax.broadcasted_iota(jnp.int32, u.shape, 0) % period
    prev = jnp.where(pos == 0, 0.0, pltpu.roll(u, 1, axis=0))
    nxt = jnp.where(pos == period - 1, 0.0, pltpu.roll(u, tm - 1, axis=0))
    cw = cw_ref[...]
    conv = cb_ref[...] + cw[0:1] * prev + cw[1:2] * u + cw[2:3] * nxt
    o_ref[...] = (jax.nn.gelu(conv) * v).astype(o_ref.dtype)


def _ffn1_call(h, w12, conv_w, conv_b, period, rider=None):
    bm, lm, d = h.shape
    ffn = w12.shape[1] // 2
    tm = _tile(lm, 1024)
    tn = _tile(ffn, 512)
    ni, nj = lm // tm, ffn // tn
    assert tm % period == 0
    if rider is not None and not _rider_fits(rider, bm * ni * nj):
        return _ffn1_call(h, w12, conv_w, conv_b, period), _pad_cast_call(*rider)
    in_specs = [
        pl.BlockSpec((None, tm, d), lambda b, i, j: (b, i, 0)),
        pl.BlockSpec((d, tn), lambda b, i, j: (0, j)),
        pl.BlockSpec((d, tn), lambda b, i, j: (0, nj + j)),
        pl.BlockSpec((3, tn), lambda b, i, j: (0, j)),
        pl.BlockSpec((1, tn), lambda b, i, j: (0, j)),
    ]
    out_specs = [pl.BlockSpec((None, tm, tn), lambda b, i, j: (b, i, j))]
    out_shape = [jax.ShapeDtypeStruct((bm, lm, ffn), BF16)]
    args = [h, w12, w12, conv_w, conv_b]
    job = None
    if rider is not None:
        job = _attach_rider(rider, lambda b, i, j: (b * ni + i) * nj + j,
                            in_specs, out_specs, out_shape, args)
    out = pl.pallas_call(
        functools.partial(_ffn1_kernel, period=period, ni=ni, nj=nj, job=job),
        grid=(bm, ni, nj),
        in_specs=in_specs,
        out_specs=out_specs,
        out_shape=out_shape,
        compiler_params=_params(("arbitrary", "arbitrary", "arbitrary")),
        name="ffn1",
    )(*args)
    return out if rider is not None else out[0]


def _dft_cos_sin(n):
    idx = np.arange(n, dtype=np.int64)
    ang = (2.0 * math.pi / n) * ((idx[:, None] * idx[None, :]) % n)
    scale = 1.0 / math.sqrt(n)
    return np.cos(ang) * scale, np.sin(ang) * scale


def _bf16_const(a):
    return jnp.asarray(np.asarray(a, np.float32).astype(ml_dtypes.bfloat16))


def kernel(x, c, ctx, c_ctx, ada_w, ada_b, w_in, fourier_w, ssm_a_re, ssm_a_im, ssm_log_dt,
           ssm_b_re, ssm_b_im, ssm_c_re, ssm_c_im, ssm_d, glu_w, w_out, ln1_g, ln1_b,
           ffn_w12, ffn_conv_w, ffn_conv_b, ffn_w2, ln2_g, ln2_b):
    nb, seq, d = x.shape
    clen = ctx.shape[1]
    depth = ada_w.shape[0]
    fw = fourier_w.shape[1]
    sw = ssm_d.shape[1]
    ffn = ffn_conv_b.shape[1]
    alpha = (2.0 * depth) ** 0.25
    gd = fw // FOURIER_GROUPS
    chunk = clen
    assert seq % chunk == 0 and seq % GRID_W == 0
    ffn_pad = 256 if (ffn % 512 == 256) else 0

    ctx_row = nb
    n_rows = -(-(nb + 1) // SUBLANES) * SUBLANES
    cond = jnp.zeros((n_rows, d), F32).at[:nb].set(c).at[ctx_row].set(c_ctx)
    mods_all = _adaln_call(cond, ada_w, ada_b).reshape(depth, n_rows, 1, N_MOD * d)

    cos_c, sin_c = _dft_cos_sin(gd)
    cs_chan = _bf16_const(np.concatenate([cos_c, sin_c], axis=1))

    def pos_dft_matrix(n):
        cos_l, sin_l = _dft_cos_sin(n)
        return _bf16_const(np.concatenate([cos_l, -sin_l], axis=1))[None]

    csl_x = pos_dft_matrix(seq)
    csl_c = pos_dft_matrix(clen)

    def token_mix_tail(proj, csl, yf, yb, row_off, lw):
        ls = proj.shape[1]
        ab = _chan_dft_call(proj, cs_chan, fw)
        ab = ab.reshape(nb, 2 * ls, fw)
        if "fourier_w" not in lw:
            f, lw["fourier_w"] = _mm_call(csl, ab, BF16, "pos_dft", rider=lw.pop("ride_fourier_w"))
        else:
            f = _mm_call(csl, ab, BF16, "pos_dft")
        f = f.reshape(1, nb * ls, fw)
        if "glu_w" not in lw:
            y_f, lw["glu_w"] = _mm_call(f, lw["fourier_w"][None], BF16, "fourier_out",
                                        rider=lw.pop("ride_glu_w"))
        else:
            y_f = _mm_call(f, lw["fourier_w"][None], BF16, "fourier_out")
        y_f = y_f.reshape(nb, ls, d)
        act = _ssm_act_call(proj, yf, yb, row_off, lw["ssm_d"], fw, sw)
        if "w_out" not in lw:
            merged, lw["w_out"] = _glu_call(act, proj, lw["glu_w"], y_f, fw, sw, d,
                                            rider=lw.pop("ride_w_out"))
        else:
            merged = _glu_call(act, proj, lw["glu_w"], y_f, fw, sw, d)
        merged = merged.reshape(1, nb * ls, d)
        if "ride_next_w_in" in lw:
            out, nxt_w["w_in"] = _mm_call(merged, lw["w_out"][None], BF16, "w_out", tn_pref=512,
                                          rider=lw.pop("ride_next_w_in"))
        else:
            out = _mm_call(merged, lw["w_out"][None], BF16, "w_out")
        return out.reshape(nb, ls, d)

    def conv_ffn(h, lw, period):
        if "ride_next_w12" in lw:
            act, nxt_w["w12"] = _ffn1_call(h, lw["w12"], lw["conv_w"], lw["conv_b"], period,
                                           rider=lw.pop("ride_next_w12"))
        else:
            act = _ffn1_call(h, lw["w12"], lw["conv_w"], lw["conv_b"], period)
        return _mm_call(act, lw["w2"][None], BF16, "ffn2", tm_pref=512, tn_pref=512)

    xs = x
    cs = ctx.reshape(1, nb * clen, d)
    nxt_w = {}
    hx = _ln_mod_call(xs, mods_all[0], None, 0, 1)
    hc = _ln_mod_call(cs, mods_all[0], ctx_row, 0, 1)
    for i in range(depth):
        last = i == depth - 1
        mods = mods_all[i]
        nxt = None if last else (mods_all[i + 1], 0, 1)
        lw = {
            "ssm_d": ssm_d[i].reshape(1, sw),
            "ride_fourier_w": (_rows_cast(fourier_w, i, _tile(fw, CAST_ROWS)), fourier_w),
            "ride_glu_w": (_rows_cast(glu_w, i, _tile(sw, CAST_ROWS // 2)), glu_w),
            "ride_w_out": (_rows_cast(w_out, i, _tile(d, CAST_ROWS)), w_out),
        }
        cur_w, nxt_w = nxt_w, {}
        ride_w12 = ride_w2 = None
        if ffn_pad:
            if "w12" in cur_w:
                lw["w12"] = cur_w["w12"]
            else:
                ride_w12 = (_pad_halves_cast(ffn_w12, i, ffn_pad), ffn_w12)
            ride_w2 = (_pad_rows_cast(ffn_w2, i, ffn_pad), ffn_w2)
            if not last:
                lw["ride_next_w_in"] = (_rows_cast(w_in, i + 1, _tile(d, CAST_ROWS // 4)), w_in)
                lw["ride_next_w12"] = (_pad_halves_cast(ffn_w12, i + 1, ffn_pad), ffn_w12)
        else:
            lw["w12"] = _cast_call(ffn_w12, i)
            lw["w2"] = _cast_call(ffn_w2, i)
        lw["conv_w"] = jnp.pad(ffn_conv_w[i], ((0, 0), (0, ffn_pad)))
        lw["conv_b"] = jnp.pad(ffn_conv_b[i].reshape(1, ffn), ((0, 0), (0, ffn_pad)))
        g1, b1 = ln1_g[i].reshape(1, d), ln1_b[i].reshape(1, d)
        g2, b2 = ln2_g[i].reshape(1, d), ln2_b[i].reshape(1, d)
        bmat, cmat, amat = _ssm_tables(ssm_a_re[i], ssm_a_im[i], ssm_log_dt[i], ssm_b_re[i],
                                       ssm_b_im[i], ssm_c_re[i], ssm_c_im[i])

        n_in = w_in.shape[2]
        hc2d = hc.reshape(nb * clen, d)
        c_col = 0 if last else fw
        if "w_in" in cur_w:
            w_in_b = cur_w["w_in"]
            proj_x = _mm_call(hx.reshape(1, nb * seq, d), w_in_b[None], BF16, "w_in", rider=ride_w12)
            w_in_c = w_in_b[None, :, fw:fw + sw] if last else w_in_b[None]
            proj_c = _mm_call(hc2d[None], w_in_c, BF16, "w_in_ctx")
        else:
            proj_x = _mm_w32_call(hx.reshape(nb * seq, d), w_in, i, 0, n_in, BF16, "w_in", ride_w12)
            if last:
                proj_c = _mm_w32_call(hc2d, w_in, i, fw, sw, BF16, "w_in_ctx")
            else:
                proj_c = _mm_w32_call(hc2d, w_in, i, 0, n_in, BF16, "w_in_ctx")
        if ride_w12 is not None:
            proj_x, lw["w12"] = proj_x
        proj_x = proj_x.reshape(nb, seq, n_in)
        proj_c = proj_c.reshape(nb, clen, -1)
        ssm_out = _ssm_call(proj_x, fw, proj_c, c_col, bmat, cmat, amat, chunk, ride_w2)
        yf, yb = ssm_out[0], ssm_out[1]
        if ride_w2 is not None:
            lw["w2"] = ssm_out[2]

        out_x = token_mix_tail(proj_x, csl_x, yf, yb, 0, lw)
        x1, h2 = _postnorm_call(xs, out_x, mods, None, 2, g1, b1, alpha, (mods, 3, 4))
        ffn_x = conv_ffn(h2, lw, GRID_W)
        xs, hx = _postnorm_call(x1, ffn_x, mods, None, 5, g2, b2, alpha, nxt)

        if not last:
            out_c = token_mix_tail(proj_c, csl_c, yf, yb, seq, lw).reshape(1, nb * clen, d)
            c1, hc2 = _postnorm_call(cs, out_c, mods, ctx_row, 2, g1, b1, alpha, (mods, 3, 4))
            ffn_c = conv_ffn(hc2, lw, clen)
            cs, hc = _postnorm_call(c1, ffn_c, mods, ctx_row, 5, g2, b2, alpha, nxt)
    return xs
```

```python
import functools
import math
from typing import Callable, NamedTuple

import jax
import jax.numpy as jnp
import ml_dtypes
import numpy as np
from jax import lax
from jax.experimental import pallas as pl
from jax.experimental.pallas import tpu as pltpu

GRID_W = 64
FOURIER_GROUPS = 4
N_MOD = 6
LN_EPS = 1e-6
SSM_BLOCK_GROUPS = 16
LANES = 128
SUBLANES = 8
VMEM_LIMIT = 56 * 1024 * 1024
GLU_SUBTILE = 256
CAST_ROWS = 256
SSM_MM_ROWS = 256

F32 = jnp.float32
BF16 = jnp.bfloat16


def _params(sem):
    return pltpu.CompilerParams(dimension_semantics=sem, vmem_limit_bytes=VMEM_LIMIT)


def _tile(n, pref):
    if n <= pref:
        return n
    while n % pref:
        pref //= 2
    assert pref >= SUBLANES, (n, pref)
    return pref


def _lane_tile(n, cap):
    for t in range(cap - cap % LANES, 0, -LANES):
        if n % t == 0:
            return t
    return n


def _dot(a, b):
    return jnp.dot(a, b, preferred_element_type=F32)


def _layer_norm(x):
    mu = jnp.mean(x, axis=-1, keepdims=True)
    xc = x - mu
    var = jnp.mean(xc * xc, axis=-1, keepdims=True)
    return xc * lax.rsqrt(var + LN_EPS)


def _mod_spec(mod_row, k, d):
    if mod_row is None:
        return pl.BlockSpec((None, 1, d), lambda b, *_: (b, 0, k))
    return pl.BlockSpec((None, 1, d), lambda b, *_: (mod_row, 0, k))


def _cast_kernel(w_ref, o_ref):
    o_ref[...] = w_ref[...].astype(o_ref.dtype)


def _cast_call(w, layer):
    _, r, c = w.shape
    tr = _tile(r, 512)
    tc = _lane_tile(c, 6144)
    return pl.pallas_call(
        _cast_kernel,
        grid=(r // tr, c // tc),
        in_specs=[pl.BlockSpec((None, tr, tc), lambda i, j: (layer, i, j))],
        out_specs=pl.BlockSpec((tr, tc), lambda i, j: (i, j)),
        out_shape=jax.ShapeDtypeStruct((r, c), BF16),
        compiler_params=_params(("arbitrary", "arbitrary")),
        name="cast_w",
    )(w)


class _PadCast(NamedTuple):
    layer: int
    in_block: tuple
    in_index: Callable
    out_block: tuple
    out_index: Callable
    out_shape: tuple
    nblk: int
    is_data: Callable


def _rows_cast(w, layer, tr):
    _, r, c = w.shape
    assert r % tr == 0
    return _PadCast(layer, (None, tr, c), lambda s: (s, 0), (tr, c), lambda s: (s, 0), (r, c),
                    r // tr, lambda s: s >= 0)


def _pad_rows_cast(w, layer, tr):
    _, r, c = w.shape
    nvalid = r // tr
    assert r % tr == 0
    return _PadCast(layer, (None, tr, c), lambda s: (jnp.minimum(s, nvalid - 1), 0),
                    (tr, c), lambda s: (s, 0), (r + tr, c), nvalid + 1, lambda s: s < nvalid)


def _pad_halves_cast(w, layer, tc):
    _, r, c2 = w.shape
    f = c2 // 2
    nvalid = f // tc
    assert f % tc == 0
    per = nvalid + 1
    return _PadCast(layer, (None, r, tc),
                    lambda s: (0, (s // per) * nvalid + jnp.minimum(s % per, nvalid - 1)),
                    (r, tc), lambda s: (0, s), (r, 2 * (f + tc)), 2 * per, lambda s: s % per < nvalid)


def _cast_specs(job, step_of):
    def clamped(*g):
        return jnp.minimum(step_of(*g), job.nblk - 1)

    in_spec = pl.BlockSpec(job.in_block, lambda *g: (job.layer,) + tuple(job.in_index(clamped(*g))))
    out_spec = pl.BlockSpec(job.out_block, lambda *g: tuple(job.out_index(clamped(*g))))
    return in_spec, out_spec


def _cast_step(job, step, w_ref, o_ref):
    @pl.when(step < job.nblk)
    def _():
        o_ref[...] = jnp.where(job.is_data(step), w_ref[...], 0.0).astype(o_ref.dtype)


def _rider_fits(rider, nsteps):
    return rider[0].nblk <= nsteps


def _attach_rider(rider, step_of, in_specs, out_specs, out_shape, args):
    job, rw = rider
    r_in, r_out = _cast_specs(job, step_of)
    in_specs.append(r_in)
    out_specs.append(r_out)
    out_shape.append(jax.ShapeDtypeStruct(job.out_shape, BF16))
    args.append(rw)
    return job


def _pad_cast_kernel(w_ref, o_ref, *, job):
    _cast_step(job, pl.program_id(0), w_ref, o_ref)


def _pad_cast_call(job, w):
    in_spec, out_spec = _cast_specs(job, lambda s: s)
    return pl.pallas_call(
        functools.partial(_pad_cast_kernel, job=job),
        grid=(job.nblk,),
        in_specs=[in_spec],
        out_specs=out_spec,
        out_shape=jax.ShapeDtypeStruct(job.out_shape, BF16),
        compiler_params=_params(("arbitrary",)),
        name="cast_w_pad",
    )(w)


def _adaln_kernel(c_ref, w_ref, b_ref, o_ref):
    c = c_ref[...]
    s = (c * jax.nn.sigmoid(c)).astype(BF16)
    o_ref[...] = _dot(s, w_ref[...].astype(BF16)) + b_ref[...]


class _ModJob(NamedTuple):
    layer: int
    tcol: int
    nblk: int


def _mod_job(ada_w, layer, nsteps):
    n = ada_w.shape[2]
    for tcol in range(LANES, n + 1, LANES):
        if n % tcol == 0 and n // tcol <= nsteps:
            return _ModJob(layer, tcol, n // tcol)
    return None


def _mod_attach(job, operands, step_of, in_specs, out_specs, out_shape, args):
    cond, ada_w, ada_b = operands
    r, d = cond.shape

    def clamped(*g):
        return jnp.minimum(step_of(*g), job.nblk - 1)

    in_specs += [
        pl.BlockSpec((r, d), lambda *g: (0, 0)),
        pl.BlockSpec((None, d, job.tcol), lambda *g: (job.layer, 0, clamped(*g))),
        pl.BlockSpec((None, 1, job.tcol), lambda *g: (job.layer, 0, clamped(*g))),
    ]
    out_specs.append(pl.BlockSpec((r, job.tcol), lambda *g: (0, clamped(*g))))
    out_shape.append(jax.ShapeDtypeStruct((r, ada_w.shape[2]), F32))
    args += [cond, ada_w, ada_b.reshape(ada_b.shape[0], 1, -1)]


def _mod_step(job, step, c_ref, w_ref, b_ref, o_ref):
    @pl.when(step < job.nblk)
    def _():
        _adaln_kernel(c_ref, w_ref, b_ref, o_ref)


def _adaln_call(cond, ada_w, ada_b, layer_lo=0, nlayer=None):
    all_layers, d, n = ada_w.shape
    depth = all_layers - layer_lo if nlayer is None else nlayer
    r = cond.shape[0]
    tn = _tile(n, 512)
    return pl.pallas_call(
        _adaln_kernel,
        grid=(depth, n // tn),
        in_specs=[
            pl.BlockSpec((r, d), lambda l, j: (0, 0)),
            pl.BlockSpec((None, d, tn), lambda l, j: (layer_lo + l, 0, j)),
            pl.BlockSpec((None, 1, tn), lambda l, j: (layer_lo + l, 0, j)),
        ],
        out_specs=pl.BlockSpec((None, r, tn), lambda l, j: (l, 0, j)),
        out_shape=jax.ShapeDtypeStruct((depth, r, n), F32),
        compiler_params=_params(("arbitrary", "arbitrary")),
        name="adaln",
    )(cond, ada_w, ada_b.reshape(all_layers, 1, n))


def _ln_mod_kernel(x_ref, sh_ref, sc_ref, o_ref):
    y = _layer_norm(x_ref[...])
    o_ref[...] = (y * (1.0 + sc_ref[...]) + sh_ref[...]).astype(o_ref.dtype)


def _ln_mod_call(x, mods, mod_row, k_shift, k_scale):
    bm, lm, d = x.shape
    tm = _tile(lm, 256)
    return pl.pallas_call(
        _ln_mod_kernel,
        grid=(bm, lm // tm),
        in_specs=[
            pl.BlockSpec((None, tm, d), lambda b, i: (b, i, 0)),
            _mod_spec(mod_row, k_shift, d),
            _mod_spec(mod_row, k_scale, d),
        ],
        out_specs=pl.BlockSpec((None, tm, d), lambda b, i: (b, i, 0)),
        out_shape=jax.ShapeDtypeStruct((bm, lm, d), BF16),
        compiler_params=_params(("arbitrary", "arbitrary")),
        name="ln_mod",
    )(x, mods, mods)


def _postnorm_kernel(res_ref, pre_ref, gate_ref, g_ref, b_ref, *rest, alpha, with_next):
    z = alpha * res_ref[...] + gate_ref[...] * pre_ref[...].astype(F32)
    x_new = _layer_norm(z) * g_ref[...] + b_ref[...]
    if with_next:
        sh_ref, sc_ref, o_ref, h_ref = rest
        o_ref[...] = x_new
        h_ref[...] = (_layer_norm(x_new) * (1.0 + sc_ref[...]) + sh_ref[...]).astype(h_ref.dtype)
    else:
        (o_ref,) = rest
        o_ref[...] = x_new


def _postnorm_call(res, pre, mods, mod_row, k_gate, ln_g, ln_b, alpha, next_mod=None):
    bm, lm, d = res.shape
    tm = _tile(lm, 256)
    row = pl.BlockSpec((None, tm, d), lambda b, i: (b, i, 0))
    vec = pl.BlockSpec((1, d), lambda b, i: (0, 0))
    in_specs = [row, row, _mod_spec(mod_row, k_gate, d), vec, vec]
    args = [res, pre, mods, ln_g, ln_b]
    out_specs = [row]
    out_shape = [jax.ShapeDtypeStruct((bm, lm, d), F32)]
    if next_mod is not None:
        nmods, k_shift, k_scale = next_mod
        in_specs += [_mod_spec(mod_row, k_shift, d), _mod_spec(mod_row, k_scale, d)]
        args += [nmods, nmods]
        out_specs.append(row)
        out_shape.append(jax.ShapeDtypeStruct((bm, lm, d), BF16))
    out = pl.pallas_call(
        functools.partial(_postnorm_kernel, alpha=alpha, with_next=next_mod is not None),
        grid=(bm, lm // tm),
        in_specs=in_specs,
        out_specs=out_specs,
        out_shape=out_shape,
        compiler_params=_params(("arbitrary", "arbitrary")),
        name="postnorm",
    )(*args)
    return (out[0], out[1]) if next_mod is not None else (out[0], None)


def _mm_kernel(a_ref, w_ref, *rest, ni, nj, job, mod_job):
    n_side_in = (job is not None) + 3 * (mod_job is not None)
    side_in, (o_ref, *side_out) = rest[:n_side_in], rest[n_side_in:]
    step = (pl.program_id(0) * ni + pl.program_id(1)) * nj + pl.program_id(2)
    if job is not None:
        _cast_step(job, step, side_in[0], side_out[0])
    if mod_job is not None:
        _mod_step(mod_job, step, *side_in[-3:], side_out[-1])
    o_ref[...] = _dot(a_ref[...], w_ref[...]).astype(o_ref.dtype)


def _mm_call(a, w, out_dtype, name, tm_pref=1024, tn_pref=1024, rider=None, mod_rider=None):
    ba, m, k = a.shape
    bw, _, n = w.shape
    nb = max(ba, bw)
    tm = _tile(m, tm_pref)
    tn = _tile(n, tn_pref)
    ni, nj = m // tm, n // tn
    if rider is not None and not _rider_fits(rider, nb * ni * nj):
        out = _mm_call(a, w, out_dtype, name, tm_pref, tn_pref, mod_rider=mod_rider)
        out = out if mod_rider is not None else (out,)
        return (out[0], _pad_cast_call(*rider)) + tuple(out[1:])
    a_map = (lambda b, i, j: (b, i, 0)) if ba > 1 else (lambda b, i, j: (0, i, 0))
    w_map = (lambda b, i, j: (b, 0, j)) if bw > 1 else (lambda b, i, j: (0, 0, j))
    in_specs = [pl.BlockSpec((None, tm, k), a_map), pl.BlockSpec((None, k, tn), w_map)]
    out_specs = [pl.BlockSpec((None, tm, tn), lambda b, i, j: (b, i, j))]
    out_shape = [jax.ShapeDtypeStruct((nb, m, n), out_dtype)]
    args = [a, w]
    job = mod_job = None

    def step_of(b, i, j):
        return (b * ni + i) * nj + j

    if rider is not None:
        job = _attach_rider(rider, step_of, in_specs, out_specs, out_shape, args)
    if mod_rider is not None:
        mod_job, operands = mod_rider
        assert mod_job.nblk <= nb * ni * nj
        _mod_attach(mod_job, operands, step_of, in_specs, out_specs, out_shape, args)
    out = pl.pallas_call(
        functools.partial(_mm_kernel, ni=ni, nj=nj, job=job, mod_job=mod_job),
        grid=(nb, ni, nj),
        in_specs=in_specs,
        out_specs=out_specs,
        out_shape=out_shape,
        compiler_params=_params(("arbitrary", "arbitrary", "arbitrary")),
        name=name,
    )(*args)
    return out if len(out) > 1 else out[0]


def _mm_w32_kernel(a_ref, w_ref, *rest, ni, job):
    if job is None:
        o_ref, w_scr = rest
    else:
        rw_ref, o_ref, ro_ref, w_scr = rest
    i = pl.program_id(1)

    @pl.when(i == 0)
    def _():
        w_scr[...] = w_ref[...].astype(w_scr.dtype)

    if job is not None:
        _cast_step(job, pl.program_id(0) * ni + i, rw_ref, ro_ref)
    o_ref[...] = _dot(a_ref[...], w_scr[...]).astype(o_ref.dtype)


def _mm_w32_call(a, w, layer, col0, ncols, out_dtype, name, rider=None):
    m, k = a.shape
    tm = _tile(m, 1024)
    tn = _tile(ncols, 512)
    assert col0 % tn == 0
    j0 = col0 // tn
    ni = m // tm
    in_specs = [
        pl.BlockSpec((tm, k), lambda j, i: (i, 0)),
        pl.BlockSpec((None, k, tn), lambda j, i: (layer, 0, j0 + j)),
    ]
    out_specs = [pl.BlockSpec((tm, tn), lambda j, i: (i, j))]
    out_shape = [jax.ShapeDtypeStruct((m, ncols), out_dtype)]
    args = [a, w]
    job = None
    if rider is not None and not _rider_fits(rider, (ncols // tn) * ni):
        return _mm_w32_call(a, w, layer, col0, ncols, out_dtype, name), _pad_cast_call(*rider)
    if rider is not None:
        job = _attach_rider(rider, lambda j, i: j * ni + i, in_specs, out_specs, out_shape, args)
    out = pl.pallas_call(
        functools.partial(_mm_w32_kernel, ni=ni, job=job),
        grid=(ncols // tn, ni),
        in_specs=in_specs,
        out_specs=out_specs,
        out_shape=out_shape,
        scratch_shapes=[pltpu.VMEM((k, tn), BF16)],
        compiler_params=_params(("arbitrary", "arbitrary")),
        name=name,
    )(*args)
    return out if rider is not None else out[0]


def _chan_dft_kernel(u_ref, cs_ref, o_ref, *, gd):
    r = _dot(u_ref[...], cs_ref[...])
    o_ref[0] = r[:, :gd].astype(o_ref.dtype)
    o_ref[1] = r[:, gd:].astype(o_ref.dtype)


def _chan_dft_call(proj, cs, fw):
    bs, ls, _ = proj.shape
    gd = fw // FOURIER_GROUPS
    tm = _tile(ls, 1024)
    return pl.pallas_call(
        functools.partial(_chan_dft_kernel, gd=gd),
        grid=(bs, ls // tm, FOURIER_GROUPS),
        in_specs=[
            pl.BlockSpec((None, tm, gd), lambda b, i, g: (b, i, g)),
            pl.BlockSpec((gd, 2 * gd), lambda b, i, g: (0, 0)),
        ],
        out_specs=pl.BlockSpec((None, 2, tm, gd), lambda b, i, g: (b, 0, i, g)),
        out_shape=jax.ShapeDtypeStruct((bs, 2, ls, fw), BF16),
        compiler_params=_params(("arbitrary", "arbitrary", "arbitrary")),
        name="chan_dft",
    )(proj, cs)


def _ssm_kernel(uxf_ref, uxb_ref, ucf_ref, ucb_ref, bm_ref, cm_ref, a_ref, *rest,
                chunk, nbat, half, n_ctx, n_all, job):
    side_work = None
    if job is None:
        yf_ref, yb_ref, *scratch = rest
    else:
        rw_ref, yf_ref, yb_ref, ro_ref, *scratch = rest
        step = pl.program_id(0) * n_all + pl.program_id(1)
        side_work = functools.partial(_cast_step, job, step, rw_ref, ro_ref)
    _ssm_step(uxf_ref, uxb_ref, ucf_ref, ucb_ref, bm_ref, cm_ref, a_ref, yf_ref, yb_ref,
              *scratch, chunk=chunk, nbat=nbat, half=half, n_ctx=n_ctx, side_work=side_work)


def _ssm_step(uxf_ref, uxb_ref, ucf_ref, ucb_ref, bm_ref, cm_ref, a_ref, yf_ref, yb_ref,
              u_scr, uin_scr, s_scr, y_scr, h_scr, *, chunk, nbat, half, n_ctx, side_work):
    k = pl.program_id(0)
    c = pl.program_id(1)
    rows = SUBLANES
    hw = half // 2
    mm_rows = min(chunk * rows, SSM_MM_ROWS)
    fwd_starts = list(range(0, chunk * rows, mm_rows))
    row_starts = (fwd_starts, fwd_starts[::-1])

    @pl.when(jnp.logical_and(k == 0, c == 0))
    def _():
        uin_scr[...] = jnp.zeros_like(uin_scr)

    @pl.when(c == 0)
    def _():
        h_scr[...] = jnp.zeros_like(h_scr)

    @pl.when(c < n_ctx)
    def _():
        u_scr[0] = ucf_ref[...]
        u_scr[1] = ucb_ref[...]

    @pl.when(c >= n_ctx)
    def _():
        u_scr[0] = uxf_ref[...]
        u_scr[1] = uxb_ref[...]

    if side_work is not None:
        side_work()
    for d in range(2):
        for b in range(nbat):
            ub = u_scr[d, b].astype(F32)
            for h in range(2):
                uin_scr[d, h, pl.ds(2 * b + h, chunk, stride=rows), :] = ub[:, h * LANES:(h + 1) * LANES]
        bm = bm_ref[d]
        for r0 in row_starts[d]:
            rs = slice(r0, r0 + mm_rows)
            lhs = jnp.concatenate([uin_scr[d, 0, rs], uin_scr[d, 1, rs]], axis=1).astype(BF16)
            s_scr[d, rs] = _dot(lhs, bm)

    arf, aif = a_ref[0, 0], a_ref[0, 1]
    arb, aib = a_ref[1, 0], a_ref[1, 1]

    def body(t, carry):
        hrf, hif, hrb, hib = carry
        rowf = t * rows
        rowb = (chunk - 1 - t) * rows
        xf = s_scr[0, pl.ds(rowf, rows), :]
        xb = s_scr[1, pl.ds(rowb, rows), :]
        nrf = arf * hrf - aif * hif + xf[:, :hw]
        nif = arf * hif + aif * hrf + xf[:, hw:]
        nrb = arb * hrb - aib * hib + xb[:, :hw]
        nib = arb * hib + aib * hrb + xb[:, hw:]
        s_scr[0, pl.ds(rowf, rows), :hw] = nrf
        s_scr[0, pl.ds(rowf, rows), hw:] = nif
        s_scr[1, pl.ds(rowb, rows), :hw] = nrb
        s_scr[1, pl.ds(rowb, rows), hw:] = nib
        return nrf, nif, nrb, nib

    init = (h_scr[0, :, :hw], h_scr[0, :, hw:], h_scr[1, :, :hw], h_scr[1, :, hw:])
    carry = init
    for t in range(chunk):
        carry = body(t, carry)
    hrf, hif, hrb, hib = carry
    h_scr[0, :, :hw] = hrf
    h_scr[0, :, hw:] = hif
    h_scr[1, :, :hw] = hrb
    h_scr[1, :, hw:] = hib

    for d, y_ref in enumerate((yf_ref, yb_ref)):
        cm = cm_ref[d]
        for r0 in row_starts[d]:
            rs = slice(r0, r0 + mm_rows)
            out = _dot(s_scr[d, rs].astype(BF16), cm)
            y_scr[d, 0, rs] = out[:, :LANES]
            y_scr[d, 1, rs] = out[:, LANES:]
        for b in range(nbat):
            for h in range(2):
                piece = y_scr[d, h, pl.ds(2 * b + h, chunk, stride=rows), :]
                y_ref[b, :, h * LANES:(h + 1) * LANES] = piece.astype(y_ref.dtype)


def _ssm_call(ux, ux_col, uc, uc_col, bmat, cmat, amat, chunk, rider=None):
    nbat, seq, _ = ux.shape
    clen = uc.shape[1]
    assert 2 * nbat == SUBLANES, "the scan packs 2 column halves x batch on the 8 sublanes"
    ndir, nblk, cin, cst = bmat.shape
    assert cin == 2 * LANES, "each column half of a group block is one 128-lane input slab"
    sw = nblk * cin
    n_lat, n_ctx = seq // chunk, clen // chunk
    n_all = n_lat + n_ctx
    assert ux_col % cin == 0 and uc_col % cin == 0
    xo, co = ux_col // cin, uc_col // cin

    def lat_f(c):
        return jnp.maximum(c - n_ctx, 0)

    def lat_b(c):
        return n_lat - 1 - jnp.maximum(c - n_ctx, 0)

    def ctx_f(c):
        return jnp.minimum(c, n_ctx - 1)

    def ctx_b(c):
        return jnp.maximum(n_ctx - 1 - c, 0)

    def out_f(c):
        return jnp.where(c < n_ctx, n_lat + c, c - n_ctx)

    def out_b(c):
        return jnp.where(c < n_ctx, n_lat + n_ctx - 1 - c, n_lat - 1 - (c - n_ctx))

    ublk = (nbat, chunk, cin)
    in_specs = [
        pl.BlockSpec(ublk, lambda k, c: (0, lat_f(c), xo + k)),
        pl.BlockSpec(ublk, lambda k, c: (0, lat_b(c), xo + k)),
        pl.BlockSpec(ublk, lambda k, c: (0, ctx_f(c), co + k)),
        pl.BlockSpec(ublk, lambda k, c: (0, ctx_b(c), co + k)),
        pl.BlockSpec((ndir, None, cin, cst), lambda k, c: (0, k, 0, 0)),
        pl.BlockSpec((ndir, None, cst, cin), lambda k, c: (0, k, 0, 0)),
        pl.BlockSpec((ndir, None, 2, SUBLANES, cst // 2), lambda k, c: (0, k, 0, 0, 0)),
    ]
    out_specs = [
        pl.BlockSpec(ublk, lambda k, c: (0, out_f(c), k)),
        pl.BlockSpec(ublk, lambda k, c: (0, out_b(c), k)),
    ]
    out_shape = [jax.ShapeDtypeStruct((nbat, seq + clen, sw), BF16)] * 2
    args = [ux, ux, uc, uc, bmat, cmat, amat]
    job = None
    if rider is not None and not _rider_fits(rider, nblk * n_all):
        yf, yb = _ssm_call(ux, ux_col, uc, uc_col, bmat, cmat, amat, chunk)
        return yf, yb, _pad_cast_call(*rider)
    if rider is not None:
        job = _attach_rider(rider, lambda k, c: k * n_all + c, in_specs, out_specs, out_shape, args)
    return pl.pallas_call(
        functools.partial(_ssm_kernel, chunk=chunk, nbat=nbat, half=cst, n_ctx=n_ctx,
                          n_all=n_all, job=job),
        grid=(nblk, n_all),
        in_specs=in_specs,
        out_specs=out_specs,
        out_shape=out_shape,
        scratch_shapes=[
            pltpu.VMEM((ndir, nbat, chunk, cin), BF16),
            pltpu.VMEM((ndir, 2, SUBLANES * chunk, LANES), F32),
            pltpu.VMEM((ndir, SUBLANES * chunk, cst), F32),
            pltpu.VMEM((ndir, 2, SUBLANES * chunk, LANES), F32),
            pltpu.VMEM((ndir, SUBLANES, cst), F32),
        ],
        compiler_params=_params(("arbitrary", "arbitrary")),
        name="ssm",
    )(*args)


def _ssm_tables(a_re, a_im, log_dt, b_re, b_im, c_re, c_im):
    ndir, g, p = a_re.shape
    cg = b_re.shape[-1]
    gb = SSM_BLOCK_GROUPS
    nblk = g // gb
    hg = gb // 2
    npair = hg // 2
    assert 2 * p == LANES and g % gb == 0 and hg * cg == LANES
    lr, li = a_re.astype(F32), a_im.astype(F32)
    dt = jnp.exp(log_dt.astype(F32))[..., None]
    zr, zi = lr * dt, li * dt
    a_bar_r = jnp.exp(zr) * jnp.cos(zi)
    a_bar_i = jnp.exp(zr) * jnp.sin(zi)
    em1_r = jnp.expm1(zr) * jnp.cos(zi) - 2.0 * jnp.square(jnp.sin(0.5 * zi))
    den = lr * lr + li * li
    cf_r = ((em1_r * lr + a_bar_i * li) / den)[..., None]
    cf_i = ((a_bar_i * lr - em1_r * li) / den)[..., None]
    br, bi = b_re.astype(F32), b_im.astype(F32)
    b_bar_r = cf_r * br - cf_i * bi
    b_bar_i = cf_r * bi + cf_i * br

    lane_q = np.arange(LANES) // p
    own = (np.arange(gb)[:, None, None] % hg
           == 2 * np.arange(npair)[None, :, None] + lane_q[None, None, :])
    own_in = np.repeat(own, cg, axis=0).astype(np.float32)
    own_out = np.ascontiguousarray(own_in.transpose(1, 2, 0))

    def in_mat(bb):
        t = jnp.swapaxes(bb.reshape(ndir, nblk, gb, p, cg), -1, -2).reshape(ndir, nblk, gb * cg, p)
        t = jnp.tile(t, (1, 1, 1, 2))[:, :, :, None, :] * own_in
        return t.reshape(ndir, nblk, gb * cg, npair * LANES)

    def out_mat(cc):
        t = jnp.swapaxes(cc.reshape(ndir, nblk, gb * cg, p), -1, -2)
        t = jnp.tile(t, (1, 1, 2, 1))[:, :, None, :, :] * own_out
        return t.reshape(ndir, nblk, npair * LANES, gb * cg)

    bmat = jnp.concatenate([in_mat(b_bar_r), in_mat(b_bar_i)], axis=-1).astype(BF16)
    cmat = jnp.concatenate([out_mat(c_re.astype(F32)), out_mat(-c_im.astype(F32))], axis=-2).astype(BF16)

    def decay(a):
        t = a.reshape(ndir, nblk, 2, npair * LANES)
        return jnp.tile(t, (1, 1, SUBLANES // 2, 1))
    aa = jnp.stack([decay(a_bar_r), decay(a_bar_i)], axis=2)
    return bmat, cmat, aa


def _ssm_act_kernel(u_ref, yf_ref, yb_ref, d_ref, o_ref):
    y = d_ref[...] * u_ref[...].astype(F32) + yf_ref[...].astype(F32) + yb_ref[...].astype(F32)
    o_ref[...] = jax.nn.gelu(y).astype(o_ref.dtype)


def _ssm_act_call(proj, yf, yb, row_off, ssm_d, fw, sw):
    bs, ls, _ = proj.shape
    tm = _tile(ls, 512)
    assert fw % sw == 0 and row_off % tm == 0
    u_blk, r_blk = fw // sw, row_off // tm
    return pl.pallas_call(
        _ssm_act_kernel,
        grid=(bs, ls // tm),
        in_specs=[
            pl.BlockSpec((None, tm, sw), lambda b, i: (b, i, u_blk)),
            pl.BlockSpec((None, tm, sw), lambda b, i: (b, i + r_blk, 0)),
            pl.BlockSpec((None, tm, sw), lambda b, i: (b, i + r_blk, 0)),
            pl.BlockSpec((1, sw), lambda b, i: (0, 0)),
        ],
        out_specs=pl.BlockSpec((None, tm, sw), lambda b, i: (b, i, 0)),
        out_shape=jax.ShapeDtypeStruct((bs, ls, sw), BF16),
        compiler_params=_params(("arbitrary", "arbitrary")),
        name="ssm_act",
    )(proj, yf, yb, ssm_d)


def _glu_kernel(a_ref, wv_ref, wg_ref, gf_ref, gs_ref, fo_ref, *rest, nb, ni, job):
    if job is None:
        (o_ref,) = rest
    else:
        rw_ref, o_ref, ro_ref = rest
        step = (pl.program_id(0) * nb + pl.program_id(1)) * ni + pl.program_id(2)
        _cast_step(job, step, rw_ref, ro_ref)
    a = a_ref[...]
    tn = o_ref.shape[1]
    sub = min(tn, GLU_SUBTILE)
    for n0 in range(0, tn, sub):
        cols = slice(n0, n0 + sub)
        y_s = (_dot(a, wv_ref[:, cols]) * jax.nn.sigmoid(_dot(a, wg_ref[:, cols]))).astype(BF16)
        o_ref[:, cols] = (jax.nn.sigmoid(gf_ref[:, cols]) * fo_ref[:, cols]
                          + jax.nn.sigmoid(gs_ref[:, cols]) * y_s)


def _glu_call(act, proj, glu_w, y_f, fw, sw, d, rider=None):
    bs, ls, _ = act.shape
    tm = _tile(ls, 512)
    tn = _tile(d, 1024)
    assert (fw + sw) % tn == 0
    gf_blk = (fw + sw) // tn
    gs_blk = (fw + sw + d) // tn
    ni, nj = ls // tm, d // tn
    if rider is not None and not _rider_fits(rider, bs * ni * nj):
        return _glu_call(act, proj, glu_w, y_f, fw, sw, d), _pad_cast_call(*rider)
    in_specs = [
        pl.BlockSpec((None, tm, sw), lambda j, b, i: (b, i, 0)),
        pl.BlockSpec((sw, tn), lambda j, b, i: (0, j)),
        pl.BlockSpec((sw, tn), lambda j, b, i: (0, nj + j)),
        pl.BlockSpec((None, tm, tn), lambda j, b, i: (b, i, gf_blk + j)),
        pl.BlockSpec((None, tm, tn), lambda j, b, i: (b, i, gs_blk + j)),
        pl.BlockSpec((None, tm, tn), lambda j, b, i: (b, i, j)),
    ]
    out_specs = [pl.BlockSpec((None, tm, tn), lambda j, b, i: (b, i, j))]
    out_shape = [jax.ShapeDtypeStruct((bs, ls, d), BF16)]
    args = [act, glu_w, glu_w, proj, proj, y_f]
    job = None
    if rider is not None:
        job = _attach_rider(rider, lambda j, b, i: (j * bs + b) * ni + i,
                            in_specs, out_specs, out_shape, args)
    out = pl.pallas_call(
        functools.partial(_glu_kernel, nb=bs, ni=ni, job=job),
        grid=(nj, bs, ni),
        in_specs=in_specs,
        out_specs=out_specs,
        out_shape=out_shape,
        compiler_params=_params(("arbitrary", "arbitrary", "arbitrary")),
        name="glu_merge",
    )(*args)
    return out if rider is not None else out[0]


def _ffn1_kernel(h_ref, wu_ref, wv_ref, cw_ref, cb_ref, *rest, period, ni, nj, job):
    if job is None:
        (o_ref,) = rest
    else:
        rw_ref, o_ref, ro_ref = rest
        step = (pl.program_id(0) * ni + pl.program_id(1)) * nj + pl.program_id(2)
        _cast_step(job, step, rw_ref, ro_ref)
    h = h_ref[...]
    u = _dot(h, wu_ref[...])
    v = _dot(h, wv_ref[...])
    tm = u.shape[0]
    pos = lax.broadcasted_iota(jnp.int32, u.shape, 0) % period
    prev = jnp.where(pos == 0, 0.0, pltpu.roll(u, 1, axis=0))
    nxt = jnp.where(pos == period - 1, 0.0, pltpu.roll(u, tm - 1, axis=0))
    cw = cw_ref[...]
    conv = cb_ref[...] + cw[0:1] * prev + cw[1:2] * u + cw[2:3] * nxt
    o_ref[...] = (jax.nn.gelu(conv) * v).astype(o_ref.dtype)


def _ffn1_call(h, w12, conv_w, conv_b, period, rider=None):
    bm, lm, d = h.shape
    ffn = w12.shape[1] // 2
    tm = _tile(lm, 1024)
    tn = _tile(ffn, 512)
    ni, nj = lm // tm, ffn // tn
    assert tm % period == 0
    if rider is not None and not _rider_fits(rider, bm * ni * nj):
        return _ffn1_call(h, w12, conv_w, conv_b, period), _pad_cast_call(*rider)
    in_specs = [
        pl.BlockSpec((None, tm, d), lambda b, i, j: (b, i, 0)),
        pl.BlockSpec((d, tn), lambda b, i, j: (0, j)),
        pl.BlockSpec((d, tn), lambda b, i, j: (0, nj + j)),
        pl.BlockSpec((3, tn), lambda b, i, j: (0, j)),
        pl.BlockSpec((1, tn), lambda b, i, j: (0, j)),
    ]
    out_specs = [pl.BlockSpec((None, tm, tn), lambda b, i, j: (b, i, j))]
    out_shape = [jax.ShapeDtypeStruct((bm, lm, ffn), BF16)]
    args = [h, w12, w12, conv_w, conv_b]
    job = None
    if rider is not None:
        job = _attach_rider(rider, lambda b, i, j: (b * ni + i) * nj + j,
                            in_specs, out_specs, out_shape, args)
    out = pl.pallas_call(
        functools.partial(_ffn1_kernel, period=period, ni=ni, nj=nj, job=job),
        grid=(bm, ni, nj),
        in_specs=in_specs,
        out_specs=out_specs,
        out_shape=out_shape,
        compiler_params=_params(("arbitrary", "arbitrary", "arbitrary")),
        name="ffn1",
    )(*args)
    return out if rider is not None else out[0]


def _dft_cos_sin(n):
    idx = np.arange(n, dtype=np.int64)
    ang = (2.0 * math.pi / n) * ((idx[:, None] * idx[None, :]) % n)
    scale = 1.0 / math.sqrt(n)
    return np.cos(ang) * scale, np.sin(ang) * scale


def _bf16_const(a):
    return jnp.asarray(np.asarray(a, np.float32).astype(ml_dtypes.bfloat16))


def kernel(x, c, ctx, c_ctx, ada_w, ada_b, w_in, fourier_w, ssm_a_re, ssm_a_im, ssm_log_dt,
           ssm_b_re, ssm_b_im, ssm_c_re, ssm_c_im, ssm_d, glu_w, w_out, ln1_g, ln1_b,
           ffn_w12, ffn_conv_w, ffn_conv_b, ffn_w2, ln2_g, ln2_b):
    nb, seq, d = x.shape
    clen = ctx.shape[1]
    depth = ada_w.shape[0]
    fw = fourier_w.shape[1]
    sw = ssm_d.shape[1]
    ffn = ffn_conv_b.shape[1]
    alpha = (2.0 * depth) ** 0.25
    gd = fw // FOURIER_GROUPS
    chunk = clen
    assert seq % chunk == 0 and seq % GRID_W == 0
    ffn_pad = 256 if (ffn % 512 == 256) else 0

    ctx_row = nb
    n_rows = -(-(nb + 1) // SUBLANES) * SUBLANES
    cond = jnp.zeros((n_rows, d), F32).at[:nb].set(c).at[ctx_row].set(c_ctx)
    ride_mods = ffn_pad > 0
    first_mods = _adaln_call(cond, ada_w, ada_b, 0, 1 if ride_mods else depth)
    mods_list = [m.reshape(n_rows, 1, N_MOD * d) for m in first_mods]

    cos_c, sin_c = _dft_cos_sin(gd)
    cs_chan = _bf16_const(np.concatenate([cos_c, sin_c], axis=1))

    def pos_dft_matrix(n):
        cos_l, sin_l = _dft_cos_sin(n)
        return _bf16_const(np.concatenate([cos_l, -sin_l], axis=1))[None]

    csl_x = pos_dft_matrix(seq)
    csl_c = pos_dft_matrix(clen)

    def token_mix_tail(proj, csl, yf, yb, row_off, lw):
        ls = proj.shape[1]
        ab = _chan_dft_call(proj, cs_chan, fw)
        ab = ab.reshape(nb, 2 * ls, fw)
        if "fourier_w" not in lw:
            f, lw["fourier_w"] = _mm_call(csl, ab, BF16, "pos_dft", rider=lw.pop("ride_fourier_w"))
        else:
            f = _mm_call(csl, ab, BF16, "pos_dft")
        f = f.reshape(1, nb * ls, fw)
        if "glu_w" not in lw:
            y_f, lw["glu_w"] = _mm_call(f, lw["fourier_w"][None], BF16, "fourier_out",
                                        rider=lw.pop("ride_glu_w"))
        else:
            y_f = _mm_call(f, lw["fourier_w"][None], BF16, "fourier_out")
        y_f = y_f.reshape(nb, ls, d)
        act = _ssm_act_call(proj, yf, yb, row_off, lw["ssm_d"], fw, sw)
        if "w_out" not in lw:
            merged, lw["w_out"] = _glu_call(act, proj, lw["glu_w"], y_f, fw, sw, d,
                                            rider=lw.pop("ride_w_out"))
        else:
            merged = _glu_call(act, proj, lw["glu_w"], y_f, fw, sw, d)
        merged = merged.reshape(1, nb * ls, d)
        if "ride_next_w_in" in lw:
            tn_ride = 512
            nsteps = (nb * ls // _tile(nb * ls, 1024)) * (d // _tile(d, tn_ride))
            mod_job = _mod_job(ada_w, lw["layer"] + 1, nsteps)
            mod_rider = None if mod_job is None else (mod_job, (cond, ada_w, ada_b))
            res = _mm_call(merged, lw["w_out"][None], BF16, "w_out", tn_pref=tn_ride,
                           rider=lw.pop("ride_next_w_in"), mod_rider=mod_rider)
            out, nxt_w["w_in"] = res[0], res[1]
            if mod_rider is not None:
                nxt_w["mods"] = res[2]
        else:
            out = _mm_call(merged, lw["w_out"][None], BF16, "w_out")
        return out.reshape(nb, ls, d)

    def conv_ffn(h, lw, period):
        if "ride_next_w12" in lw:
            act, nxt_w["w12"] = _ffn1_call(h, lw["w12"], lw["conv_w"], lw["conv_b"], period,
                                           rider=lw.pop("ride_next_w12"))
        else:
            act = _ffn1_call(h, lw["w12"], lw["conv_w"], lw["conv_b"], period)
        return _mm_call(act, lw["w2"][None], BF16, "ffn2", tm_pref=512, tn_pref=512)

    xs = x
    cs = ctx.reshape(1, nb * clen, d)
    nxt_w = {}
    hx = _ln_mod_call(xs, mods_list[0], None, 0, 1)
    hc = _ln_mod_call(cs, mods_list[0], ctx_row, 0, 1)
    for i in range(depth):
        last = i == depth - 1
        mods = mods_list[i]
        lw = {
            "layer": i,
            "ssm_d": ssm_d[i].reshape(1, sw),
            "ride_fourier_w": (_rows_cast(fourier_w, i, _tile(fw, CAST_ROWS)), fourier_w),
            "ride_glu_w": (_rows_cast(glu_w, i, _tile(sw, CAST_ROWS // 2)), glu_w),
            "ride_w_out": (_rows_cast(w_out, i, _tile(d, CAST_ROWS)), w_out),
        }
        cur_w, nxt_w = nxt_w, {}
        ride_w12 = ride_w2 = None
        if ffn_pad:
            if "w12" in cur_w:
                lw["w12"] = cur_w["w12"]
            else:
                ride_w12 = (_pad_halves_cast(ffn_w12, i, ffn_pad), ffn_w12)
            ride_w2 = (_pad_rows_cast(ffn_w2, i, ffn_pad), ffn_w2)
            if not last:
                lw["ride_next_w_in"] = (_rows_cast(w_in, i + 1, _tile(d, CAST_ROWS // 4)), w_in)
                lw["ride_next_w12"] = (_pad_halves_cast(ffn_w12, i + 1, ffn_pad), ffn_w12)
        else:
            lw["w12"] = _cast_call(ffn_w12, i)
            lw["w2"] = _cast_call(ffn_w2, i)
        lw["conv_w"] = jnp.pad(ffn_conv_w[i], ((0, 0), (0, ffn_pad)))
        lw["conv_b"] = jnp.pad(ffn_conv_b[i].reshape(1, ffn), ((0, 0), (0, ffn_pad)))
        g1, b1 = ln1_g[i].reshape(1, d), ln1_b[i].reshape(1, d)
        g2, b2 = ln2_g[i].reshape(1, d), ln2_b[i].reshape(1, d)
        bmat, cmat, amat = _ssm_tables(ssm_a_re[i], ssm_a_im[i], ssm_log_dt[i], ssm_b_re[i],
                                       ssm_b_im[i], ssm_c_re[i], ssm_c_im[i])

        n_in = w_in.shape[2]
        hc2d = hc.reshape(nb * clen, d)
        c_col = 0 if last else fw
        if "w_in" in cur_w:
            w_in_b = cur_w["w_in"]
            proj_x = _mm_call(hx.reshape(1, nb * seq, d), w_in_b[None], BF16, "w_in", rider=ride_w12)
            w_in_c = w_in_b[None, :, fw:fw + sw] if last else w_in_b[None]
            proj_c = _mm_call(hc2d[None], w_in_c, BF16, "w_in_ctx")
        else:
            proj_x = _mm_w32_call(hx.reshape(nb * seq, d), w_in, i, 0, n_in, BF16, "w_in", ride_w12)
            if last:
                proj_c = _mm_w32_call(hc2d, w_in, i, fw, sw, BF16, "w_in_ctx")
            else:
                proj_c = _mm_w32_call(hc2d, w_in, i, 0, n_in, BF16, "w_in_ctx")
        if ride_w12 is not None:
            proj_x, lw["w12"] = proj_x
        proj_x = proj_x.reshape(nb, seq, n_in)
        proj_c = proj_c.reshape(nb, clen, -1)
        ssm_out = _ssm_call(proj_x, fw, proj_c, c_col, bmat, cmat, amat, chunk, ride_w2)
        yf, yb = ssm_out[0], ssm_out[1]
        if ride_w2 is not None:
            lw["w2"] = ssm_out[2]

        out_x = token_mix_tail(proj_x, csl_x, yf, yb, 0, lw)
        if not last and len(mods_list) == i + 1:
            if "mods" in nxt_w:
                mods_list.append(nxt_w["mods"].reshape(n_rows, 1, N_MOD * d))
            else:
                mods_list.append(_adaln_call(cond, ada_w, ada_b, i + 1, 1).reshape(n_rows, 1, N_MOD * d))
        nxt = None if last else (mods_list[i + 1], 0, 1)
        x1, h2 = _postnorm_call(xs, out_x, mods, None, 2, g1, b1, alpha, (mods, 3, 4))
        ffn_x = conv_ffn(h2, lw, GRID_W)
        xs, hx = _postnorm_call(x1, ffn_x, mods, None, 5, g2, b2, alpha, nxt)

        if not last:
            out_c = token_mix_tail(proj_c, csl_c, yf, yb, seq, lw).reshape(1, nb * clen, d)
            c1, hc2 = _postnorm_call(cs, out_c, mods, ctx_row, 2, g1, b1, alpha, (mods, 3, 4))
            ffn_c = conv_ffn(hc2, lw, clen)
            cs, hc = _postnorm_call(c1, ffn_c, mods, ctx_row, 5, g2, b2, alpha, nxt)
    return xs
```

```python
import functools
import math
from typing import Callable, NamedTuple

import jax
import jax.numpy as jnp
import ml_dtypes
import numpy as np
from jax import lax
from jax.experimental import pallas as pl
from jax.experimental.pallas import tpu as pltpu

GRID_W = 64
FOURIER_GROUPS = 4
N_MOD = 6
LN_EPS = 1e-6
SSM_BLOCK_GROUPS = 16
LANES = 128
SUBLANES = 8
VMEM_LIMIT = 56 * 1024 * 1024
GLU_SUBTILE = 256
CAST_ROWS = 256
SSM_MM_ROWS = 256

F32 = jnp.float32
BF16 = jnp.bfloat16


def _params(sem):
    return pltpu.CompilerParams(dimension_semantics=sem, vmem_limit_bytes=VMEM_LIMIT)


def _tile(n, pref):
    if n <= pref:
        return n
    while n % pref:
        pref //= 2
    assert pref >= SUBLANES, (n, pref)
    return pref


def _lane_tile(n, cap):
    for t in range(cap - cap % LANES, 0, -LANES):
        if n % t == 0:
            return t
    return n


def _dot(a, b):
    return jnp.dot(a, b, preferred_element_type=F32)


def _layer_norm(x):
    mu = jnp.mean(x, axis=-1, keepdims=True)
    xc = x - mu
    var = jnp.mean(xc * xc, axis=-1, keepdims=True)
    return xc * lax.rsqrt(var + LN_EPS)


def _mod_spec(mod_row, k, d):
    if mod_row is None:
        return pl.BlockSpec((None, 1, d), lambda b, *_: (b, 0, k))
    return pl.BlockSpec((None, 1, d), lambda b, *_: (mod_row, 0, k))


def _cast_kernel(w_ref, o_ref):
    o_ref[...] = w_ref[...].astype(o_ref.dtype)


def _cast_call(w, layer):
    _, r, c = w.shape
    tr = _tile(r, 512)
    tc = _lane_tile(c, 6144)
    return pl.pallas_call(
        _cast_kernel,
        grid=(r // tr, c // tc),
        in_specs=[pl.BlockSpec((None, tr, tc), lambda i, j: (layer, i, j))],
        out_specs=pl.BlockSpec((tr, tc), lambda i, j: (i, j)),
        out_shape=jax.ShapeDtypeStruct((r, c), BF16),
        compiler_params=_params(("arbitrary", "arbitrary")),
        name="cast_w",
    )(w)


class _PadCast(NamedTuple):
    layer: int
    in_block: tuple
    in_index: Callable
    out_block: tuple
    out_index: Callable
    out_shape: tuple
    nblk: int
    is_data: Callable


def _rows_cast(w, layer, tr):
    _, r, c = w.shape
    assert r % tr == 0
    return _PadCast(layer, (None, tr, c), lambda s: (s, 0), (tr, c), lambda s: (s, 0), (r, c),
                    r // tr, lambda s: s >= 0)


def _pad_rows_cast(w, layer, tr):
    _, r, c = w.shape
    nvalid = r // tr
    assert r % tr == 0
    return _PadCast(layer, (None, tr, c), lambda s: (jnp.minimum(s, nvalid - 1), 0),
                    (tr, c), lambda s: (s, 0), (r + tr, c), nvalid + 1, lambda s: s < nvalid)


def _pad_halves_cast(w, layer, tc):
    _, r, c2 = w.shape
    f = c2 // 2
    nvalid = f // tc
    assert f % tc == 0
    per = nvalid + 1
    return _PadCast(layer, (None, r, tc),
                    lambda s: (0, (s // per) * nvalid + jnp.minimum(s % per, nvalid - 1)),
                    (r, tc), lambda s: (0, s), (r, 2 * (f + tc)), 2 * per, lambda s: s % per < nvalid)


def _cast_specs(job, step_of):
    def clamped(*g):
        return jnp.minimum(step_of(*g), job.nblk - 1)

    in_spec = pl.BlockSpec(job.in_block, lambda *g: (job.layer,) + tuple(job.in_index(clamped(*g))))
    out_spec = pl.BlockSpec(job.out_block, lambda *g: tuple(job.out_index(clamped(*g))))
    return in_spec, out_spec


def _when_live(job, step, nsteps, work):
    if job.nblk == nsteps:
        work()
    else:
        pl.when(step < job.nblk)(work)


def _cast_step(job, step, w_ref, o_ref, nsteps=None):
    def work():
        o_ref[...] = jnp.where(job.is_data(step), w_ref[...], 0.0).astype(o_ref.dtype)

    _when_live(job, step, nsteps, work)


def _rider_fits(rider, nsteps):
    return rider[0].nblk <= nsteps


def _attach_rider(rider, step_of, in_specs, out_specs, out_shape, args):
    job, rw = rider
    r_in, r_out = _cast_specs(job, step_of)
    in_specs.append(r_in)
    out_specs.append(r_out)
    out_shape.append(jax.ShapeDtypeStruct(job.out_shape, BF16))
    args.append(rw)
    return job


def _pad_cast_kernel(w_ref, o_ref, *, job):
    _cast_step(job, pl.program_id(0), w_ref, o_ref)


def _pad_cast_call(job, w):
    in_spec, out_spec = _cast_specs(job, lambda s: s)
    return pl.pallas_call(
        functools.partial(_pad_cast_kernel, job=job),
        grid=(job.nblk,),
        in_specs=[in_spec],
        out_specs=out_spec,
        out_shape=jax.ShapeDtypeStruct(job.out_shape, BF16),
        compiler_params=_params(("arbitrary",)),
        name="cast_w_pad",
    )(w)


def _adaln_kernel(c_ref, w_ref, b_ref, o_ref):
    c = c_ref[...]
    s = (c * jax.nn.sigmoid(c)).astype(BF16)
    o_ref[...] = _dot(s, w_ref[...].astype(BF16)) + b_ref[...]


class _ModJob(NamedTuple):
    layer: int
    tcol: int
    nblk: int


def _mod_job(ada_w, layer, nsteps):
    n = ada_w.shape[2]
    for tcol in range(LANES, n + 1, LANES):
        if n % tcol == 0 and n // tcol <= nsteps:
            return _ModJob(layer, tcol, n // tcol)
    return None


def _mod_attach(job, operands, step_of, in_specs, out_specs, out_shape, args):
    cond, ada_w, ada_b = operands
    r, d = cond.shape

    def clamped(*g):
        return jnp.minimum(step_of(*g), job.nblk - 1)

    in_specs += [
        pl.BlockSpec((r, d), lambda *g: (0, 0)),
        pl.BlockSpec((None, d, job.tcol), lambda *g: (job.layer, 0, clamped(*g))),
        pl.BlockSpec((None, 1, job.tcol), lambda *g: (job.layer, 0, clamped(*g))),
    ]
    out_specs.append(pl.BlockSpec((r, job.tcol), lambda *g: (0, clamped(*g))))
    out_shape.append(jax.ShapeDtypeStruct((r, ada_w.shape[2]), F32))
    args += [cond, ada_w, ada_b.reshape(ada_b.shape[0], 1, -1)]


def _mod_step(job, step, c_ref, w_ref, b_ref, o_ref, nsteps=None):
    _when_live(job, step, nsteps, functools.partial(_adaln_kernel, c_ref, w_ref, b_ref, o_ref))


def _adaln_call(cond, ada_w, ada_b, layer_lo=0, nlayer=None):
    all_layers, d, n = ada_w.shape
    depth = all_layers - layer_lo if nlayer is None else nlayer
    r = cond.shape[0]
    tn = _tile(n, 512)
    return pl.pallas_call(
        _adaln_kernel,
        grid=(depth, n // tn),
        in_specs=[
            pl.BlockSpec((r, d), lambda l, j: (0, 0)),
            pl.BlockSpec((None, d, tn), lambda l, j: (layer_lo + l, 0, j)),
            pl.BlockSpec((None, 1, tn), lambda l, j: (layer_lo + l, 0, j)),
        ],
        out_specs=pl.BlockSpec((None, r, tn), lambda l, j: (l, 0, j)),
        out_shape=jax.ShapeDtypeStruct((depth, r, n), F32),
        compiler_params=_params(("arbitrary", "arbitrary")),
        name="adaln",
    )(cond, ada_w, ada_b.reshape(all_layers, 1, n))


def _ln_mod_kernel(x_ref, sh_ref, sc_ref, o_ref):
    y = _layer_norm(x_ref[...])
    o_ref[...] = (y * (1.0 + sc_ref[...]) + sh_ref[...]).astype(o_ref.dtype)


def _ln_mod_call(x, mods, mod_row, k_shift, k_scale):
    bm, lm, d = x.shape
    tm = _tile(lm, 256)
    return pl.pallas_call(
        _ln_mod_kernel,
        grid=(bm, lm // tm),
        in_specs=[
            pl.BlockSpec((None, tm, d), lambda b, i: (b, i, 0)),
            _mod_spec(mod_row, k_shift, d),
            _mod_spec(mod_row, k_scale, d),
        ],
        out_specs=pl.BlockSpec((None, tm, d), lambda b, i: (b, i, 0)),
        out_shape=jax.ShapeDtypeStruct((bm, lm, d), BF16),
        compiler_params=_params(("arbitrary", "arbitrary")),
        name="ln_mod",
    )(x, mods, mods)


def _postnorm_kernel(res_ref, pre_ref, gate_ref, g_ref, b_ref, *rest, alpha, with_next):
    z = alpha * res_ref[...] + gate_ref[...] * pre_ref[...].astype(F32)
    x_new = _layer_norm(z) * g_ref[...] + b_ref[...]
    if with_next:
        sh_ref, sc_ref, o_ref, h_ref = rest
        o_ref[...] = x_new
        h_ref[...] = (_layer_norm(x_new) * (1.0 + sc_ref[...]) + sh_ref[...]).astype(h_ref.dtype)
    else:
        (o_ref,) = rest
        o_ref[...] = x_new


def _postnorm_call(res, pre, mods, mod_row, k_gate, ln_g, ln_b, alpha, next_mod=None):
    bm, lm, d = res.shape
    tm = _tile(lm, 256)
    row = pl.BlockSpec((None, tm, d), lambda b, i: (b, i, 0))
    vec = pl.BlockSpec((1, d), lambda b, i: (0, 0))
    in_specs = [row, row, _mod_spec(mod_row, k_gate, d), vec, vec]
    args = [res, pre, mods, ln_g, ln_b]
    out_specs = [row]
    out_shape = [jax.ShapeDtypeStruct((bm, lm, d), F32)]
    if next_mod is not None:
        nmods, k_shift, k_scale = next_mod
        in_specs += [_mod_spec(mod_row, k_shift, d), _mod_spec(mod_row, k_scale, d)]
        args += [nmods, nmods]
        out_specs.append(row)
        out_shape.append(jax.ShapeDtypeStruct((bm, lm, d), BF16))
    out = pl.pallas_call(
        functools.partial(_postnorm_kernel, alpha=alpha, with_next=next_mod is not None),
        grid=(bm, lm // tm),
        in_specs=in_specs,
        out_specs=out_specs,
        out_shape=out_shape,
        compiler_params=_params(("arbitrary", "arbitrary")),
        name="postnorm",
    )(*args)
    return (out[0], out[1]) if next_mod is not None else (out[0], None)


def _mm_kernel(a_ref, w_ref, *rest, nb, ni, nj, job, mod_job):
    n_side_in = (job is not None) + 3 * (mod_job is not None)
    side_in, (o_ref, *side_out) = rest[:n_side_in], rest[n_side_in:]
    step = (pl.program_id(0) * ni + pl.program_id(1)) * nj + pl.program_id(2)
    nsteps = nb * ni * nj
    if job is not None:
        _cast_step(job, step, side_in[0], side_out[0], nsteps)
    if mod_job is not None:
        _mod_step(mod_job, step, *side_in[-3:], side_out[-1], nsteps)
    o_ref[...] = _dot(a_ref[...], w_ref[...]).astype(o_ref.dtype)


def _mm_call(a, w, out_dtype, name, tm_pref=1024, tn_pref=1024, rider=None, mod_rider=None):
    ba, m, k = a.shape
    bw, _, n = w.shape
    nb = max(ba, bw)
    tm = _tile(m, tm_pref)
    tn = _tile(n, tn_pref)
    ni, nj = m // tm, n // tn
    if rider is not None and not _rider_fits(rider, nb * ni * nj):
        out = _mm_call(a, w, out_dtype, name, tm_pref, tn_pref, mod_rider=mod_rider)
        out = out if mod_rider is not None else (out,)
        return (out[0], _pad_cast_call(*rider)) + tuple(out[1:])
    a_map = (lambda b, i, j: (b, i, 0)) if ba > 1 else (lambda b, i, j: (0, i, 0))
    w_map = (lambda b, i, j: (b, 0, j)) if bw > 1 else (lambda b, i, j: (0, 0, j))
    in_specs = [pl.BlockSpec((None, tm, k), a_map), pl.BlockSpec((None, k, tn), w_map)]
    out_specs = [pl.BlockSpec((None, tm, tn), lambda b, i, j: (b, i, j))]
    out_shape = [jax.ShapeDtypeStruct((nb, m, n), out_dtype)]
    args = [a, w]
    job = mod_job = None

    def step_of(b, i, j):
        return (b * ni + i) * nj + j

    if rider is not None:
        job = _attach_rider(rider, step_of, in_specs, out_specs, out_shape, args)
    if mod_rider is not None:
        mod_job, operands = mod_rider
        assert mod_job.nblk <= nb * ni * nj
        _mod_attach(mod_job, operands, step_of, in_specs, out_specs, out_shape, args)
    out = pl.pallas_call(
        functools.partial(_mm_kernel, nb=nb, ni=ni, nj=nj, job=job, mod_job=mod_job),
        grid=(nb, ni, nj),
        in_specs=in_specs,
        out_specs=out_specs,
        out_shape=out_shape,
        compiler_params=_params(("arbitrary", "arbitrary", "arbitrary")),
        name=name,
    )(*args)
    return out if len(out) > 1 else out[0]


def _mm_w32_kernel(a_ref, w_ref, *rest, ni, job):
    if job is None:
        o_ref, w_scr = rest
    else:
        rw_ref, o_ref, ro_ref, w_scr = rest
    i = pl.program_id(1)

    @pl.when(i == 0)
    def _():
        w_scr[...] = w_ref[...].astype(w_scr.dtype)

    if job is not None:
        _cast_step(job, pl.program_id(0) * ni + i, rw_ref, ro_ref)
    o_ref[...] = _dot(a_ref[...], w_scr[...]).astype(o_ref.dtype)


def _mm_w32_call(a, w, layer, col0, ncols, out_dtype, name, rider=None):
    m, k = a.shape
    tm = _tile(m, 1024)
    tn = _tile(ncols, 512)
    assert col0 % tn == 0
    j0 = col0 // tn
    ni = m // tm
    in_specs = [
        pl.BlockSpec((tm, k), lambda j, i: (i, 0)),
        pl.BlockSpec((None, k, tn), lambda j, i: (layer, 0, j0 + j)),
    ]
    out_specs = [pl.BlockSpec((tm, tn), lambda j, i: (i, j))]
    out_shape = [jax.ShapeDtypeStruct((m, ncols), out_dtype)]
    args = [a, w]
    job = None
    if rider is not None and not _rider_fits(rider, (ncols // tn) * ni):
        return _mm_w32_call(a, w, layer, col0, ncols, out_dtype, name), _pad_cast_call(*rider)
    if rider is not None:
        job = _attach_rider(rider, lambda j, i: j * ni + i, in_specs, out_specs, out_shape, args)
    out = pl.pallas_call(
        functools.partial(_mm_w32_kernel, ni=ni, job=job),
        grid=(ncols // tn, ni),
        in_specs=in_specs,
        out_specs=out_specs,
        out_shape=out_shape,
        scratch_shapes=[pltpu.VMEM((k, tn), BF16)],
        compiler_params=_params(("arbitrary", "arbitrary")),
        name=name,
    )(*args)
    return out if rider is not None else out[0]


def _chan_dft_kernel(u_ref, cs_ref, o_ref, *, gd):
    r = _dot(u_ref[...], cs_ref[...])
    o_ref[0] = r[:, :gd].astype(o_ref.dtype)
    o_ref[1] = r[:, gd:].astype(o_ref.dtype)


def _chan_dft_call(proj, cs, fw):
    bs, ls, _ = proj.shape
    gd = fw // FOURIER_GROUPS
    tm = _tile(ls, 1024)
    return pl.pallas_call(
        functools.partial(_chan_dft_kernel, gd=gd),
        grid=(bs, ls // tm, FOURIER_GROUPS),
        in_specs=[
            pl.BlockSpec((None, tm, gd), lambda b, i, g: (b, i, g)),
            pl.BlockSpec((gd, 2 * gd), lambda b, i, g: (0, 0)),
        ],
        out_specs=pl.BlockSpec((None, 2, tm, gd), lambda b, i, g: (b, 0, i, g)),
        out_shape=jax.ShapeDtypeStruct((bs, 2, ls, fw), BF16),
        compiler_params=_params(("arbitrary", "arbitrary", "arbitrary")),
        name="chan_dft",
    )(proj, cs)


def _ssm_kernel(uxf_ref, uxb_ref, ucf_ref, ucb_ref, bm_ref, cm_ref, a_ref, *rest,
                chunk, nbat, half, n_ctx, n_all, job):
    side_work = None
    if job is None:
        yf_ref, yb_ref, *scratch = rest
    else:
        rw_ref, yf_ref, yb_ref, ro_ref, *scratch = rest
        step = pl.program_id(0) * n_all + pl.program_id(1)
        side_work = functools.partial(_cast_step, job, step, rw_ref, ro_ref)
    _ssm_step(uxf_ref, uxb_ref, ucf_ref, ucb_ref, bm_ref, cm_ref, a_ref, yf_ref, yb_ref,
              *scratch, chunk=chunk, nbat=nbat, half=half, n_ctx=n_ctx, side_work=side_work)


def _ssm_step(uxf_ref, uxb_ref, ucf_ref, ucb_ref, bm_ref, cm_ref, a_ref, yf_ref, yb_ref,
              u_scr, uin_scr, s_scr, y_scr, h_scr, *, chunk, nbat, half, n_ctx, side_work):
    k = pl.program_id(0)
    c = pl.program_id(1)
    rows = SUBLANES
    hw = half // 2
    mm_rows = min(chunk * rows, SSM_MM_ROWS)
    fwd_starts = list(range(0, chunk * rows, mm_rows))
    row_starts = (fwd_starts, fwd_starts[::-1])

    @pl.when(jnp.logical_and(k == 0, c == 0))
    def _():
        uin_scr[...] = jnp.zeros_like(uin_scr)

    @pl.when(c == 0)
    def _():
        h_scr[...] = jnp.zeros_like(h_scr)

    @pl.when(c < n_ctx)
    def _():
        u_scr[0] = ucf_ref[...]
        u_scr[1] = ucb_ref[...]

    @pl.when(c >= n_ctx)
    def _():
        u_scr[0] = uxf_ref[...]
        u_scr[1] = uxb_ref[...]

    if side_work is not None:
        side_work()
    for d in range(2):
        for b in range(nbat):
            ub = u_scr[d, b].astype(F32)
            for h in range(2):
                uin_scr[d, h, pl.ds(2 * b + h, chunk, stride=rows), :] = ub[:, h * LANES:(h + 1) * LANES]
        bm = bm_ref[d]
        for r0 in row_starts[d]:
            rs = slice(r0, r0 + mm_rows)
            lhs = jnp.concatenate([uin_scr[d, 0, rs], uin_scr[d, 1, rs]], axis=1).astype(BF16)
            s_scr[d, rs] = _dot(lhs, bm)

    arf, aif = a_ref[0, 0], a_ref[0, 1]
    arb, aib = a_ref[1, 0], a_ref[1, 1]

    def body(t, carry):
        hrf, hif, hrb, hib = carry
        rowf = t * rows
        rowb = (chunk - 1 - t) * rows
        xf = s_scr[0, pl.ds(rowf, rows), :]
        xb = s_scr[1, pl.ds(rowb, rows), :]
        nrf = arf * hrf - aif * hif + xf[:, :hw]
        nif = arf * hif + aif * hrf + xf[:, hw:]
        nrb = arb * hrb - aib * hib + xb[:, :hw]
        nib = arb * hib + aib * hrb + xb[:, hw:]
        s_scr[0, pl.ds(rowf, rows), :hw] = nrf
        s_scr[0, pl.ds(rowf, rows), hw:] = nif
        s_scr[1, pl.ds(rowb, rows), :hw] = nrb
        s_scr[1, pl.ds(rowb, rows), hw:] = nib
        return nrf, nif, nrb, nib

    init = (h_scr[0, :, :hw], h_scr[0, :, hw:], h_scr[1, :, :hw], h_scr[1, :, hw:])
    carry = init
    for t in range(chunk):
        carry = body(t, carry)
    hrf, hif, hrb, hib = carry
    h_scr[0, :, :hw] = hrf
    h_scr[0, :, hw:] = hif
    h_scr[1, :, :hw] = hrb
    h_scr[1, :, hw:] = hib

    for d, y_ref in enumerate((yf_ref, yb_ref)):
        cm = cm_ref[d]
        for r0 in row_starts[d]:
            rs = slice(r0, r0 + mm_rows)
            out = _dot(s_scr[d, rs].astype(BF16), cm)
            y_scr[d, 0, rs] = out[:, :LANES]
            y_scr[d, 1, rs] = out[:, LANES:]
        for b in range(nbat):
            for h in range(2):
                piece = y_scr[d, h, pl.ds(2 * b + h, chunk, stride=rows), :]
                y_ref[b, :, h * LANES:(h + 1) * LANES] = piece.astype(y_ref.dtype)


def _ssm_call(ux, ux_col, uc, uc_col, bmat, cmat, amat, chunk, rider=None):
    nbat, seq, _ = ux.shape
    clen = uc.shape[1]
    assert 2 * nbat == SUBLANES, "the scan packs 2 column halves x batch on the 8 sublanes"
    ndir, nblk, cin, cst = bmat.shape
    assert cin == 2 * LANES, "each column half of a group block is one 128-lane input slab"
    sw = nblk * cin
    n_lat, n_ctx = seq // chunk, clen // chunk
    n_all = n_lat + n_ctx
    assert ux_col % cin == 0 and uc_col % cin == 0
    xo, co = ux_col // cin, uc_col // cin

    def lat_f(c):
        return jnp.maximum(c - n_ctx, 0)

    def lat_b(c):
        return n_lat - 1 - jnp.maximum(c - n_ctx, 0)

    def ctx_f(c):
        return jnp.minimum(c, n_ctx - 1)

    def ctx_b(c):
        return jnp.maximum(n_ctx - 1 - c, 0)

    def out_f(c):
        return jnp.where(c < n_ctx, n_lat + c, c - n_ctx)

    def out_b(c):
        return jnp.where(c < n_ctx, n_lat + n_ctx - 1 - c, n_lat - 1 - (c - n_ctx))

    ublk = (nbat, chunk, cin)
    in_specs = [
        pl.BlockSpec(ublk, lambda k, c: (0, lat_f(c), xo + k)),
        pl.BlockSpec(ublk, lambda k, c: (0, lat_b(c), xo + k)),
        pl.BlockSpec(ublk, lambda k, c: (0, ctx_f(c), co + k)),
        pl.BlockSpec(ublk, lambda k, c: (0, ctx_b(c), co + k)),
        pl.BlockSpec((ndir, None, cin, cst), lambda k, c: (0, k, 0, 0)),
        pl.BlockSpec((ndir, None, cst, cin), lambda k, c: (0, k, 0, 0)),
        pl.BlockSpec((ndir, None, 2, SUBLANES, cst // 2), lambda k, c: (0, k, 0, 0, 0)),
    ]
    out_specs = [
        pl.BlockSpec(ublk, lambda k, c: (0, out_f(c), k)),
        pl.BlockSpec(ublk, lambda k, c: (0, out_b(c), k)),
    ]
    out_shape = [jax.ShapeDtypeStruct((nbat, seq + clen, sw), BF16)] * 2
    args = [ux, ux, uc, uc, bmat, cmat, amat]
    job = None
    if rider is not None and not _rider_fits(rider, nblk * n_all):
        yf, yb = _ssm_call(ux, ux_col, uc, uc_col, bmat, cmat, amat, chunk)
        return yf, yb, _pad_cast_call(*rider)
    if rider is not None:
        job = _attach_rider(rider, lambda k, c: k * n_all + c, in_specs, out_specs, out_shape, args)
    return pl.pallas_call(
        functools.partial(_ssm_kernel, chunk=chunk, nbat=nbat, half=cst, n_ctx=n_ctx,
                          n_all=n_all, job=job),
        grid=(nblk, n_all),
        in_specs=in_specs,
        out_specs=out_specs,
        out_shape=out_shape,
        scratch_shapes=[
            pltpu.VMEM((ndir, nbat, chunk, cin), BF16),
            pltpu.VMEM((ndir, 2, SUBLANES * chunk, LANES), F32),
            pltpu.VMEM((ndir, SUBLANES * chunk, cst), F32),
            pltpu.VMEM((ndir, 2, SUBLANES * chunk, LANES), F32),
            pltpu.VMEM((ndir, SUBLANES, cst), F32),
        ],
        compiler_params=_params(("arbitrary", "arbitrary")),
        name="ssm",
    )(*args)


def _ssm_tables(a_re, a_im, log_dt, b_re, b_im, c_re, c_im):
    ndir, g, p = a_re.shape
    cg = b_re.shape[-1]
    gb = SSM_BLOCK_GROUPS
    nblk = g // gb
    hg = gb // 2
    npair = hg // 2
    assert 2 * p == LANES and g % gb == 0 and hg * cg == LANES
    lr, li = a_re.astype(F32), a_im.astype(F32)
    dt = jnp.exp(log_dt.astype(F32))[..., None]
    zr, zi = lr * dt, li * dt
    a_bar_r = jnp.exp(zr) * jnp.cos(zi)
    a_bar_i = jnp.exp(zr) * jnp.sin(zi)
    em1_r = jnp.expm1(zr) * jnp.cos(zi) - 2.0 * jnp.square(jnp.sin(0.5 * zi))
    den = lr * lr + li * li
    cf_r = ((em1_r * lr + a_bar_i * li) / den)[..., None]
    cf_i = ((a_bar_i * lr - em1_r * li) / den)[..., None]
    br, bi = b_re.astype(F32), b_im.astype(F32)
    b_bar_r = cf_r * br - cf_i * bi
    b_bar_i = cf_r * bi + cf_i * br

    lane_q = np.arange(LANES) // p
    own = (np.arange(gb)[:, None, None] % hg
           == 2 * np.arange(npair)[None, :, None] + lane_q[None, None, :])
    own_in = np.repeat(own, cg, axis=0).astype(np.float32)
    own_in = own_in.reshape(gb * cg, npair * LANES)
    own_out = np.ascontiguousarray(own_in.T)

    def in_mat(bb):
        t = jnp.swapaxes(bb.reshape(ndir, nblk, gb, p, cg), -1, -2).reshape(ndir, nblk, gb * cg, p)
        return jnp.tile(t, (1, 1, 1, 2 * npair)) * own_in

    def out_mat(cc):
        t = jnp.swapaxes(cc.reshape(ndir, nblk, gb * cg, p), -1, -2)
        return jnp.tile(t, (1, 1, 2 * npair, 1)) * own_out

    bmat = jnp.concatenate([in_mat(b_bar_r), in_mat(b_bar_i)], axis=-1).astype(BF16)
    cmat = jnp.concatenate([out_mat(c_re.astype(F32)), out_mat(-c_im.astype(F32))], axis=-2).astype(BF16)

    def decay(a):
        t = a.reshape(ndir, nblk, 2, npair * LANES)
        return jnp.tile(t, (1, 1, SUBLANES // 2, 1))
    aa = jnp.stack([decay(a_bar_r), decay(a_bar_i)], axis=2)
    return bmat, cmat, aa


def _ssm_act_kernel(u_ref, yf_ref, yb_ref, d_ref, o_ref):
    y = d_ref[...] * u_ref[...].astype(F32) + yf_ref[...].astype(F32) + yb_ref[...].astype(F32)
    o_ref[...] = jax.nn.gelu(y).astype(o_ref.dtype)


def _ssm_act_call(proj, yf, yb, row_off, ssm_d, fw, sw):
    bs, ls, _ = proj.shape
    tm = _tile(ls, 512)
    assert fw % sw == 0 and row_off % tm == 0
    u_blk, r_blk = fw // sw, row_off // tm
    return pl.pallas_call(
        _ssm_act_kernel,
        grid=(bs, ls // tm),
        in_specs=[
            pl.BlockSpec((None, tm, sw), lambda b, i: (b, i, u_blk)),
            pl.BlockSpec((None, tm, sw), lambda b, i: (b, i + r_blk, 0)),
            pl.BlockSpec((None, tm, sw), lambda b, i: (b, i + r_blk, 0)),
            pl.BlockSpec((1, sw), lambda b, i: (0, 0)),
        ],
        out_specs=pl.BlockSpec((None, tm, sw), lambda b, i: (b, i, 0)),
        out_shape=jax.ShapeDtypeStruct((bs, ls, sw), BF16),
        compiler_params=_params(("arbitrary", "arbitrary")),
        name="ssm_act",
    )(proj, yf, yb, ssm_d)


def _glu_kernel(a_ref, wv_ref, wg_ref, gf_ref, gs_ref, fo_ref, *rest, nb, ni, job):
    if job is None:
        (o_ref,) = rest
    else:
        rw_ref, o_ref, ro_ref = rest
        step = (pl.program_id(0) * nb + pl.program_id(1)) * ni + pl.program_id(2)
        _cast_step(job, step, rw_ref, ro_ref)
    a = a_ref[...]
    tn = o_ref.shape[1]
    sub = min(tn, GLU_SUBTILE)
    for n0 in range(0, tn, sub):
        cols = slice(n0, n0 + sub)
        y_s = (_dot(a, wv_ref[:, cols]) * jax.nn.sigmoid(_dot(a, wg_ref[:, cols]))).astype(BF16)
        o_ref[:, cols] = (jax.nn.sigmoid(gf_ref[:, cols]) * fo_ref[:, cols]
                          + jax.nn.sigmoid(gs_ref[:, cols]) * y_s)


def _glu_call(act, proj, glu_w, y_f, fw, sw, d, rider=None):
    bs, ls, _ = act.shape
    tm = _tile(ls, 512)
    tn = _tile(d, 1024)
    assert (fw + sw) % tn == 0
    gf_blk = (fw + sw) // tn
    gs_blk = (fw + sw + d) // tn
    ni, nj = ls // tm, d // tn
    if rider is not None and not _rider_fits(rider, bs * ni * nj):
        return _glu_call(act, proj, glu_w, y_f, fw, sw, d), _pad_cast_call(*rider)
    in_specs = [
        pl.BlockSpec((None, tm, sw), lambda j, b, i: (b, i, 0)),
        pl.BlockSpec((sw, tn), lambda j, b, i: (0, j)),
        pl.BlockSpec((sw, tn), lambda j, b, i: (0, nj + j)),
        pl.BlockSpec((None, tm, tn), lambda j, b, i: (b, i, gf_blk + j)),
        pl.BlockSpec((None, tm, tn), lambda j, b, i: (b, i, gs_blk + j)),
        pl.BlockSpec((None, tm, tn), lambda j, b, i: (b, i, j)),
    ]
    out_specs = [pl.BlockSpec((None, tm, tn), lambda j, b, i: (b, i, j))]
    out_shape = [jax.ShapeDtypeStruct((bs, ls, d), BF16)]
    args = [act, glu_w, glu_w, proj, proj, y_f]
    job = None
    if rider is not None:
        job = _attach_rider(rider, lambda j, b, i: (j * bs + b) * ni + i,
                            in_specs, out_specs, out_shape, args)
    out = pl.pallas_call(
        functools.partial(_glu_kernel, nb=bs, ni=ni, job=job),
        grid=(nj, bs, ni),
        in_specs=in_specs,
        out_specs=out_specs,
        out_shape=out_shape,
        compiler_params=_params(("arbitrary", "arbitrary", "arbitrary")),
        name="glu_merge",
    )(*args)
    return out if rider is not None else out[0]


def _ffn1_kernel(h_ref, wu_ref, wv_ref, cw_ref, cb_ref, *rest, period, ni, nj, job):
    if job is None:
        (o_ref,) = rest
    else:
        rw_ref, o_ref, ro_ref = rest
        step = (pl.program_id(0) * ni + pl.program_id(1)) * nj + pl.program_id(2)
        _cast_step(job, step, rw_ref, ro_ref)
    h = h_ref[...]
    u = _dot(h, wu_ref[...])
    v = _dot(h, wv_ref[...])
    tm = u.shape[0]
    pos = lax.broadcasted_iota(jnp.int32, u.shape, 0) % period
    prev = jnp.where(pos == 0, 0.0, pltpu.roll(u, 1, axis=0))
    nxt = jnp.where(pos == period - 1, 0.0, pltpu.roll(u, tm - 1, axis=0))
    cw = cw_ref[...]
    conv = cb_ref[...] + cw[0:1] * prev + cw[1:2] * u + cw[2:3] * nxt
    o_ref[...] = (jax.nn.gelu(conv) * v).astype(o_ref.dtype)


def _ffn1_call(h, w12, conv_w, conv_b, period, rider=None):
    bm, lm, d = h.shape
    ffn = w12.shape[1] // 2
    tm = _tile(lm, 1024)
    tn = _tile(ffn, 512)
    ni, nj = lm // tm, ffn // tn
    assert tm % period == 0
    if rider is not None and not _rider_fits(rider, bm * ni * nj):
        return _ffn1_call(h, w12, conv_w, conv_b, period), _pad_cast_call(*rider)
    in_specs = [
        pl.BlockSpec((None, tm, d), lambda b, i, j: (b, i, 0)),
        pl.BlockSpec((d, tn), lambda b, i, j: (0, j)),
        pl.BlockSpec((d, tn), lambda b, i, j: (0, nj + j)),
        pl.BlockSpec((3, tn), lambda b, i, j: (0, j)),
        pl.BlockSpec((1, tn), lambda b, i, j: (0, j)),
    ]
    out_specs = [pl.BlockSpec((None, tm, tn), lambda b, i, j: (b, i, j))]
    out_shape = [jax.ShapeDtypeStruct((bm, lm, ffn), BF16)]
    args = [h, w12, w12, conv_w, conv_b]
    job = None
    if rider is not None:
        job = _attach_rider(rider, lambda b, i, j: (b * ni + i) * nj + j,
                            in_specs, out_specs, out_shape, args)
    out = pl.pallas_call(
        functools.partial(_ffn1_kernel, period=period, ni=ni, nj=nj, job=job),
        grid=(bm, ni, nj),
        in_specs=in_specs,
        out_specs=out_specs,
        out_shape=out_shape,
        compiler_params=_params(("arbitrary", "arbitrary", "arbitrary")),
        name="ffn1",
    )(*args)
    return out if rider is not None else out[0]


def _dft_cos_sin(n):
    idx = np.arange(n, dtype=np.int64)
    ang = (2.0 * math.pi / n) * ((idx[:, None] * idx[None, :]) % n)
    scale = 1.0 / math.sqrt(n)
    return np.cos(ang) * scale, np.sin(ang) * scale


def _bf16_const(a):
    return jnp.asarray(np.asarray(a, np.float32).astype(ml_dtypes.bfloat16))


def kernel(x, c, ctx, c_ctx, ada_w, ada_b, w_in, fourier_w, ssm_a_re, ssm_a_im, ssm_log_dt,
           ssm_b_re, ssm_b_im, ssm_c_re, ssm_c_im, ssm_d, glu_w, w_out, ln1_g, ln1_b,
           ffn_w12, ffn_conv_w, ffn_conv_b, ffn_w2, ln2_g, ln2_b):
    nb, seq, d = x.shape
    clen = ctx.shape[1]
    depth = ada_w.shape[0]
    fw = fourier_w.shape[1]
    sw = ssm_d.shape[1]
    ffn = ffn_conv_b.shape[1]
    alpha = (2.0 * depth) ** 0.25
    gd = fw // FOURIER_GROUPS
    chunk = clen
    assert seq % chunk == 0 and seq % GRID_W == 0
    ffn_pad = 256 if (ffn % 512 == 256) else 0

    ctx_row = nb
    n_rows = -(-(nb + 1) // SUBLANES) * SUBLANES
    cond = jnp.zeros((n_rows, d), F32).at[:nb].set(c).at[ctx_row].set(c_ctx)
    ride_mods = ffn_pad > 0
    first_mods = _adaln_call(cond, ada_w, ada_b, 0, 1 if ride_mods else depth)
    mods_list = [m.reshape(n_rows, 1, N_MOD * d) for m in first_mods]

    cos_c, sin_c = _dft_cos_sin(gd)
    cs_chan = _bf16_const(np.concatenate([cos_c, sin_c], axis=1))

    def pos_dft_matrix(n):
        cos_l, sin_l = _dft_cos_sin(n)
        return _bf16_const(np.concatenate([cos_l, -sin_l], axis=1))[None]

    csl_x = pos_dft_matrix(seq)
    csl_c = pos_dft_matrix(clen)

    def token_mix_tail(proj, csl, yf, yb, row_off, lw):
        ls = proj.shape[1]
        ab = _chan_dft_call(proj, cs_chan, fw)
        ab = ab.reshape(nb, 2 * ls, fw)
        if "fourier_w" not in lw:
            f, lw["fourier_w"] = _mm_call(csl, ab, BF16, "pos_dft", rider=lw.pop("ride_fourier_w"))
        else:
            f = _mm_call(csl, ab, BF16, "pos_dft")
        f = f.reshape(1, nb * ls, fw)
        if "glu_w" not in lw:
            y_f, lw["glu_w"] = _mm_call(f, lw["fourier_w"][None], BF16, "fourier_out",
                                        rider=lw.pop("ride_glu_w"))
        else:
            y_f = _mm_call(f, lw["fourier_w"][None], BF16, "fourier_out")
        y_f = y_f.reshape(nb, ls, d)
        act = _ssm_act_call(proj, yf, yb, row_off, lw["ssm_d"], fw, sw)
        if "w_out" not in lw:
            merged, lw["w_out"] = _glu_call(act, proj, lw["glu_w"], y_f, fw, sw, d,
                                            rider=lw.pop("ride_w_out"))
        else:
            merged = _glu_call(act, proj, lw["glu_w"], y_f, fw, sw, d)
        merged = merged.reshape(1, nb * ls, d)
        if "ride_next_w_in" in lw:
            tn_ride = 512
            nsteps = (nb * ls // _tile(nb * ls, 1024)) * (d // _tile(d, tn_ride))
            mod_job = _mod_job(ada_w, lw["layer"] + 1, nsteps)
            mod_rider = None if mod_job is None else (mod_job, (cond, ada_w, ada_b))
            res = _mm_call(merged, lw["w_out"][None], BF16, "w_out", tn_pref=tn_ride,
                           rider=lw.pop("ride_next_w_in"), mod_rider=mod_rider)
            out, nxt_w["w_in"] = res[0], res[1]
            if mod_rider is not None:
                nxt_w["mods"] = res[2]
        else:
            out = _mm_call(merged, lw["w_out"][None], BF16, "w_out")
        return out.reshape(nb, ls, d)

    def conv_ffn(h, lw, period):
        if "ride_next_w12" in lw:
            act, nxt_w["w12"] = _ffn1_call(h, lw["w12"], lw["conv_w"], lw["conv_b"], period,
                                           rider=lw.pop("ride_next_w12"))
        else:
            act = _ffn1_call(h, lw["w12"], lw["conv_w"], lw["conv_b"], period)
        return _mm_call(act, lw["w2"][None], BF16, "ffn2", tm_pref=512, tn_pref=512)

    xs = x
    cs = ctx.reshape(1, nb * clen, d)
    nxt_w = {}
    hx = _ln_mod_call(xs, mods_list[0], None, 0, 1)
    hc = _ln_mod_call(cs, mods_list[0], ctx_row, 0, 1)
    for i in range(depth):
        last = i == depth - 1
        mods = mods_list[i]
        lw = {
            "layer": i,
            "ssm_d": ssm_d[i].reshape(1, sw),
            "ride_fourier_w": (_rows_cast(fourier_w, i, _tile(fw, CAST_ROWS)), fourier_w),
            "ride_glu_w": (_rows_cast(glu_w, i, _tile(sw, CAST_ROWS // 2)), glu_w),
            "ride_w_out": (_rows_cast(w_out, i, _tile(d, CAST_ROWS)), w_out),
        }
        cur_w, nxt_w = nxt_w, {}
        ride_w12 = ride_w2 = None
        if ffn_pad:
            if "w12" in cur_w:
                lw["w12"] = cur_w["w12"]
            else:
                ride_w12 = (_pad_halves_cast(ffn_w12, i, ffn_pad), ffn_w12)
            ride_w2 = (_pad_rows_cast(ffn_w2, i, ffn_pad), ffn_w2)
            if not last:
                lw["ride_next_w_in"] = (_rows_cast(w_in, i + 1, _tile(d, CAST_ROWS // 4)), w_in)
                lw["ride_next_w12"] = (_pad_halves_cast(ffn_w12, i + 1, ffn_pad), ffn_w12)
        else:
            lw["w12"] = _cast_call(ffn_w12, i)
            lw["w2"] = _cast_call(ffn_w2, i)
        lw["conv_w"] = jnp.pad(ffn_conv_w[i], ((0, 0), (0, ffn_pad)))
        lw["conv_b"] = jnp.pad(ffn_conv_b[i].reshape(1, ffn), ((0, 0), (0, ffn_pad)))
        g1, b1 = ln1_g[i].reshape(1, d), ln1_b[i].reshape(1, d)
        g2, b2 = ln2_g[i].reshape(1, d), ln2_b[i].reshape(1, d)
        bmat, cmat, amat = _ssm_tables(ssm_a_re[i], ssm_a_im[i], ssm_log_dt[i], ssm_b_re[i],
                                       ssm_b_im[i], ssm_c_re[i], ssm_c_im[i])

        n_in = w_in.shape[2]
        hc2d = hc.reshape(nb * clen, d)
        c_col = 0 if last else fw
        if "w_in" in cur_w:
            w_in_b = cur_w["w_in"]
            proj_x = _mm_call(hx.reshape(1, nb * seq, d), w_in_b[None], BF16, "w_in", rider=ride_w12)
            w_in_c = w_in_b[None, :, fw:fw + sw] if last else w_in_b[None]
            proj_c = _mm_call(hc2d[None], w_in_c, BF16, "w_in_ctx")
        else:
            proj_x = _mm_w32_call(hx.reshape(nb * seq, d), w_in, i, 0, n_in, BF16, "w_in", ride_w12)
            if last:
                proj_c = _mm_w32_call(hc2d, w_in, i, fw, sw, BF16, "w_in_ctx")
            else:
                proj_c = _mm_w32_call(hc2d, w_in, i, 0, n_in, BF16, "w_in_ctx")
        if ride_w12 is not None:
            proj_x, lw["w12"] = proj_x
        proj_x = proj_x.reshape(nb, seq, n_in)
        proj_c = proj_c.reshape(nb, clen, -1)
        ssm_out = _ssm_call(proj_x, fw, proj_c, c_col, bmat, cmat, amat, chunk, ride_w2)
        yf, yb = ssm_out[0], ssm_out[1]
        if ride_w2 is not None:
            lw["w2"] = ssm_out[2]

        out_x = token_mix_tail(proj_x, csl_x, yf, yb, 0, lw)
        if not last and len(mods_list) == i + 1:
            if "mods" in nxt_w:
                mods_list.append(nxt_w["mods"].reshape(n_rows, 1, N_MOD * d))
            else:
                mods_list.append(_adaln_call(cond, ada_w, ada_b, i + 1, 1).reshape(n_rows, 1, N_MOD * d))
        nxt = None if last else (mods_list[i + 1], 0, 1)
        x1, h2 = _postnorm_call(xs, out_x, mods, None, 2, g1, b1, alpha, (mods, 3, 4))
        ffn_x = conv_ffn(h2, lw, GRID_W)
        xs, hx = _postnorm_call(x1, ffn_x, mods, None, 5, g2, b2, alpha, nxt)

        if not last:
            out_c = token_mix_tail(proj_c, csl_c, yf, yb, seq, lw).reshape(1, nb * clen, d)
            c1, hc2 = _postnorm_call(cs, out_c, mods, ctx_row, 2, g1, b1, alpha, (mods, 3, 4))
            ffn_c = conv_ffn(hc2, lw, clen)
            cs, hc = _postnorm_call(c1, ffn_c, mods, ctx_row, 5, g2, b2, alpha, nxt)
    return xs
```

```python
import functools
import math
from typing import Callable, NamedTuple

import jax
import jax.numpy as jnp
import ml_dtypes
import numpy as np
from jax import lax
from jax.experimental import pallas as pl
from jax.experimental.pallas import tpu as pltpu

GRID_W = 64
FOURIER_GROUPS = 4
N_MOD = 6
LN_EPS = 1e-6
SSM_BLOCK_GROUPS = 16
LANES = 128
SUBLANES = 8
VMEM_LIMIT = 56 * 1024 * 1024
GLU_SUBTILE = 256
CAST_ROWS = 256
HALVES_CAST_COLS = 128
SSM_MM_ROWS = 256

F32 = jnp.float32
BF16 = jnp.bfloat16


def _params(sem):
    return pltpu.CompilerParams(dimension_semantics=sem, vmem_limit_bytes=VMEM_LIMIT)


def _tile(n, pref):
    if n <= pref:
        return n
    while n % pref:
        pref //= 2
    assert pref >= SUBLANES, (n, pref)
    return pref


def _lane_tile(n, cap):
    for t in range(cap - cap % LANES, 0, -LANES):
        if n % t == 0:
            return t
    return n


def _dot(a, b):
    return jnp.dot(a, b, preferred_element_type=F32)


def _layer_norm(x):
    mu = jnp.mean(x, axis=-1, keepdims=True)
    xc = x - mu
    var = jnp.mean(xc * xc, axis=-1, keepdims=True)
    return xc * lax.rsqrt(var + LN_EPS)


def _mod_spec(mod_row, k, d):
    if mod_row is None:
        return pl.BlockSpec((None, 1, d), lambda b, *_: (b, 0, k))
    return pl.BlockSpec((None, 1, d), lambda b, *_: (mod_row, 0, k))


def _cast_kernel(w_ref, o_ref):
    o_ref[...] = w_ref[...].astype(o_ref.dtype)


def _cast_call(w, layer):
    _, r, c = w.shape
    tr = _tile(r, 512)
    tc = _lane_tile(c, 6144)
    return pl.pallas_call(
        _cast_kernel,
        grid=(r // tr, c // tc),
        in_specs=[pl.BlockSpec((None, tr, tc), lambda i, j: (layer, i, j))],
        out_specs=pl.BlockSpec((tr, tc), lambda i, j: (i, j)),
        out_shape=jax.ShapeDtypeStruct((r, c), BF16),
        compiler_params=_params(("arbitrary", "arbitrary")),
        name="cast_w",
    )(w)


class _PadCast(NamedTuple):
    layer: int
    in_block: tuple
    in_index: Callable
    out_block: tuple
    out_index: Callable
    out_shape: tuple
    nblk: int
    is_data: Callable


def _rows_cast(w, layer, tr):
    _, r, c = w.shape
    assert r % tr == 0
    return _PadCast(layer, (None, tr, c), lambda s: (s, 0), (tr, c), lambda s: (s, 0), (r, c),
                    r // tr, lambda s: s >= 0)


def _pad_rows_cast(w, layer, tr):
    _, r, c = w.shape
    nvalid = r // tr
    assert r % tr == 0
    return _PadCast(layer, (None, tr, c), lambda s: (jnp.minimum(s, nvalid - 1), 0),
                    (tr, c), lambda s: (s, 0), (r + tr, c), nvalid + 1, lambda s: s < nvalid)


def _pad_halves_cast(w, layer, pad, tc):
    _, r, c2 = w.shape
    f = c2 // 2
    nvalid = f // tc
    assert f % tc == 0 and pad % tc == 0
    per = nvalid + pad // tc
    return _PadCast(layer, (None, r, tc),
                    lambda s: (0, (s // per) * nvalid + jnp.minimum(s % per, nvalid - 1)),
                    (r, tc), lambda s: (0, s), (r, 2 * (f + pad)), 2 * per, lambda s: s % per < nvalid)


def _cast_specs(job, step_of):
    def clamped(*g):
        return jnp.minimum(step_of(*g), job.nblk - 1)

    in_spec = pl.BlockSpec(job.in_block, lambda *g: (job.layer,) + tuple(job.in_index(clamped(*g))))
    out_spec = pl.BlockSpec(job.out_block, lambda *g: tuple(job.out_index(clamped(*g))))
    return in_spec, out_spec


def _when_live(job, step, nsteps, work):
    if job.nblk == nsteps:
        work()
    else:
        pl.when(step < job.nblk)(work)


def _cast_step(job, step, w_ref, o_ref, nsteps=None):
    def work():
        o_ref[...] = jnp.where(job.is_data(step), w_ref[...], 0.0).astype(o_ref.dtype)

    _when_live(job, step, nsteps, work)


def _rider_fits(rider, nsteps):
    return rider[0].nblk <= nsteps


def _attach_rider(rider, step_of, in_specs, out_specs, out_shape, args):
    job, rw = rider
    r_in, r_out = _cast_specs(job, step_of)
    in_specs.append(r_in)
    out_specs.append(r_out)
    out_shape.append(jax.ShapeDtypeStruct(job.out_shape, BF16))
    args.append(rw)
    return job


def _pad_cast_kernel(w_ref, o_ref, *, job):
    _cast_step(job, pl.program_id(0), w_ref, o_ref)


def _pad_cast_call(job, w):
    in_spec, out_spec = _cast_specs(job, lambda s: s)
    return pl.pallas_call(
        functools.partial(_pad_cast_kernel, job=job),
        grid=(job.nblk,),
        in_specs=[in_spec],
        out_specs=out_spec,
        out_shape=jax.ShapeDtypeStruct(job.out_shape, BF16),
        compiler_params=_params(("arbitrary",)),
        name="cast_w_pad",
    )(w)


def _adaln_kernel(c_ref, w_ref, b_ref, o_ref):
    c = c_ref[...]
    s = (c * jax.nn.sigmoid(c)).astype(BF16)
    o_ref[...] = _dot(s, w_ref[...].astype(BF16)) + b_ref[...]


class _ModJob(NamedTuple):
    layer: int
    tcol: int
    nblk: int


def _mod_job(ada_w, layer, nsteps):
    n = ada_w.shape[2]
    for tcol in range(LANES, n + 1, LANES):
        if n % tcol == 0 and n // tcol <= nsteps:
            return _ModJob(layer, tcol, n // tcol)
    return None


def _mod_attach(job, operands, step_of, in_specs, out_specs, out_shape, args):
    cond, ada_w, ada_b = operands
    r, d = cond.shape

    def clamped(*g):
        return jnp.minimum(step_of(*g), job.nblk - 1)

    in_specs += [
        pl.BlockSpec((r, d), lambda *g: (0, 0)),
        pl.BlockSpec((None, d, job.tcol), lambda *g: (job.layer, 0, clamped(*g))),
        pl.BlockSpec((None, 1, job.tcol), lambda *g: (job.layer, 0, clamped(*g))),
    ]
    out_specs.append(pl.BlockSpec((r, job.tcol), lambda *g: (0, clamped(*g))))
    out_shape.append(jax.ShapeDtypeStruct((r, ada_w.shape[2]), F32))
    args += [cond, ada_w, ada_b.reshape(ada_b.shape[0], 1, -1)]


def _mod_step(job, step, c_ref, w_ref, b_ref, o_ref, nsteps=None):
    _when_live(job, step, nsteps, functools.partial(_adaln_kernel, c_ref, w_ref, b_ref, o_ref))


def _adaln_call(cond, ada_w, ada_b, layer_lo=0, nlayer=None):
    all_layers, d, n = ada_w.shape
    depth = all_layers - layer_lo if nlayer is None else nlayer
    r = cond.shape[0]
    tn = _tile(n, 512)
    return pl.pallas_call(
        _adaln_kernel,
        grid=(depth, n // tn),
        in_specs=[
            pl.BlockSpec((r, d), lambda l, j: (0, 0)),
            pl.BlockSpec((None, d, tn), lambda l, j: (layer_lo + l, 0, j)),
            pl.BlockSpec((None, 1, tn), lambda l, j: (layer_lo + l, 0, j)),
        ],
        out_specs=pl.BlockSpec((None, r, tn), lambda l, j: (l, 0, j)),
        out_shape=jax.ShapeDtypeStruct((depth, r, n), F32),
        compiler_params=_params(("arbitrary", "arbitrary")),
        name="adaln",
    )(cond, ada_w, ada_b.reshape(all_layers, 1, n))


def _ln_mod_kernel(x_ref, sh_ref, sc_ref, o_ref):
    y = _layer_norm(x_ref[...])
    o_ref[...] = (y * (1.0 + sc_ref[...]) + sh_ref[...]).astype(o_ref.dtype)


def _ln_mod_call(x, mods, mod_row, k_shift, k_scale):
    bm, lm, d = x.shape
    tm = _tile(lm, 256)
    return pl.pallas_call(
        _ln_mod_kernel,
        grid=(bm, lm // tm),
        in_specs=[
            pl.BlockSpec((None, tm, d), lambda b, i: (b, i, 0)),
            _mod_spec(mod_row, k_shift, d),
            _mod_spec(mod_row, k_scale, d),
        ],
        out_specs=pl.BlockSpec((None, tm, d), lambda b, i: (b, i, 0)),
        out_shape=jax.ShapeDtypeStruct((bm, lm, d), BF16),
        compiler_params=_params(("arbitrary", "arbitrary")),
        name="ln_mod",
    )(x, mods, mods)


def _postnorm_kernel(res_ref, pre_ref, gate_ref, g_ref, b_ref, *rest, alpha, with_next):
    z = alpha * res_ref[...] + gate_ref[...] * pre_ref[...].astype(F32)
    x_new = _layer_norm(z) * g_ref[...] + b_ref[...]
    if with_next:
        sh_ref, sc_ref, o_ref, h_ref = rest
        o_ref[...] = x_new
        h_ref[...] = (_layer_norm(x_new) * (1.0 + sc_ref[...]) + sh_ref[...]).astype(h_ref.dtype)
    else:
        (o_ref,) = rest
        o_ref[...] = x_new


def _postnorm_call(res, pre, mods, mod_row, k_gate, ln_g, ln_b, alpha, next_mod=None):
    bm, lm, d = res.shape
    tm = _tile(lm, 256)
    row = pl.BlockSpec((None, tm, d), lambda b, i: (b, i, 0))
    vec = pl.BlockSpec((1, d), lambda b, i: (0, 0))
    in_specs = [row, row, _mod_spec(mod_row, k_gate, d), vec, vec]
    args = [res, pre, mods, ln_g, ln_b]
    out_specs = [row]
    out_shape = [jax.ShapeDtypeStruct((bm, lm, d), F32)]
    if next_mod is not None:
        nmods, k_shift, k_scale = next_mod
        in_specs += [_mod_spec(mod_row, k_shift, d), _mod_spec(mod_row, k_scale, d)]
        args += [nmods, nmods]
        out_specs.append(row)
        out_shape.append(jax.ShapeDtypeStruct((bm, lm, d), BF16))
    out = pl.pallas_call(
        functools.partial(_postnorm_kernel, alpha=alpha, with_next=next_mod is not None),
        grid=(bm, lm // tm),
        in_specs=in_specs,
        out_specs=out_specs,
        out_shape=out_shape,
        compiler_params=_params(("arbitrary", "arbitrary")),
        name="postnorm",
    )(*args)
    return (out[0], out[1]) if next_mod is not None else (out[0], None)


def _mm_kernel(a_ref, w_ref, *rest, nb, ni, nj, job, mod_job):
    n_side_in = (job is not None) + 3 * (mod_job is not None)
    side_in, (o_ref, *side_out) = rest[:n_side_in], rest[n_side_in:]
    step = (pl.program_id(0) * ni + pl.program_id(1)) * nj + pl.program_id(2)
    nsteps = nb * ni * nj
    if job is not None:
        _cast_step(job, step, side_in[0], side_out[0], nsteps)
    if mod_job is not None:
        _mod_step(mod_job, step, *side_in[-3:], side_out[-1], nsteps)
    o_ref[...] = _dot(a_ref[...], w_ref[...]).astype(o_ref.dtype)


def _mm_call(a, w, out_dtype, name, tm_pref=1024, tn_pref=1024, rider=None, mod_rider=None):
    ba, m, k = a.shape
    bw, _, n = w.shape
    nb = max(ba, bw)
    tm = _tile(m, tm_pref)
    tn = _tile(n, tn_pref)
    ni, nj = m // tm, n // tn
    if rider is not None and not _rider_fits(rider, nb * ni * nj):
        out = _mm_call(a, w, out_dtype, name, tm_pref, tn_pref, mod_rider=mod_rider)
        out = out if mod_rider is not None else (out,)
        return (out[0], _pad_cast_call(*rider)) + tuple(out[1:])
    rows_outer = ba == 1 and bw > 1
    n0, n1 = (ni, nb) if rows_outer else (nb, ni)

    def in_grid_order(f):
        return (lambda i, b, j: f(b, i, j)) if rows_outer else f

    a_map = (lambda b, i, j: (b, i, 0)) if ba > 1 else (lambda b, i, j: (0, i, 0))
    w_map = (lambda b, i, j: (b, 0, j)) if bw > 1 else (lambda b, i, j: (0, 0, j))
    in_specs = [pl.BlockSpec((None, tm, k), in_grid_order(a_map)),
                pl.BlockSpec((None, k, tn), in_grid_order(w_map))]
    out_specs = [pl.BlockSpec((None, tm, tn), in_grid_order(lambda b, i, j: (b, i, j)))]
    out_shape = [jax.ShapeDtypeStruct((nb, m, n), out_dtype)]
    args = [a, w]
    job = mod_job = None

    def step_of(p0, p1, j):
        return (p0 * n1 + p1) * nj + j

    if rider is not None:
        job = _attach_rider(rider, step_of, in_specs, out_specs, out_shape, args)
    if mod_rider is not None:
        mod_job, operands = mod_rider
        assert mod_job.nblk <= nb * ni * nj
        _mod_attach(mod_job, operands, step_of, in_specs, out_specs, out_shape, args)
    out = pl.pallas_call(
        functools.partial(_mm_kernel, nb=n0, ni=n1, nj=nj, job=job, mod_job=mod_job),
        grid=(n0, n1, nj),
        in_specs=in_specs,
        out_specs=out_specs,
        out_shape=out_shape,
        compiler_params=_params(("arbitrary", "arbitrary", "arbitrary")),
        name=name,
    )(*args)
    return out if len(out) > 1 else out[0]


def _mm_w32_kernel(a_ref, w_ref, *rest, ni, job):
    if job is None:
        o_ref, w_scr = rest
    else:
        rw_ref, o_ref, ro_ref, w_scr = rest
    i = pl.program_id(1)

    @pl.when(i == 0)
    def _():
        w_scr[...] = w_ref[...].astype(w_scr.dtype)

    if job is not None:
        _cast_step(job, pl.program_id(0) * ni + i, rw_ref, ro_ref)
    o_ref[...] = _dot(a_ref[...], w_scr[...]).astype(o_ref.dtype)


def _mm_w32_call(a, w, layer, col0, ncols, out_dtype, name, rider=None):
    m, k = a.shape
    tm = _tile(m, 1024)
    tn = _tile(ncols, 512)
    assert col0 % tn == 0
    j0 = col0 // tn
    ni = m // tm
    in_specs = [
        pl.BlockSpec((tm, k), lambda j, i: (i, 0)),
        pl.BlockSpec((None, k, tn), lambda j, i: (layer, 0, j0 + j)),
    ]
    out_specs = [pl.BlockSpec((tm, tn), lambda j, i: (i, j))]
    out_shape = [jax.ShapeDtypeStruct((m, ncols), out_dtype)]
    args = [a, w]
    job = None
    if rider is not None and not _rider_fits(rider, (ncols // tn) * ni):
        return _mm_w32_call(a, w, layer, col0, ncols, out_dtype, name), _pad_cast_call(*rider)
    if rider is not None:
        job = _attach_rider(rider, lambda j, i: j * ni + i, in_specs, out_specs, out_shape, args)
    out = pl.pallas_call(
        functools.partial(_mm_w32_kernel, ni=ni, job=job),
        grid=(ncols // tn, ni),
        in_specs=in_specs,
        out_specs=out_specs,
        out_shape=out_shape,
        scratch_shapes=[pltpu.VMEM((k, tn), BF16)],
        compiler_params=_params(("arbitrary", "arbitrary")),
        name=name,
    )(*args)
    return out if rider is not None else out[0]


def _chan_dft_kernel(u_ref, cs_ref, o_ref, *, gd):
    r = _dot(u_ref[...], cs_ref[...])
    o_ref[0] = r[:, :gd].astype(o_ref.dtype)
    o_ref[1] = r[:, gd:].astype(o_ref.dtype)


def _chan_dft_call(proj, cs, fw):
    bs, ls, _ = proj.shape
    gd = fw // FOURIER_GROUPS
    tm = _tile(ls, 1024)
    return pl.pallas_call(
        functools.partial(_chan_dft_kernel, gd=gd),
        grid=(bs, ls // tm, FOURIER_GROUPS),
        in_specs=[
            pl.BlockSpec((None, tm, gd), lambda b, i, g: (b, i, g)),
            pl.BlockSpec((gd, 2 * gd), lambda b, i, g: (0, 0)),
        ],
        out_specs=pl.BlockSpec((None, 2, tm, gd), lambda b, i, g: (b, 0, i, g)),
        out_shape=jax.ShapeDtypeStruct((bs, 2, ls, fw), BF16),
        compiler_params=_params(("arbitrary", "arbitrary", "arbitrary")),
        name="chan_dft",
    )(proj, cs)


def _ssm_kernel(uxf_ref, uxb_ref, ucf_ref, ucb_ref, bm_ref, cm_ref, a_ref, *rest,
                chunk, nbat, half, n_ctx, n_all, job):
    side_work = None
    if job is None:
        yf_ref, yb_ref, *scratch = rest
    else:
        rw_ref, yf_ref, yb_ref, ro_ref, *scratch = rest
        step = pl.program_id(0) * n_all + pl.program_id(1)
        side_work = functools.partial(_cast_step, job, step, rw_ref, ro_ref)
    _ssm_step(uxf_ref, uxb_ref, ucf_ref, ucb_ref, bm_ref, cm_ref, a_ref, yf_ref, yb_ref,
              *scratch, chunk=chunk, nbat=nbat, half=half, n_ctx=n_ctx, side_work=side_work)


def _ssm_step(uxf_ref, uxb_ref, ucf_ref, ucb_ref, bm_ref, cm_ref, a_ref, yf_ref, yb_ref,
              u_scr, uin_scr, s_scr, y_scr, h_scr, *, chunk, nbat, half, n_ctx, side_work):
    k = pl.program_id(0)
    c = pl.program_id(1)
    rows = SUBLANES
    hw = half // 2
    mm_rows = min(chunk * rows, SSM_MM_ROWS)
    fwd_starts = list(range(0, chunk * rows, mm_rows))
    row_starts = (fwd_starts, fwd_starts[::-1])

    @pl.when(jnp.logical_and(k == 0, c == 0))
    def _():
        uin_scr[...] = jnp.zeros_like(uin_scr)

    @pl.when(c == 0)
    def _():
        h_scr[...] = jnp.zeros_like(h_scr)

    @pl.when(c < n_ctx)
    def _():
        u_scr[0] = ucf_ref[...]
        u_scr[1] = ucb_ref[...]

    @pl.when(c >= n_ctx)
    def _():
        u_scr[0] = uxf_ref[...]
        u_scr[1] = uxb_ref[...]

    if side_work is not None:
        side_work()
    for d in range(2):
        for b in range(nbat):
            ub = u_scr[d, b].astype(F32)
            for h in range(2):
                uin_scr[d, h, pl.ds(2 * b + h, chunk, stride=rows), :] = ub[:, h * LANES:(h + 1) * LANES]
        bm = bm_ref[d]
        for r0 in row_starts[d]:
            rs = slice(r0, r0 + mm_rows)
            lhs = jnp.concatenate([uin_scr[d, 0, rs], uin_scr[d, 1, rs]], axis=1).astype(BF16)
            s_scr[d, rs] = _dot(lhs, bm)

    arf, aif = a_ref[0, 0], a_ref[0, 1]
    arb, aib = a_ref[1, 0], a_ref[1, 1]

    def body(t, carry):
        hrf, hif, hrb, hib = carry
        rowf = t * rows
        rowb = (chunk - 1 - t) * rows
        xf = s_scr[0, pl.ds(rowf, rows), :]
        xb = s_scr[1, pl.ds(rowb, rows), :]
        nrf = arf * hrf - aif * hif + xf[:, :hw]
        nif = arf * hif + aif * hrf + xf[:, hw:]
        nrb = arb * hrb - aib * hib + xb[:, :hw]
        nib = arb * hib + aib * hrb + xb[:, hw:]
        s_scr[0, pl.ds(rowf, rows), :hw] = nrf
        s_scr[0, pl.ds(rowf, rows), hw:] = nif
        s_scr[1, pl.ds(rowb, rows), :hw] = nrb
        s_scr[1, pl.ds(rowb, rows), hw:] = nib
        return nrf, nif, nrb, nib

    init = (h_scr[0, :, :hw], h_scr[0, :, hw:], h_scr[1, :, :hw], h_scr[1, :, hw:])
    carry = init
    for t in range(chunk):
        carry = body(t, carry)
    hrf, hif, hrb, hib = carry
    h_scr[0, :, :hw] = hrf
    h_scr[0, :, hw:] = hif
    h_scr[1, :, :hw] = hrb
    h_scr[1, :, hw:] = hib

    for d, y_ref in enumerate((yf_ref, yb_ref)):
        cm = cm_ref[d]
        for r0 in row_starts[d]:
            rs = slice(r0, r0 + mm_rows)
            out = _dot(s_scr[d, rs].astype(BF16), cm)
            y_scr[d, 0, rs] = out[:, :LANES]
            y_scr[d, 1, rs] = out[:, LANES:]
        for b in range(nbat):
            for h in range(2):
                piece = y_scr[d, h, pl.ds(2 * b + h, chunk, stride=rows), :]
                y_ref[b, :, h * LANES:(h + 1) * LANES] = piece.astype(y_ref.dtype)


def _ssm_call(ux, ux_col, uc, uc_col, bmat, cmat, amat, chunk, rider=None):
    nbat, seq, _ = ux.shape
    clen = uc.shape[1]
    assert 2 * nbat == SUBLANES, "the scan packs 2 column halves x batch on the 8 sublanes"
    ndir, nblk, cin, cst = bmat.shape
    assert cin == 2 * LANES, "each column half of a group block is one 128-lane input slab"
    sw = nblk * cin
    n_lat, n_ctx = seq // chunk, clen // chunk
    n_all = n_lat + n_ctx
    assert ux_col % cin == 0 and uc_col % cin == 0
    xo, co = ux_col // cin, uc_col // cin

    def lat_f(c):
        return jnp.maximum(c - n_ctx, 0)

    def lat_b(c):
        return n_lat - 1 - jnp.maximum(c - n_ctx, 0)

    def ctx_f(c):
        return jnp.minimum(c, n_ctx - 1)

    def ctx_b(c):
        return jnp.maximum(n_ctx - 1 - c, 0)

    def out_f(c):
        return jnp.where(c < n_ctx, n_lat + c, c - n_ctx)

    def out_b(c):
        return jnp.where(c < n_ctx, n_lat + n_ctx - 1 - c, n_lat - 1 - (c - n_ctx))

    ublk = (nbat, chunk, cin)
    in_specs = [
        pl.BlockSpec(ublk, lambda k, c: (0, lat_f(c), xo + k)),
        pl.BlockSpec(ublk, lambda k, c: (0, lat_b(c), xo + k)),
        pl.BlockSpec(ublk, lambda k, c: (0, ctx_f(c), co + k)),
        pl.BlockSpec(ublk, lambda k, c: (0, ctx_b(c), co + k)),
        pl.BlockSpec((ndir, None, cin, cst), lambda k, c: (0, k, 0, 0)),
        pl.BlockSpec((ndir, None, cst, cin), lambda k, c: (0, k, 0, 0)),
        pl.BlockSpec((ndir, None, 2, SUBLANES, cst // 2), lambda k, c: (0, k, 0, 0, 0)),
    ]
    out_specs = [
        pl.BlockSpec(ublk, lambda k, c: (0, out_f(c), k)),
        pl.BlockSpec(ublk, lambda k, c: (0, out_b(c), k)),
    ]
    out_shape = [jax.ShapeDtypeStruct((nbat, seq + clen, sw), BF16)] * 2
    args = [ux, ux, uc, uc, bmat, cmat, amat]
    job = None
    if rider is not None and not _rider_fits(rider, nblk * n_all):
        yf, yb = _ssm_call(ux, ux_col, uc, uc_col, bmat, cmat, amat, chunk)
        return yf, yb, _pad_cast_call(*rider)
    if rider is not None:
        job = _attach_rider(rider, lambda k, c: k * n_all + c, in_specs, out_specs, out_shape, args)
    return pl.pallas_call(
        functools.partial(_ssm_kernel, chunk=chunk, nbat=nbat, half=cst, n_ctx=n_ctx,
                          n_all=n_all, job=job),
        grid=(nblk, n_all),
        in_specs=in_specs,
        out_specs=out_specs,
        out_shape=out_shape,
        scratch_shapes=[
            pltpu.VMEM((ndir, nbat, chunk, cin), BF16),
            pltpu.VMEM((ndir, 2, SUBLANES * chunk, LANES), F32),
            pltpu.VMEM((ndir, SUBLANES * chunk, cst), F32),
            pltpu.VMEM((ndir, 2, SUBLANES * chunk, LANES), F32),
            pltpu.VMEM((ndir, SUBLANES, cst), F32),
        ],
        compiler_params=_params(("arbitrary", "arbitrary")),
        name="ssm",
    )(*args)


def _ssm_tables(a_re, a_im, log_dt, b_re, b_im, c_re, c_im):
    ndir, g, p = a_re.shape
    cg = b_re.shape[-1]
    gb = SSM_BLOCK_GROUPS
    nblk = g // gb
    hg = gb // 2
    npair = hg // 2
    assert 2 * p == LANES and g % gb == 0 and hg * cg == LANES
    lr, li = a_re.astype(F32), a_im.astype(F32)
    dt = jnp.exp(log_dt.astype(F32))[..., None]
    zr, zi = lr * dt, li * dt
    a_bar_r = jnp.exp(zr) * jnp.cos(zi)
    a_bar_i = jnp.exp(zr) * jnp.sin(zi)
    em1_r = jnp.expm1(zr) * jnp.cos(zi) - 2.0 * jnp.square(jnp.sin(0.5 * zi))
    den = lr * lr + li * li
    cf_r = ((em1_r * lr + a_bar_i * li) / den)[..., None]
    cf_i = ((a_bar_i * lr - em1_r * li) / den)[..., None]
    br, bi = b_re.astype(F32), b_im.astype(F32)
    b_bar_r = cf_r * br - cf_i * bi
    b_bar_i = cf_r * bi + cf_i * br

    lane_q = np.arange(LANES) // p
    own = (np.arange(gb)[:, None, None] % hg
           == 2 * np.arange(npair)[None, :, None] + lane_q[None, None, :])
    own_in = np.repeat(own, cg, axis=0).astype(np.float32)
    own_in = own_in.reshape(gb * cg, npair * LANES)
    own_out = np.ascontiguousarray(own_in.T)

    def in_mat(bb):
        t = jnp.swapaxes(bb.reshape(ndir, nblk, gb, p, cg), -1, -2).reshape(ndir, nblk, gb * cg, p)
        return jnp.tile(t, (1, 1, 1, 2 * npair)) * own_in

    def out_mat(cc):
        t = jnp.swapaxes(cc.reshape(ndir, nblk, gb * cg, p), -1, -2)
        return jnp.tile(t, (1, 1, 2 * npair, 1)) * own_out

    bmat = jnp.concatenate([in_mat(b_bar_r), in_mat(b_bar_i)], axis=-1).astype(BF16)
    cmat = jnp.concatenate([out_mat(c_re.astype(F32)), out_mat(-c_im.astype(F32))], axis=-2).astype(BF16)

    def decay(a):
        t = a.reshape(ndir, nblk, 2, npair * LANES)
        return jnp.tile(t, (1, 1, SUBLANES // 2, 1))
    aa = jnp.stack([decay(a_bar_r), decay(a_bar_i)], axis=2)
    return bmat, cmat, aa


def _ssm_act_kernel(u_ref, yf_ref, yb_ref, d_ref, o_ref):
    y = d_ref[...] * u_ref[...].astype(F32) + yf_ref[...].astype(F32) + yb_ref[...].astype(F32)
    o_ref[...] = jax.nn.gelu(y).astype(o_ref.dtype)


def _ssm_act_call(proj, yf, yb, row_off, ssm_d, fw, sw):
    bs, ls, _ = proj.shape
    tm = _tile(ls, 512)
    assert fw % sw == 0 and row_off % tm == 0
    u_blk, r_blk = fw // sw, row_off // tm
    return pl.pallas_call(
        _ssm_act_kernel,
        grid=(bs, ls // tm),
        in_specs=[
            pl.BlockSpec((None, tm, sw), lambda b, i: (b, i, u_blk)),
            pl.BlockSpec((None, tm, sw), lambda b, i: (b, i + r_blk, 0)),
            pl.BlockSpec((None, tm, sw), lambda b, i: (b, i + r_blk, 0)),
            pl.BlockSpec((1, sw), lambda b, i: (0, 0)),
        ],
        out_specs=pl.BlockSpec((None, tm, sw), lambda b, i: (b, i, 0)),
        out_shape=jax.ShapeDtypeStruct((bs, ls, sw), BF16),
        compiler_params=_params(("arbitrary", "arbitrary")),
        name="ssm_act",
    )(proj, yf, yb, ssm_d)


def _glu_kernel(a_ref, wv_ref, wg_ref, gf_ref, gs_ref, fo_ref, *rest, nb, ni, job):
    if job is None:
        (o_ref,) = rest
    else:
        rw_ref, o_ref, ro_ref = rest
        step = (pl.program_id(0) * nb + pl.program_id(1)) * ni + pl.program_id(2)
        _cast_step(job, step, rw_ref, ro_ref)
    a = a_ref[...]
    tn = o_ref.shape[1]
    sub = min(tn, GLU_SUBTILE)
    for n0 in range(0, tn, sub):
        cols = slice(n0, n0 + sub)
        y_s = (_dot(a, wv_ref[:, cols]) * jax.nn.sigmoid(_dot(a, wg_ref[:, cols]))).astype(BF16)
        o_ref[:, cols] = (jax.nn.sigmoid(gf_ref[:, cols]) * fo_ref[:, cols]
                          + jax.nn.sigmoid(gs_ref[:, cols]) * y_s)


def _glu_call(act, proj, glu_w, y_f, fw, sw, d, rider=None):
    bs, ls, _ = act.shape
    tm = _tile(ls, 512)
    tn = _tile(d, 1024)
    assert (fw + sw) % tn == 0
    gf_blk = (fw + sw) // tn
    gs_blk = (fw + sw + d) // tn
    ni, nj = ls // tm, d // tn
    if rider is not None and not _rider_fits(rider, bs * ni * nj):
        return _glu_call(act, proj, glu_w, y_f, fw, sw, d), _pad_cast_call(*rider)
    in_specs = [
        pl.BlockSpec((None, tm, sw), lambda j, b, i: (b, i, 0)),
        pl.BlockSpec((sw, tn), lambda j, b, i: (0, j)),
        pl.BlockSpec((sw, tn), lambda j, b, i: (0, nj + j)),
        pl.BlockSpec((None, tm, tn), lambda j, b, i: (b, i, gf_blk + j)),
        pl.BlockSpec((None, tm, tn), lambda j, b, i: (b, i, gs_blk + j)),
        pl.BlockSpec((None, tm, tn), lambda j, b, i: (b, i, j)),
    ]
    out_specs = [pl.BlockSpec((None, tm, tn), lambda j, b, i: (b, i, j))]
    out_shape = [jax.ShapeDtypeStruct((bs, ls, d), BF16)]
    args = [act, glu_w, glu_w, proj, proj, y_f]
    job = None
    if rider is not None:
        job = _attach_rider(rider, lambda j, b, i: (j * bs + b) * ni + i,
                            in_specs, out_specs, out_shape, args)
    out = pl.pallas_call(
        functools.partial(_glu_kernel, nb=bs, ni=ni, job=job),
        grid=(nj, bs, ni),
        in_specs=in_specs,
        out_specs=out_specs,
        out_shape=out_shape,
        compiler_params=_params(("arbitrary", "arbitrary", "arbitrary")),
        name="glu_merge",
    )(*args)
    return out if rider is not None else out[0]


def _ffn1_kernel(h_ref, wu_ref, wv_ref, cw_ref, cb_ref, *rest, period, nb, ni, nj, job):
    if job is None:
        (o_ref,) = rest
    else:
        rw_ref, o_ref, ro_ref = rest
        step = (pl.program_id(0) * ni + pl.program_id(1)) * nj + pl.program_id(2)
        _cast_step(job, step, rw_ref, ro_ref, nb * ni * nj)
    h = h_ref[...]
    u = _dot(h, wu_ref[...])
    v = _dot(h, wv_ref[...])
    tm = u.shape[0]
    pos = lax.broadcasted_iota(jnp.int32, u.shape, 0) % period
    prev = jnp.where(pos == 0, 0.0, pltpu.roll(u, 1, axis=0))
    nxt = jnp.where(pos == period - 1, 0.0, pltpu.roll(u, tm - 1, axis=0))
    cw = cw_ref[...]
    conv = cb_ref[...] + cw[0:1] * prev + cw[1:2] * u + cw[2:3] * nxt
    o_ref[...] = (jax.nn.gelu(conv) * v).astype(o_ref.dtype)


def _ffn1_call(h, w12, conv_w, conv_b, period, rider=None):
    bm, lm, d = h.shape
    ffn = w12.shape[1] // 2
    tm = _tile(lm, 1024)
    tn = _tile(ffn, 512)
    ni, nj = lm // tm, ffn // tn
    assert tm % period == 0
    if rider is not None and not _rider_fits(rider, bm * ni * nj):
        return _ffn1_call(h, w12, conv_w, conv_b, period), _pad_cast_call(*rider)
    in_specs = [
        pl.BlockSpec((None, tm, d), lambda b, i, j: (b, i, 0)),
        pl.BlockSpec((d, tn), lambda b, i, j: (0, j)),
        pl.BlockSpec((d, tn), lambda b, i, j: (0, nj + j)),
        pl.BlockSpec((3, tn), lambda b, i, j: (0, j)),
        pl.BlockSpec((1, tn), lambda b, i, j: (0, j)),
    ]
    out_specs = [pl.BlockSpec((None, tm, tn), lambda b, i, j: (b, i, j))]
    out_shape = [jax.ShapeDtypeStruct((bm, lm, ffn), BF16)]
    args = [h, w12, w12, conv_w, conv_b]
    job = None
    if rider is not None:
        job = _attach_rider(rider, lambda b, i, j: (b * ni + i) * nj + j,
                            in_specs, out_specs, out_shape, args)
    out = pl.pallas_call(
        functools.partial(_ffn1_kernel, period=period, nb=bm, ni=ni, nj=nj, job=job),
        grid=(bm, ni, nj),
        in_specs=in_specs,
        out_specs=out_specs,
        out_shape=out_shape,
        compiler_params=_params(("arbitrary", "arbitrary", "arbitrary")),
        name="ffn1",
    )(*args)
    return out if rider is not None else out[0]


def _dft_cos_sin(n):
    idx = np.arange(n, dtype=np.int64)
    ang = (2.0 * math.pi / n) * ((idx[:, None] * idx[None, :]) % n)
    scale = 1.0 / math.sqrt(n)
    return np.cos(ang) * scale, np.sin(ang) * scale


def _bf16_const(a):
    return jnp.asarray(np.asarray(a, np.float32).astype(ml_dtypes.bfloat16))


def kernel(x, c, ctx, c_ctx, ada_w, ada_b, w_in, fourier_w, ssm_a_re, ssm_a_im, ssm_log_dt,
           ssm_b_re, ssm_b_im, ssm_c_re, ssm_c_im, ssm_d, glu_w, w_out, ln1_g, ln1_b,
           ffn_w12, ffn_conv_w, ffn_conv_b, ffn_w2, ln2_g, ln2_b):
    nb, seq, d = x.shape
    clen = ctx.shape[1]
    depth = ada_w.shape[0]
    fw = fourier_w.shape[1]
    sw = ssm_d.shape[1]
    ffn = ffn_conv_b.shape[1]
    alpha = (2.0 * depth) ** 0.25
    gd = fw // FOURIER_GROUPS
    chunk = clen
    assert seq % chunk == 0 and seq % GRID_W == 0
    ffn_pad = 256 if (ffn % 512 == 256) else 0

    ctx_row = nb
    n_rows = -(-(nb + 1) // SUBLANES) * SUBLANES
    cond = jnp.zeros((n_rows, d), F32).at[:nb].set(c).at[ctx_row].set(c_ctx)
    ride_mods = ffn_pad > 0
    first_mods = _adaln_call(cond, ada_w, ada_b, 0, 1 if ride_mods else depth)
    mods_list = [m.reshape(n_rows, 1, N_MOD * d) for m in first_mods]

    cos_c, sin_c = _dft_cos_sin(gd)
    cs_chan = _bf16_const(np.concatenate([cos_c, sin_c], axis=1))

    def pos_dft_matrix(n):
        cos_l, sin_l = _dft_cos_sin(n)
        return _bf16_const(np.concatenate([cos_l, -sin_l], axis=1))[None]

    csl_x = pos_dft_matrix(seq)
    csl_c = pos_dft_matrix(clen)

    def token_mix_tail(proj, csl, yf, yb, row_off, lw):
        ls = proj.shape[1]
        ab = _chan_dft_call(proj, cs_chan, fw)
        ab = ab.reshape(nb, 2 * ls, fw)
        if "fourier_w" not in lw:
            f, lw["fourier_w"] = _mm_call(csl, ab, BF16, "pos_dft", rider=lw.pop("ride_fourier_w"))
        else:
            f = _mm_call(csl, ab, BF16, "pos_dft")
        f = f.reshape(1, nb * ls, fw)
        if "glu_w" not in lw:
            y_f, lw["glu_w"] = _mm_call(f, lw["fourier_w"][None], BF16, "fourier_out",
                                        rider=lw.pop("ride_glu_w"))
        else:
            y_f = _mm_call(f, lw["fourier_w"][None], BF16, "fourier_out")
        y_f = y_f.reshape(nb, ls, d)
        act = _ssm_act_call(proj, yf, yb, row_off, lw["ssm_d"], fw, sw)
        if "w_out" not in lw:
            merged, lw["w_out"] = _glu_call(act, proj, lw["glu_w"], y_f, fw, sw, d,
                                            rider=lw.pop("ride_w_out"))
        else:
            merged = _glu_call(act, proj, lw["glu_w"], y_f, fw, sw, d)
        merged = merged.reshape(1, nb * ls, d)
        if "ride_next_w_in" in lw:
            tn_ride = 512
            nsteps = (nb * ls // _tile(nb * ls, 1024)) * (d // _tile(d, tn_ride))
            mod_job = _mod_job(ada_w, lw["layer"] + 1, nsteps)
            mod_rider = None if mod_job is None else (mod_job, (cond, ada_w, ada_b))
            res = _mm_call(merged, lw["w_out"][None], BF16, "w_out", tn_pref=tn_ride,
                           rider=lw.pop("ride_next_w_in"), mod_rider=mod_rider)
            out, nxt_w["w_in"] = res[0], res[1]
            if mod_rider is not None:
                nxt_w["mods"] = res[2]
        else:
            out = _mm_call(merged, lw["w_out"][None], BF16, "w_out")
        return out.reshape(nb, ls, d)

    def conv_ffn(h, lw, period):
        if "ride_next_w12" in lw:
            act, nxt_w["w12"] = _ffn1_call(h, lw["w12"], lw["conv_w"], lw["conv_b"], period,
                                           rider=lw.pop("ride_next_w12"))
        else:
            act = _ffn1_call(h, lw["w12"], lw["conv_w"], lw["conv_b"], period)
        return _mm_call(act, lw["w2"][None], BF16, "ffn2", tm_pref=512, tn_pref=512)

    xs = x
    cs = ctx.reshape(1, nb * clen, d)
    nxt_w = {}
    hx = _ln_mod_call(xs, mods_list[0], None, 0, 1)
    hc = _ln_mod_call(cs, mods_list[0], ctx_row, 0, 1)
    for i in range(depth):
        last = i == depth - 1
        mods = mods_list[i]
        lw = {
            "layer": i,
            "ssm_d": ssm_d[i].reshape(1, sw),
            "ride_fourier_w": (_rows_cast(fourier_w, i, _tile(fw, CAST_ROWS)), fourier_w),
            "ride_glu_w": (_rows_cast(glu_w, i, _tile(sw, CAST_ROWS // 2)), glu_w),
            "ride_w_out": (_rows_cast(w_out, i, _tile(d, CAST_ROWS)), w_out),
        }
        cur_w, nxt_w = nxt_w, {}
        ride_w12 = ride_w2 = None
        if ffn_pad:
            if "w12" in cur_w:
                lw["w12"] = cur_w["w12"]
            else:
                ride_w12 = (_pad_halves_cast(ffn_w12, i, ffn_pad, HALVES_CAST_COLS), ffn_w12)
            ride_w2 = (_pad_rows_cast(ffn_w2, i, ffn_pad), ffn_w2)
            if not last:
                lw["ride_next_w_in"] = (_rows_cast(w_in, i + 1, _tile(d, CAST_ROWS // 4)), w_in)
                lw["ride_next_w12"] = (_pad_halves_cast(ffn_w12, i + 1, ffn_pad, HALVES_CAST_COLS),
                                       ffn_w12)
        else:
            lw["w12"] = _cast_call(ffn_w12, i)
            lw["w2"] = _cast_call(ffn_w2, i)
        lw["conv_w"] = jnp.pad(ffn_conv_w[i], ((0, 0), (0, ffn_pad)))
        lw["conv_b"] = jnp.pad(ffn_conv_b[i].reshape(1, ffn), ((0, 0), (0, ffn_pad)))
        g1, b1 = ln1_g[i].reshape(1, d), ln1_b[i].reshape(1, d)
        g2, b2 = ln2_g[i].reshape(1, d), ln2_b[i].reshape(1, d)
        bmat, cmat, amat = _ssm_tables(ssm_a_re[i], ssm_a_im[i], ssm_log_dt[i], ssm_b_re[i],
                                       ssm_b_im[i], ssm_c_re[i], ssm_c_im[i])

        n_in = w_in.shape[2]
        hc2d = hc.reshape(nb * clen, d)
        c_col = 0 if last else fw
        if "w_in" in cur_w:
            w_in_b = cur_w["w_in"]
            proj_x = _mm_call(hx.reshape(1, nb * seq, d), w_in_b[None], BF16, "w_in", rider=ride_w12)
            w_in_c = w_in_b[None, :, fw:fw + sw] if last else w_in_b[None]
            proj_c = _mm_call(hc2d[None], w_in_c, BF16, "w_in_ctx")
        else:
            proj_x = _mm_w32_call(hx.reshape(nb * seq, d), w_in, i, 0, n_in, BF16, "w_in", ride_w12)
            if last:
                proj_c = _mm_w32_call(hc2d, w_in, i, fw, sw, BF16, "w_in_ctx")
            else:
                proj_c = _mm_w32_call(hc2d, w_in, i, 0, n_in, BF16, "w_in_ctx")
        if ride_w12 is not None:
            proj_x, lw["w12"] = proj_x
        proj_x = proj_x.reshape(nb, seq, n_in)
        proj_c = proj_c.reshape(nb, clen, -1)
        ssm_out = _ssm_call(proj_x, fw, proj_c, c_col, bmat, cmat, amat, chunk, ride_w2)
        yf, yb = ssm_out[0], ssm_out[1]
        if ride_w2 is not None:
            lw["w2"] = ssm_out[2]

        out_x = token_mix_tail(proj_x, csl_x, yf, yb, 0, lw)
        if not last and len(mods_list) == i + 1:
            if "mods" in nxt_w:
                mods_list.append(nxt_w["mods"].reshape(n_rows, 1, N_MOD * d))
            else:
                mods_list.append(_adaln_call(cond, ada_w, ada_b, i + 1, 1).reshape(n_rows, 1, N_MOD * d))
        nxt = None if last else (mods_list[i + 1], 0, 1)
        x1, h2 = _postnorm_call(xs, out_x, mods, None, 2, g1, b1, alpha, (mods, 3, 4))
        ffn_x = conv_ffn(h2, lw, GRID_W)
        xs, hx = _postnorm_call(x1, ffn_x, mods, None, 5, g2, b2, alpha, nxt)

        if not last:
            out_c = token_mix_tail(proj_c, csl_c, yf, yb, seq, lw).reshape(1, nb * clen, d)
            c1, hc2 = _postnorm_call(cs, out_c, mods, ctx_row, 2, g1, b1, alpha, (mods, 3, 4))
            ffn_c = conv_ffn(hc2, lw, clen)
            cs, hc = _postnorm_call(c1, ffn_c, mods, ctx_row, 5, g2, b2, alpha, nxt)
    return xs
```

```python
import functools
import math
from typing import Callable, NamedTuple

import jax
import jax.numpy as jnp
import ml_dtypes
import numpy as np
from jax import lax
from jax.experimental import pallas as pl
from jax.experimental.pallas import tpu as pltpu

GRID_W = 64
FOURIER_GROUPS = 4
N_MOD = 6
LN_EPS = 1e-6
SSM_BLOCK_GROUPS = 16
LANES = 128
SUBLANES = 8
VMEM_LIMIT = 56 * 1024 * 1024
GLU_SUBTILE = 256
CAST_ROWS = 256
HALVES_CAST_COLS = 128
FFN_TILE = 512
SSM_MM_ROWS = 256

F32 = jnp.float32
BF16 = jnp.bfloat16


def _params(sem):
    return pltpu.CompilerParams(dimension_semantics=sem, vmem_limit_bytes=VMEM_LIMIT)


def _tile(n, pref):
    if n <= pref:
        return n
    while n % pref:
        pref //= 2
    assert pref >= SUBLANES, (n, pref)
    return pref


def _lane_tile(n, cap):
    for t in range(cap - cap % LANES, 0, -LANES):
        if n % t == 0:
            return t
    return n


def _dot(a, b):
    return jnp.dot(a, b, preferred_element_type=F32)


def _layer_norm(x):
    mu = jnp.mean(x, axis=-1, keepdims=True)
    xc = x - mu
    var = jnp.mean(xc * xc, axis=-1, keepdims=True)
    return xc * lax.rsqrt(var + LN_EPS)


def _mod_spec(mod_row, k, d):
    if mod_row is None:
        return pl.BlockSpec((None, 1, d), lambda b, *_: (b, 0, k))
    return pl.BlockSpec((None, 1, d), lambda b, *_: (mod_row, 0, k))


def _cast_kernel(w_ref, o_ref):
    o_ref[...] = w_ref[...].astype(o_ref.dtype)


def _cast_call(w, layer):
    _, r, c = w.shape
    tr = _tile(r, 512)
    tc = _lane_tile(c, 6144)
    return pl.pallas_call(
        _cast_kernel,
        grid=(r // tr, c // tc),
        in_specs=[pl.BlockSpec((None, tr, tc), lambda i, j: (layer, i, j))],
        out_specs=pl.BlockSpec((tr, tc), lambda i, j: (i, j)),
        out_shape=jax.ShapeDtypeStruct((r, c), BF16),
        compiler_params=_params(("arbitrary", "arbitrary")),
        name="cast_w",
    )(w)


class _PadCast(NamedTuple):
    layer: int
    in_block: tuple
    in_index: Callable
    out_block: tuple
    out_index: Callable
    out_shape: tuple
    nblk: int
    is_data: Callable


def _rows_cast(w, layer, tr):
    _, r, c = w.shape
    assert r % tr == 0
    return _PadCast(layer, (None, tr, c), lambda s: (s, 0), (tr, c), lambda s: (s, 0), (r, c),
                    r // tr, lambda s: s >= 0)


def _pad_rows_cast(w, layer, tr):
    _, r, c = w.shape
    nvalid = r // tr
    assert r % tr == 0
    return _PadCast(layer, (None, tr, c), lambda s: (jnp.minimum(s, nvalid - 1), 0),
                    (tr, c), lambda s: (s, 0), (r + tr, c), nvalid + 1, lambda s: s < nvalid)


def _pad_halves_cast(w, layer, pad, tc):
    _, r, c2 = w.shape
    f = c2 // 2
    nvalid = f // tc
    assert f % tc == 0 and pad % tc == 0
    per = nvalid + pad // tc
    return _PadCast(layer, (None, r, tc),
                    lambda s: (0, (s // per) * nvalid + jnp.minimum(s % per, nvalid - 1)),
                    (r, tc), lambda s: (0, s), (r, 2 * (f + pad)), 2 * per, lambda s: s % per < nvalid)


def _cast_specs(job, step_of):
    def clamped(*g):
        return jnp.minimum(step_of(*g), job.nblk - 1)

    in_spec = pl.BlockSpec(job.in_block, lambda *g: (job.layer,) + tuple(job.in_index(clamped(*g))))
    out_spec = pl.BlockSpec(job.out_block, lambda *g: tuple(job.out_index(clamped(*g))))
    return in_spec, out_spec


def _when_live(job, step, nsteps, work):
    if job.nblk == nsteps:
        work()
    else:
        pl.when(step < job.nblk)(work)


def _cast_step(job, step, w_ref, o_ref, nsteps=None):
    def work():
        o_ref[...] = jnp.where(job.is_data(step), w_ref[...], 0.0).astype(o_ref.dtype)

    _when_live(job, step, nsteps, work)


def _rider_fits(rider, nsteps):
    return rider[0].nblk <= nsteps


def _attach_rider(rider, step_of, in_specs, out_specs, out_shape, args):
    job, rw = rider
    r_in, r_out = _cast_specs(job, step_of)
    in_specs.append(r_in)
    out_specs.append(r_out)
    out_shape.append(jax.ShapeDtypeStruct(job.out_shape, BF16))
    args.append(rw)
    return job


def _pad_cast_kernel(w_ref, o_ref, *, job):
    _cast_step(job, pl.program_id(0), w_ref, o_ref)


def _pad_cast_call(job, w):
    in_spec, out_spec = _cast_specs(job, lambda s: s)
    return pl.pallas_call(
        functools.partial(_pad_cast_kernel, job=job),
        grid=(job.nblk,),
        in_specs=[in_spec],
        out_specs=out_spec,
        out_shape=jax.ShapeDtypeStruct(job.out_shape, BF16),
        compiler_params=_params(("arbitrary",)),
        name="cast_w_pad",
    )(w)


def _adaln_kernel(c_ref, w_ref, b_ref, o_ref):
    c = c_ref[...]
    s = (c * jax.nn.sigmoid(c)).astype(BF16)
    o_ref[...] = _dot(s, w_ref[...].astype(BF16)) + b_ref[...]


class _ModJob(NamedTuple):
    layer: int
    tcol: int
    nblk: int


def _mod_job(ada_w, layer, nsteps):
    n = ada_w.shape[2]
    for tcol in range(LANES, n + 1, LANES):
        if n % tcol == 0 and n // tcol <= nsteps:
            return _ModJob(layer, tcol, n // tcol)
    return None


def _mod_attach(job, operands, step_of, in_specs, out_specs, out_shape, args):
    cond, ada_w, ada_b = operands
    r, d = cond.shape

    def clamped(*g):
        return jnp.minimum(step_of(*g), job.nblk - 1)

    in_specs += [
        pl.BlockSpec((r, d), lambda *g: (0, 0)),
        pl.BlockSpec((None, d, job.tcol), lambda *g: (job.layer, 0, clamped(*g))),
        pl.BlockSpec((None, 1, job.tcol), lambda *g: (job.layer, 0, clamped(*g))),
    ]
    out_specs.append(pl.BlockSpec((r, job.tcol), lambda *g: (0, clamped(*g))))
    out_shape.append(jax.ShapeDtypeStruct((r, ada_w.shape[2]), F32))
    args += [cond, ada_w, ada_b.reshape(ada_b.shape[0], 1, -1)]


def _mod_step(job, step, c_ref, w_ref, b_ref, o_ref, nsteps=None):
    _when_live(job, step, nsteps, functools.partial(_adaln_kernel, c_ref, w_ref, b_ref, o_ref))


def _adaln_call(cond, ada_w, ada_b, layer_lo=0, nlayer=None):
    all_layers, d, n = ada_w.shape
    depth = all_layers - layer_lo if nlayer is None else nlayer
    r = cond.shape[0]
    tn = _tile(n, 512)
    return pl.pallas_call(
        _adaln_kernel,
        grid=(depth, n // tn),
        in_specs=[
            pl.BlockSpec((r, d), lambda l, j: (0, 0)),
            pl.BlockSpec((None, d, tn), lambda l, j: (layer_lo + l, 0, j)),
            pl.BlockSpec((None, 1, tn), lambda l, j: (layer_lo + l, 0, j)),
        ],
        out_specs=pl.BlockSpec((None, r, tn), lambda l, j: (l, 0, j)),
        out_shape=jax.ShapeDtypeStruct((depth, r, n), F32),
        compiler_params=_params(("arbitrary", "arbitrary")),
        name="adaln",
    )(cond, ada_w, ada_b.reshape(all_layers, 1, n))


def _ln_mod_kernel(x_ref, sh_ref, sc_ref, o_ref):
    y = _layer_norm(x_ref[...])
    o_ref[...] = (y * (1.0 + sc_ref[...]) + sh_ref[...]).astype(o_ref.dtype)


def _ln_mod_call(x, mods, mod_row, k_shift, k_scale):
    bm, lm, d = x.shape
    tm = _tile(lm, 256)
    return pl.pallas_call(
        _ln_mod_kernel,
        grid=(bm, lm // tm),
        in_specs=[
            pl.BlockSpec((None, tm, d), lambda b, i: (b, i, 0)),
            _mod_spec(mod_row, k_shift, d),
            _mod_spec(mod_row, k_scale, d),
        ],
        out_specs=pl.BlockSpec((None, tm, d), lambda b, i: (b, i, 0)),
        out_shape=jax.ShapeDtypeStruct((bm, lm, d), BF16),
        compiler_params=_params(("arbitrary", "arbitrary")),
        name="ln_mod",
    )(x, mods, mods)


def _postnorm_kernel(res_ref, pre_ref, gate_ref, g_ref, b_ref, *rest, alpha, with_next):
    z = alpha * res_ref[...] + gate_ref[...] * pre_ref[...].astype(F32)
    x_new = _layer_norm(z) * g_ref[...] + b_ref[...]
    if with_next:
        sh_ref, sc_ref, o_ref, h_ref = rest
        o_ref[...] = x_new
        h_ref[...] = (_layer_norm(x_new) * (1.0 + sc_ref[...]) + sh_ref[...]).astype(h_ref.dtype)
    else:
        (o_ref,) = rest
        o_ref[...] = x_new


def _postnorm_call(res, pre, mods, mod_row, k_gate, ln_g, ln_b, alpha, next_mod=None):
    bm, lm, d = res.shape
    tm = _tile(lm, 256)
    row = pl.BlockSpec((None, tm, d), lambda b, i: (b, i, 0))
    vec = pl.BlockSpec((1, d), lambda b, i: (0, 0))
    in_specs = [row, row, _mod_spec(mod_row, k_gate, d), vec, vec]
    args = [res, pre, mods, ln_g, ln_b]
    out_specs = [row]
    out_shape = [jax.ShapeDtypeStruct((bm, lm, d), F32)]
    if next_mod is not None:
        nmods, k_shift, k_scale = next_mod
        in_specs += [_mod_spec(mod_row, k_shift, d), _mod_spec(mod_row, k_scale, d)]
        args += [nmods, nmods]
        out_specs.append(row)
        out_shape.append(jax.ShapeDtypeStruct((bm, lm, d), BF16))
    out = pl.pallas_call(
        functools.partial(_postnorm_kernel, alpha=alpha, with_next=next_mod is not None),
        grid=(bm, lm // tm),
        in_specs=in_specs,
        out_specs=out_specs,
        out_shape=out_shape,
        compiler_params=_params(("arbitrary", "arbitrary")),
        name="postnorm",
    )(*args)
    return (out[0], out[1]) if next_mod is not None else (out[0], None)


def _mm_kernel(a_ref, w_ref, *rest, nb, ni, nj, job, mod_job):
    n_side_in = (job is not None) + 3 * (mod_job is not None)
    side_in, (o_ref, *side_out) = rest[:n_side_in], rest[n_side_in:]
    step = (pl.program_id(0) * ni + pl.program_id(1)) * nj + pl.program_id(2)
    nsteps = nb * ni * nj
    if job is not None:
        _cast_step(job, step, side_in[0], side_out[0], nsteps)
    if mod_job is not None:
        _mod_step(mod_job, step, *side_in[-3:], side_out[-1], nsteps)
    o_ref[...] = _dot(a_ref[...], w_ref[...]).astype(o_ref.dtype)


def _mm_call(a, w, out_dtype, name, tm_pref=1024, tn_pref=1024, rider=None, mod_rider=None):
    ba, m, k = a.shape
    bw, _, n = w.shape
    nb = max(ba, bw)
    tm = _tile(m, tm_pref)
    tn = _tile(n, tn_pref)
    ni, nj = m // tm, n // tn
    if rider is not None and not _rider_fits(rider, nb * ni * nj):
        out = _mm_call(a, w, out_dtype, name, tm_pref, tn_pref, mod_rider=mod_rider)
        out = out if mod_rider is not None else (out,)
        return (out[0], _pad_cast_call(*rider)) + tuple(out[1:])
    rows_outer = ba == 1 and bw > 1
    n0, n1 = (ni, nb) if rows_outer else (nb, ni)

    def in_grid_order(f):
        return (lambda i, b, j: f(b, i, j)) if rows_outer else f

    a_map = (lambda b, i, j: (b, i, 0)) if ba > 1 else (lambda b, i, j: (0, i, 0))
    w_map = (lambda b, i, j: (b, 0, j)) if bw > 1 else (lambda b, i, j: (0, 0, j))
    in_specs = [pl.BlockSpec((None, tm, k), in_grid_order(a_map)),
                pl.BlockSpec((None, k, tn), in_grid_order(w_map))]
    out_specs = [pl.BlockSpec((None, tm, tn), in_grid_order(lambda b, i, j: (b, i, j)))]
    out_shape = [jax.ShapeDtypeStruct((nb, m, n), out_dtype)]
    args = [a, w]
    job = mod_job = None

    def step_of(p0, p1, j):
        return (p0 * n1 + p1) * nj + j

    if rider is not None:
        job = _attach_rider(rider, step_of, in_specs, out_specs, out_shape, args)
    if mod_rider is not None:
        mod_job, operands = mod_rider
        assert mod_job.nblk <= nb * ni * nj
        _mod_attach(mod_job, operands, step_of, in_specs, out_specs, out_shape, args)
    out = pl.pallas_call(
        functools.partial(_mm_kernel, nb=n0, ni=n1, nj=nj, job=job, mod_job=mod_job),
        grid=(n0, n1, nj),
        in_specs=in_specs,
        out_specs=out_specs,
        out_shape=out_shape,
        compiler_params=_params(("arbitrary", "arbitrary", "arbitrary")),
        name=name,
    )(*args)
    return out if len(out) > 1 else out[0]


def _mm_split_kernel(a1_ref, a2_ref, w1_ref, w2_ref, o_ref):
    acc = _dot(a1_ref[...], w1_ref[...]) + _dot(a2_ref[...], w2_ref[...])
    o_ref[...] = acc.astype(o_ref.dtype)


def _mm_split_call(a1, a2, w, out_dtype, name, tm_pref, tn_pref):
    nb, m, k1 = a1.shape
    k2 = a2.shape[2]
    n = w.shape[1]
    assert w.shape[0] == k1 + k2 and k1 % k2 == 0
    tm = _tile(m, tm_pref)
    tn = _tile(n, tn_pref)
    return pl.pallas_call(
        _mm_split_kernel,
        grid=(nb, m // tm, n // tn),
        in_specs=[
            pl.BlockSpec((None, tm, k1), lambda b, i, j: (b, i, 0)),
            pl.BlockSpec((None, tm, k2), lambda b, i, j: (b, i, 0)),
            pl.BlockSpec((k1, tn), lambda b, i, j: (0, j)),
            pl.BlockSpec((k2, tn), lambda b, i, j: (k1 // k2, j)),
        ],
        out_specs=pl.BlockSpec((None, tm, tn), lambda b, i, j: (b, i, j)),
        out_shape=jax.ShapeDtypeStruct((nb, m, n), out_dtype),
        compiler_params=_params(("arbitrary", "arbitrary", "arbitrary")),
        name=name,
    )(a1, a2, w, w)


def _mm_w32_kernel(a_ref, w_ref, *rest, ni, job):
    if job is None:
        o_ref, w_scr = rest
    else:
        rw_ref, o_ref, ro_ref, w_scr = rest
    i = pl.program_id(1)

    @pl.when(i == 0)
    def _():
        w_scr[...] = w_ref[...].astype(w_scr.dtype)

    if job is not None:
        _cast_step(job, pl.program_id(0) * ni + i, rw_ref, ro_ref)
    o_ref[...] = _dot(a_ref[...], w_scr[...]).astype(o_ref.dtype)


def _mm_w32_call(a, w, layer, col0, ncols, out_dtype, name, rider=None):
    m, k = a.shape
    tm = _tile(m, 1024)
    tn = _tile(ncols, 512)
    assert col0 % tn == 0
    j0 = col0 // tn
    ni = m // tm
    in_specs = [
        pl.BlockSpec((tm, k), lambda j, i: (i, 0)),
        pl.BlockSpec((None, k, tn), lambda j, i: (layer, 0, j0 + j)),
    ]
    out_specs = [pl.BlockSpec((tm, tn), lambda j, i: (i, j))]
    out_shape = [jax.ShapeDtypeStruct((m, ncols), out_dtype)]
    args = [a, w]
    job = None
    if rider is not None and not _rider_fits(rider, (ncols // tn) * ni):
        return _mm_w32_call(a, w, layer, col0, ncols, out_dtype, name), _pad_cast_call(*rider)
    if rider is not None:
        job = _attach_rider(rider, lambda j, i: j * ni + i, in_specs, out_specs, out_shape, args)
    out = pl.pallas_call(
        functools.partial(_mm_w32_kernel, ni=ni, job=job),
        grid=(ncols // tn, ni),
        in_specs=in_specs,
        out_specs=out_specs,
        out_shape=out_shape,
        scratch_shapes=[pltpu.VMEM((k, tn), BF16)],
        compiler_params=_params(("arbitrary", "arbitrary")),
        name=name,
    )(*args)
    return out if rider is not None else out[0]


def _chan_dft_kernel(u_ref, cs_ref, o_ref, *, gd):
    r = _dot(u_ref[...], cs_ref[...])
    o_ref[0] = r[:, :gd].astype(o_ref.dtype)
    o_ref[1] = r[:, gd:].astype(o_ref.dtype)


def _chan_dft_call(proj, cs, fw):
    bs, ls, _ = proj.shape
    gd = fw // FOURIER_GROUPS
    tm = _tile(ls, 1024)
    return pl.pallas_call(
        functools.partial(_chan_dft_kernel, gd=gd),
        grid=(bs, ls // tm, FOURIER_GROUPS),
        in_specs=[
            pl.BlockSpec((None, tm, gd), lambda b, i, g: (b, i, g)),
            pl.BlockSpec((gd, 2 * gd), lambda b, i, g: (0, 0)),
        ],
        out_specs=pl.BlockSpec((None, 2, tm, gd), lambda b, i, g: (b, 0, i, g)),
        out_shape=jax.ShapeDtypeStruct((bs, 2, ls, fw), BF16),
        compiler_params=_params(("arbitrary", "arbitrary", "arbitrary")),
        name="chan_dft",
    )(proj, cs)


def _ssm_kernel(uxf_ref, uxb_ref, ucf_ref, ucb_ref, bm_ref, cm_ref, a_ref, *rest,
                chunk, nbat, half, n_ctx, n_all, job):
    side_work = None
    if job is None:
        yf_ref, yb_ref, *scratch = rest
    else:
        rw_ref, yf_ref, yb_ref, ro_ref, *scratch = rest
        step = pl.program_id(0) * n_all + pl.program_id(1)
        side_work = functools.partial(_cast_step, job, step, rw_ref, ro_ref)
    _ssm_step(uxf_ref, uxb_ref, ucf_ref, ucb_ref, bm_ref, cm_ref, a_ref, yf_ref, yb_ref,
              *scratch, chunk=chunk, nbat=nbat, half=half, n_ctx=n_ctx, side_work=side_work)


def _ssm_step(uxf_ref, uxb_ref, ucf_ref, ucb_ref, bm_ref, cm_ref, a_ref, yf_ref, yb_ref,
              u_scr, uin_scr, s_scr, y_scr, h_scr, *, chunk, nbat, half, n_ctx, side_work):
    k = pl.program_id(0)
    c = pl.program_id(1)
    rows = SUBLANES
    hw = half // 2
    mm_rows = min(chunk * rows, SSM_MM_ROWS)
    fwd_starts = list(range(0, chunk * rows, mm_rows))
    row_starts = (fwd_starts, fwd_starts[::-1])

    @pl.when(jnp.logical_and(k == 0, c == 0))
    def _():
        uin_scr[...] = jnp.zeros_like(uin_scr)

    @pl.when(c == 0)
    def _():
        h_scr[...] = jnp.zeros_like(h_scr)

    @pl.when(c < n_ctx)
    def _():
        u_scr[0] = ucf_ref[...]
        u_scr[1] = ucb_ref[...]

    @pl.when(c >= n_ctx)
    def _():
        u_scr[0] = uxf_ref[...]
        u_scr[1] = uxb_ref[...]

    if side_work is not None:
        side_work()
    for d in range(2):
        for b in range(nbat):
            ub = u_scr[d, b].astype(F32)
            for h in range(2):
                uin_scr[d, h, pl.ds(2 * b + h, chunk, stride=rows), :] = ub[:, h * LANES:(h + 1) * LANES]
        bm = bm_ref[d]
        for r0 in row_starts[d]:
            rs = slice(r0, r0 + mm_rows)
            lhs = jnp.concatenate([uin_scr[d, 0, rs], uin_scr[d, 1, rs]], axis=1).astype(BF16)
            s_scr[d, rs] = _dot(lhs, bm)

    arf, aif = a_ref[0, 0], a_ref[0, 1]
    arb, aib = a_ref[1, 0], a_ref[1, 1]

    def body(t, carry):
        hrf, hif, hrb, hib = carry
        rowf = t * rows
        rowb = (chunk - 1 - t) * rows
        xf = s_scr[0, pl.ds(rowf, rows), :]
        xb = s_scr[1, pl.ds(rowb, rows), :]
        nrf = arf * hrf - aif * hif + xf[:, :hw]
        nif = arf * hif + aif * hrf + xf[:, hw:]
        nrb = arb * hrb - aib * hib + xb[:, :hw]
        nib = arb * hib + aib * hrb + xb[:, hw:]
        s_scr[0, pl.ds(rowf, rows), :hw] = nrf
        s_scr[0, pl.ds(rowf, rows), hw:] = nif
        s_scr[1, pl.ds(rowb, rows), :hw] = nrb
        s_scr[1, pl.ds(rowb, rows), hw:] = nib
        return nrf, nif, nrb, nib

    init = (h_scr[0, :, :hw], h_scr[0, :, hw:], h_scr[1, :, :hw], h_scr[1, :, hw:])
    carry = init
    for t in range(chunk):
        carry = body(t, carry)
    hrf, hif, hrb, hib = carry
    h_scr[0, :, :hw] = hrf
    h_scr[0, :, hw:] = hif
    h_scr[1, :, :hw] = hrb
    h_scr[1, :, hw:] = hib

    for d, y_ref in enumerate((yf_ref, yb_ref)):
        cm = cm_ref[d]
        for r0 in row_starts[d]:
            rs = slice(r0, r0 + mm_rows)
            out = _dot(s_scr[d, rs].astype(BF16), cm)
            y_scr[d, 0, rs] = out[:, :LANES]
            y_scr[d, 1, rs] = out[:, LANES:]
        for b in range(nbat):
            for h in range(2):
                piece = y_scr[d, h, pl.ds(2 * b + h, chunk, stride=rows), :]
                y_ref[b, :, h * LANES:(h + 1) * LANES] = piece.astype(y_ref.dtype)


def _ssm_call(ux, ux_col, uc, uc_col, bmat, cmat, amat, chunk, rider=None):
    nbat, seq, _ = ux.shape
    clen = uc.shape[1]
    assert 2 * nbat == SUBLANES, "the scan packs 2 column halves x batch on the 8 sublanes"
    ndir, nblk, cin, cst = bmat.shape
    assert cin == 2 * LANES, "each column half of a group block is one 128-lane input slab"
    sw = nblk * cin
    n_lat, n_ctx = seq // chunk, clen // chunk
    n_all = n_lat + n_ctx
    assert ux_col % cin == 0 and uc_col % cin == 0
    xo, co = ux_col // cin, uc_col // cin

    def lat_f(c):
        return jnp.maximum(c - n_ctx, 0)

    def lat_b(c):
        return n_lat - 1 - jnp.maximum(c - n_ctx, 0)

    def ctx_f(c):
        return jnp.minimum(c, n_ctx - 1)

    def ctx_b(c):
        return jnp.maximum(n_ctx - 1 - c, 0)

    def out_f(c):
        return jnp.where(c < n_ctx, n_lat + c, c - n_ctx)

    def out_b(c):
        return jnp.where(c < n_ctx, n_lat + n_ctx - 1 - c, n_lat - 1 - (c - n_ctx))

    ublk = (nbat, chunk, cin)
    in_specs = [
        pl.BlockSpec(ublk, lambda k, c: (0, lat_f(c), xo + k)),
        pl.BlockSpec(ublk, lambda k, c: (0, lat_b(c), xo + k)),
        pl.BlockSpec(ublk, lambda k, c: (0, ctx_f(c), co + k)),
        pl.BlockSpec(ublk, lambda k, c: (0, ctx_b(c), co + k)),
        pl.BlockSpec((ndir, None, cin, cst), lambda k, c: (0, k, 0, 0)),
        pl.BlockSpec((ndir, None, cst, cin), lambda k, c: (0, k, 0, 0)),
        pl.BlockSpec((ndir, None, 2, SUBLANES, cst // 2), lambda k, c: (0, k, 0, 0, 0)),
    ]
    out_specs = [
        pl.BlockSpec(ublk, lambda k, c: (0, out_f(c), k)),
        pl.BlockSpec(ublk, lambda k, c: (0, out_b(c), k)),
    ]
    out_shape = [jax.ShapeDtypeStruct((nbat, seq + clen, sw), BF16)] * 2
    args = [ux, ux, uc, uc, bmat, cmat, amat]
    job = None
    if rider is not None and not _rider_fits(rider, nblk * n_all):
        yf, yb = _ssm_call(ux, ux_col, uc, uc_col, bmat, cmat, amat, chunk)
        return yf, yb, _pad_cast_call(*rider)
    if rider is not None:
        job = _attach_rider(rider, lambda k, c: k * n_all + c, in_specs, out_specs, out_shape, args)
    return pl.pallas_call(
        functools.partial(_ssm_kernel, chunk=chunk, nbat=nbat, half=cst, n_ctx=n_ctx,
                          n_all=n_all, job=job),
        grid=(nblk, n_all),
        in_specs=in_specs,
        out_specs=out_specs,
        out_shape=out_shape,
        scratch_shapes=[
            pltpu.VMEM((ndir, nbat, chunk, cin), BF16),
            pltpu.VMEM((ndir, 2, SUBLANES * chunk, LANES), F32),
            pltpu.VMEM((ndir, SUBLANES * chunk, cst), F32),
            pltpu.VMEM((ndir, 2, SUBLANES * chunk, LANES), F32),
            pltpu.VMEM((ndir, SUBLANES, cst), F32),
        ],
        compiler_params=_params(("arbitrary", "arbitrary")),
        name="ssm",
    )(*args)


def _ssm_tables(a_re, a_im, log_dt, b_re, b_im, c_re, c_im):
    ndir, g, p = a_re.shape
    cg = b_re.shape[-1]
    gb = SSM_BLOCK_GROUPS
    nblk = g // gb
    hg = gb // 2
    npair = hg // 2
    assert 2 * p == LANES and g % gb == 0 and hg * cg == LANES
    lr, li = a_re.astype(F32), a_im.astype(F32)
    dt = jnp.exp(log_dt.astype(F32))[..., None]
    zr, zi = lr * dt, li * dt
    a_bar_r = jnp.exp(zr) * jnp.cos(zi)
    a_bar_i = jnp.exp(zr) * jnp.sin(zi)
    em1_r = jnp.expm1(zr) * jnp.cos(zi) - 2.0 * jnp.square(jnp.sin(0.5 * zi))
    den = lr * lr + li * li
    cf_r = ((em1_r * lr + a_bar_i * li) / den)[..., None]
    cf_i = ((a_bar_i * lr - em1_r * li) / den)[..., None]
    br, bi = b_re.astype(F32), b_im.astype(F32)
    b_bar_r = cf_r * br - cf_i * bi
    b_bar_i = cf_r * bi + cf_i * br

    lane_q = np.arange(LANES) // p
    own = (np.arange(gb)[:, None, None] % hg
           == 2 * np.arange(npair)[None, :, None] + lane_q[None, None, :])
    own_in = np.repeat(own, cg, axis=0).astype(np.float32)
    own_in = own_in.reshape(gb * cg, npair * LANES)
    own_out = np.ascontiguousarray(own_in.T)

    def in_mat(bb):
        t = jnp.swapaxes(bb.reshape(ndir, nblk, gb, p, cg), -1, -2).reshape(ndir, nblk, gb * cg, p)
        return jnp.tile(t, (1, 1, 1, 2 * npair)) * own_in

    def out_mat(cc):
        t = jnp.swapaxes(cc.reshape(ndir, nblk, gb * cg, p), -1, -2)
        return jnp.tile(t, (1, 1, 2 * npair, 1)) * own_out

    bmat = jnp.concatenate([in_mat(b_bar_r), in_mat(b_bar_i)], axis=-1).astype(BF16)
    cmat = jnp.concatenate([out_mat(c_re.astype(F32)), out_mat(-c_im.astype(F32))], axis=-2).astype(BF16)

    def decay(a):
        t = a.reshape(ndir, nblk, 2, npair * LANES)
        return jnp.tile(t, (1, 1, SUBLANES // 2, 1))
    aa = jnp.stack([decay(a_bar_r), decay(a_bar_i)], axis=2)
    return bmat, cmat, aa


def _ssm_act_kernel(u_ref, yf_ref, yb_ref, d_ref, o_ref):
    y = d_ref[...] * u_ref[...].astype(F32) + yf_ref[...].astype(F32) + yb_ref[...].astype(F32)
    o_ref[...] = jax.nn.gelu(y).astype(o_ref.dtype)


def _ssm_act_call(proj, yf, yb, row_off, ssm_d, fw, sw):
    bs, ls, _ = proj.shape
    tm = _tile(ls, 512)
    assert fw % sw == 0 and row_off % tm == 0
    u_blk, r_blk = fw // sw, row_off // tm
    return pl.pallas_call(
        _ssm_act_kernel,
        grid=(bs, ls // tm),
        in_specs=[
            pl.BlockSpec((None, tm, sw), lambda b, i: (b, i, u_blk)),
            pl.BlockSpec((None, tm, sw), lambda b, i: (b, i + r_blk, 0)),
            pl.BlockSpec((None, tm, sw), lambda b, i: (b, i + r_blk, 0)),
            pl.BlockSpec((1, sw), lambda b, i: (0, 0)),
        ],
        out_specs=pl.BlockSpec((None, tm, sw), lambda b, i: (b, i, 0)),
        out_shape=jax.ShapeDtypeStruct((bs, ls, sw), BF16),
        compiler_params=_params(("arbitrary", "arbitrary")),
        name="ssm_act",
    )(proj, yf, yb, ssm_d)


def _glu_kernel(a_ref, wv_ref, wg_ref, gf_ref, gs_ref, fo_ref, *rest, nb, ni, job):
    if job is None:
        (o_ref,) = rest
    else:
        rw_ref, o_ref, ro_ref = rest
        step = (pl.program_id(0) * nb + pl.program_id(1)) * ni + pl.program_id(2)
        _cast_step(job, step, rw_ref, ro_ref)
    a = a_ref[...]
    tn = o_ref.shape[1]
    sub = min(tn, GLU_SUBTILE)
    for n0 in range(0, tn, sub):
        cols = slice(n0, n0 + sub)
        y_s = (_dot(a, wv_ref[:, cols]) * jax.nn.sigmoid(_dot(a, wg_ref[:, cols]))).astype(BF16)
        o_ref[:, cols] = (jax.nn.sigmoid(gf_ref[:, cols]) * fo_ref[:, cols]
                          + jax.nn.sigmoid(gs_ref[:, cols]) * y_s)


def _glu_call(act, proj, glu_w, y_f, fw, sw, d, rider=None):
    bs, ls, _ = act.shape
    tm = _tile(ls, 512)
    tn = _tile(d, 1024)
    assert (fw + sw) % tn == 0
    gf_blk = (fw + sw) // tn
    gs_blk = (fw + sw + d) // tn
    ni, nj = ls // tm, d // tn
    if rider is not None and not _rider_fits(rider, bs * ni * nj):
        return _glu_call(act, proj, glu_w, y_f, fw, sw, d), _pad_cast_call(*rider)
    in_specs = [
        pl.BlockSpec((None, tm, sw), lambda j, b, i: (b, i, 0)),
        pl.BlockSpec((sw, tn), lambda j, b, i: (0, j)),
        pl.BlockSpec((sw, tn), lambda j, b, i: (0, nj + j)),
        pl.BlockSpec((None, tm, tn), lambda j, b, i: (b, i, gf_blk + j)),
        pl.BlockSpec((None, tm, tn), lambda j, b, i: (b, i, gs_blk + j)),
        pl.BlockSpec((None, tm, tn), lambda j, b, i: (b, i, j)),
    ]
    out_specs = [pl.BlockSpec((None, tm, tn), lambda j, b, i: (b, i, j))]
    out_shape = [jax.ShapeDtypeStruct((bs, ls, d), BF16)]
    args = [act, glu_w, glu_w, proj, proj, y_f]
    job = None
    if rider is not None:
        job = _attach_rider(rider, lambda j, b, i: (j * bs + b) * ni + i,
                            in_specs, out_specs, out_shape, args)
    out = pl.pallas_call(
        functools.partial(_glu_kernel, nb=bs, ni=ni, job=job),
        grid=(nj, bs, ni),
        in_specs=in_specs,
        out_specs=out_specs,
        out_shape=out_shape,
        compiler_params=_params(("arbitrary", "arbitrary", "arbitrary")),
        name="glu_merge",
    )(*args)
    return out if rider is not None else out[0]


def _ffn1_kernel(h_ref, wu_ref, wv_ref, cw_ref, cb_ref, *rest, period, nb, ni, nj, job):
    if job is None:
        (o_ref,) = rest
    else:
        rw_ref, o_ref, ro_ref = rest
        step = (pl.program_id(0) * ni + pl.program_id(1)) * nj + pl.program_id(2)
        _cast_step(job, step, rw_ref, ro_ref, nb * ni * nj)
    h = h_ref[...]
    u = _dot(h, wu_ref[...])
    v = _dot(h, wv_ref[...])
    tm = u.shape[0]
    pos = lax.broadcasted_iota(jnp.int32, u.shape, 0) % period
    prev = jnp.where(pos == 0, 0.0, pltpu.roll(u, 1, axis=0))
    nxt = jnp.where(pos == period - 1, 0.0, pltpu.roll(u, tm - 1, axis=0))
    cw = cw_ref[...]
    conv = cb_ref[...] + cw[0:1] * prev + cw[1:2] * u + cw[2:3] * nxt
    o_ref[...] = (jax.nn.gelu(conv) * v).astype(o_ref.dtype)


def _ffn1_call(h, w12, conv_w, conv_b, period, rider=None, tn=None, j0=0, nj=None):
    bm, lm, d = h.shape
    ffn = w12.shape[1] // 2
    tm = _tile(lm, 1024)
    tn = _tile(ffn, 512) if tn is None else tn
    nj = ffn // tn if nj is None else nj
    ni = lm // tm
    v0 = ffn // tn + j0
    assert tm % period == 0 and ffn % tn == 0
    if rider is not None and not _rider_fits(rider, bm * ni * nj):
        return _ffn1_call(h, w12, conv_w, conv_b, period, None, tn, j0, nj), _pad_cast_call(*rider)
    in_specs = [
        pl.BlockSpec((None, tm, d), lambda b, i, j: (b, i, 0)),
        pl.BlockSpec((d, tn), lambda b, i, j: (0, j0 + j)),
        pl.BlockSpec((d, tn), lambda b, i, j: (0, v0 + j)),
        pl.BlockSpec((3, tn), lambda b, i, j: (0, j0 + j)),
        pl.BlockSpec((1, tn), lambda b, i, j: (0, j0 + j)),
    ]
    out_specs = [pl.BlockSpec((None, tm, tn), lambda b, i, j: (b, i, j))]
    out_shape = [jax.ShapeDtypeStruct((bm, lm, nj * tn), BF16)]
    args = [h, w12, w12, conv_w, conv_b]
    job = None
    if rider is not None:
        job = _attach_rider(rider, lambda b, i, j: (b * ni + i) * nj + j,
                            in_specs, out_specs, out_shape, args)
    out = pl.pallas_call(
        functools.partial(_ffn1_kernel, period=period, nb=bm, ni=ni, nj=nj, job=job),
        grid=(bm, ni, nj),
        in_specs=in_specs,
        out_specs=out_specs,
        out_shape=out_shape,
        compiler_params=_params(("arbitrary", "arbitrary", "arbitrary")),
        name="ffn1",
    )(*args)
    return out if rider is not None else out[0]


def _dft_cos_sin(n):
    idx = np.arange(n, dtype=np.int64)
    ang = (2.0 * math.pi / n) * ((idx[:, None] * idx[None, :]) % n)
    scale = 1.0 / math.sqrt(n)
    return np.cos(ang) * scale, np.sin(ang) * scale


def _bf16_const(a):
    return jnp.asarray(np.asarray(a, np.float32).astype(ml_dtypes.bfloat16))


def kernel(x, c, ctx, c_ctx, ada_w, ada_b, w_in, fourier_w, ssm_a_re, ssm_a_im, ssm_log_dt,
           ssm_b_re, ssm_b_im, ssm_c_re, ssm_c_im, ssm_d, glu_w, w_out, ln1_g, ln1_b,
           ffn_w12, ffn_conv_w, ffn_conv_b, ffn_w2, ln2_g, ln2_b):
    nb, seq, d = x.shape
    clen = ctx.shape[1]
    depth = ada_w.shape[0]
    fw = fourier_w.shape[1]
    sw = ssm_d.shape[1]
    ffn = ffn_conv_b.shape[1]
    alpha = (2.0 * depth) ** 0.25
    gd = fw // FOURIER_GROUPS
    chunk = clen
    assert seq % chunk == 0 and seq % GRID_W == 0
    ffn_pad = FFN_TILE // 2 if (ffn % FFN_TILE == FFN_TILE // 2) else 0

    ctx_row = nb
    n_rows = -(-(nb + 1) // SUBLANES) * SUBLANES
    cond = jnp.zeros((n_rows, d), F32).at[:nb].set(c).at[ctx_row].set(c_ctx)
    ride_mods = ffn_pad > 0
    first_mods = _adaln_call(cond, ada_w, ada_b, 0, 1 if ride_mods else depth)
    mods_list = [m.reshape(n_rows, 1, N_MOD * d) for m in first_mods]

    cos_c, sin_c = _dft_cos_sin(gd)
    cs_chan = _bf16_const(np.concatenate([cos_c, sin_c], axis=1))

    def pos_dft_matrix(n):
        cos_l, sin_l = _dft_cos_sin(n)
        return _bf16_const(np.concatenate([cos_l, -sin_l], axis=1))[None]

    csl_x = pos_dft_matrix(seq)
    csl_c = pos_dft_matrix(clen)

    def token_mix_tail(proj, csl, yf, yb, row_off, lw):
        ls = proj.shape[1]
        ab = _chan_dft_call(proj, cs_chan, fw)
        ab = ab.reshape(nb, 2 * ls, fw)
        if "fourier_w" not in lw:
            f, lw["fourier_w"] = _mm_call(csl, ab, BF16, "pos_dft", rider=lw.pop("ride_fourier_w"))
        else:
            f = _mm_call(csl, ab, BF16, "pos_dft")
        f = f.reshape(1, nb * ls, fw)
        if "glu_w" not in lw:
            y_f, lw["glu_w"] = _mm_call(f, lw["fourier_w"][None], BF16, "fourier_out",
                                        rider=lw.pop("ride_glu_w"))
        else:
            y_f = _mm_call(f, lw["fourier_w"][None], BF16, "fourier_out")
        y_f = y_f.reshape(nb, ls, d)
        act = _ssm_act_call(proj, yf, yb, row_off, lw["ssm_d"], fw, sw)
        if "w_out" not in lw:
            merged, lw["w_out"] = _glu_call(act, proj, lw["glu_w"], y_f, fw, sw, d,
                                            rider=lw.pop("ride_w_out"))
        else:
            merged = _glu_call(act, proj, lw["glu_w"], y_f, fw, sw, d)
        merged = merged.reshape(1, nb * ls, d)
        if "ride_next_w_in" in lw:
            tn_ride = 512
            nsteps = (nb * ls // _tile(nb * ls, 1024)) * (d // _tile(d, tn_ride))
            mod_job = _mod_job(ada_w, lw["layer"] + 1, nsteps)
            mod_rider = None if mod_job is None else (mod_job, (cond, ada_w, ada_b))
            res = _mm_call(merged, lw["w_out"][None], BF16, "w_out", tn_pref=tn_ride,
                           rider=lw.pop("ride_next_w_in"), mod_rider=mod_rider)
            out, nxt_w["w_in"] = res[0], res[1]
            if mod_rider is not None:
                nxt_w["mods"] = res[2]
        else:
            out = _mm_call(merged, lw["w_out"][None], BF16, "w_out")
        return out.reshape(nb, ls, d)

    def conv_ffn(h, lw, period):
        ffn1 = functools.partial(_ffn1_call, h, lw["w12"], lw["conv_w"], lw["conv_b"], period)
        rider = lw.pop("ride_next_w12", None)
        if not ffn_pad:
            act = ffn1(rider)
            if rider is not None:
                act, nxt_w["w12"] = act
            return _mm_call(act, lw["w2"][None], BF16, "ffn2", tm_pref=512, tn_pref=512)
        n_main = ffn // FFN_TILE
        act = ffn1(rider, FFN_TILE, 0, n_main)
        if rider is not None:
            act, nxt_w["w12"] = act
        rest = ffn1(None, ffn_pad, n_main * (FFN_TILE // ffn_pad), 1)
        return _mm_split_call(act, rest, lw["w2"], BF16, "ffn2", 512, 512)

    xs = x
    cs = ctx.reshape(1, nb * clen, d)
    nxt_w = {}
    hx = _ln_mod_call(xs, mods_list[0], None, 0, 1)
    hc = _ln_mod_call(cs, mods_list[0], ctx_row, 0, 1)
    for i in range(depth):
        last = i == depth - 1
        mods = mods_list[i]
        lw = {
            "layer": i,
            "ssm_d": ssm_d[i].reshape(1, sw),
            "ride_fourier_w": (_rows_cast(fourier_w, i, _tile(fw, CAST_ROWS)), fourier_w),
            "ride_glu_w": (_rows_cast(glu_w, i, _tile(sw, CAST_ROWS // 2)), glu_w),
            "ride_w_out": (_rows_cast(w_out, i, _tile(d, CAST_ROWS)), w_out),
        }
        cur_w, nxt_w = nxt_w, {}
        ride_w12 = ride_w2 = None
        if ffn_pad:
            if "w12" in cur_w:
                lw["w12"] = cur_w["w12"]
            else:
                ride_w12 = (_pad_halves_cast(ffn_w12, i, ffn_pad, HALVES_CAST_COLS), ffn_w12)
            ride_w2 = (_rows_cast(ffn_w2, i, ffn_pad), ffn_w2)
            if not last:
                lw["ride_next_w_in"] = (_rows_cast(w_in, i + 1, _tile(d, CAST_ROWS // 4)), w_in)
                lw["ride_next_w12"] = (_pad_halves_cast(ffn_w12, i + 1, ffn_pad, ffn_pad), ffn_w12)
        else:
            lw["w12"] = _cast_call(ffn_w12, i)
            lw["w2"] = _cast_call(ffn_w2, i)
        lw["conv_w"] = jnp.pad(ffn_conv_w[i], ((0, 0), (0, ffn_pad)))
        lw["conv_b"] = jnp.pad(ffn_conv_b[i].reshape(1, ffn), ((0, 0), (0, ffn_pad)))
        g1, b1 = ln1_g[i].reshape(1, d), ln1_b[i].reshape(1, d)
        g2, b2 = ln2_g[i].reshape(1, d), ln2_b[i].reshape(1, d)
        bmat, cmat, amat = _ssm_tables(ssm_a_re[i], ssm_a_im[i], ssm_log_dt[i], ssm_b_re[i],
                                       ssm_b_im[i], ssm_c_re[i], ssm_c_im[i])

        n_in = w_in.shape[2]
        hc2d = hc.reshape(nb * clen, d)
        c_col = 0 if last else fw
        if "w_in" in cur_w:
            w_in_b = cur_w["w_in"]
            proj_x = _mm_call(hx.reshape(1, nb * seq, d), w_in_b[None], BF16, "w_in", rider=ride_w12)
            w_in_c = w_in_b[None, :, fw:fw + sw] if last else w_in_b[None]
            proj_c = _mm_call(hc2d[None], w_in_c, BF16, "w_in_ctx")
        else:
            proj_x = _mm_w32_call(hx.reshape(nb * seq, d), w_in, i, 0, n_in, BF16, "w_in", ride_w12)
            if last:
                proj_c = _mm_w32_call(hc2d, w_in, i, fw, sw, BF16, "w_in_ctx")
            else:
                proj_c = _mm_w32_call(hc2d, w_in, i, 0, n_in, BF16, "w_in_ctx")
        if ride_w12 is not None:
            proj_x, lw["w12"] = proj_x
        proj_x = proj_x.reshape(nb, seq, n_in)
        proj_c = proj_c.reshape(nb, clen, -1)
        ssm_out = _ssm_call(proj_x, fw, proj_c, c_col, bmat, cmat, amat, chunk, ride_w2)
        yf, yb = ssm_out[0], ssm_out[1]
        if ride_w2 is not None:
            lw["w2"] = ssm_out[2]

        out_x = token_mix_tail(proj_x, csl_x, yf, yb, 0, lw)
        if not last and len(mods_list) == i + 1:
            if "mods" in nxt_w:
                mods_list.append(nxt_w["mods"].reshape(n_rows, 1, N_MOD * d))
            else:
                mods_list.append(_adaln_call(cond, ada_w, ada_b, i + 1, 1).reshape(n_rows, 1, N_MOD * d))
        nxt = None if last else (mods_list[i + 1], 0, 1)
        x1, h2 = _postnorm_call(xs, out_x, mods, None, 2, g1, b1, alpha, (mods, 3, 4))
        ffn_x = conv_ffn(h2, lw, GRID_W)
        xs, hx = _postnorm_call(x1, ffn_x, mods, None, 5, g2, b2, alpha, nxt)

        if not last:
            out_c = token_mix_tail(proj_c, csl_c, yf, yb, seq, lw).reshape(1, nb * clen, d)
            c1, hc2 = _postnorm_call(cs, out_c, mods, ctx_row, 2, g1, b1, alpha, (mods, 3, 4))
            ffn_c = conv_ffn(hc2, lw, clen)
            cs, hc = _postnorm_call(c1, ffn_c, mods, ctx_row, 5, g2, b2, alpha, nxt)
    return xs
```

```python
import functools
import math
from typing import Callable, NamedTuple

import jax
import jax.numpy as jnp
import ml_dtypes
import numpy as np
from jax import lax
from jax.experimental import pallas as pl
from jax.experimental.pallas import tpu as pltpu

GRID_W = 64
FOURIER_GROUPS = 4
N_MOD = 6
LN_EPS = 1e-6
SSM_BLOCK_GROUPS = 16
LANES = 128
SUBLANES = 8
VMEM_LIMIT = 56 * 1024 * 1024
GLU_SUBTILE = 256
CAST_ROWS = 256
HALVES_CAST_COLS = 128
FFN_TILE = 512
SSM_MM_ROWS = 256

F32 = jnp.float32
BF16 = jnp.bfloat16


def _params(sem):
    return pltpu.CompilerParams(dimension_semantics=sem, vmem_limit_bytes=VMEM_LIMIT)


def _tile(n, pref):
    if n <= pref:
        return n
    while n % pref:
        pref //= 2
    assert pref >= SUBLANES, (n, pref)
    return pref


def _lane_tile(n, cap):
    for t in range(cap - cap % LANES, 0, -LANES):
        if n % t == 0:
            return t
    return n


def _dot(a, b):
    return jnp.dot(a, b, preferred_element_type=F32)


def _layer_norm(x):
    mu = jnp.mean(x, axis=-1, keepdims=True)
    xc = x - mu
    var = jnp.mean(xc * xc, axis=-1, keepdims=True)
    return xc * lax.rsqrt(var + LN_EPS)


def _mod_spec(mod_row, k, d):
    if mod_row is None:
        return pl.BlockSpec((None, 1, d), lambda b, *_: (b, 0, k))
    return pl.BlockSpec((None, 1, d), lambda b, *_: (mod_row, 0, k))


def _cast_kernel(w_ref, o_ref):
    o_ref[...] = w_ref[...].astype(o_ref.dtype)


def _cast_call(w, layer):
    _, r, c = w.shape
    tr = _tile(r, 512)
    tc = _lane_tile(c, 6144)
    return pl.pallas_call(
        _cast_kernel,
        grid=(r // tr, c // tc),
        in_specs=[pl.BlockSpec((None, tr, tc), lambda i, j: (layer, i, j))],
        out_specs=pl.BlockSpec((tr, tc), lambda i, j: (i, j)),
        out_shape=jax.ShapeDtypeStruct((r, c), BF16),
        compiler_params=_params(("arbitrary", "arbitrary")),
        name="cast_w",
    )(w)


class _PadCast(NamedTuple):
    layer: int
    in_block: tuple
    in_index: Callable
    out_block: tuple
    out_index: Callable
    out_shape: tuple
    nblk: int
    is_data: Callable


def _rows_cast(w, layer, tr):
    _, r, c = w.shape
    assert r % tr == 0
    return _PadCast(layer, (None, tr, c), lambda s: (s, 0), (tr, c), lambda s: (s, 0), (r, c),
                    r // tr, lambda s: s >= 0)


def _pad_halves_cast(w, layer, pad, tc):
    _, r, c2 = w.shape
    f = c2 // 2
    nvalid = f // tc
    assert f % tc == 0 and pad % tc == 0
    per = nvalid + pad // tc
    return _PadCast(layer, (None, r, tc),
                    lambda s: (0, (s // per) * nvalid + jnp.minimum(s % per, nvalid - 1)),
                    (r, tc), lambda s: (0, s), (r, 2 * (f + pad)), 2 * per, lambda s: s % per < nvalid)


def _cast_specs(job, step_of):
    def clamped(*g):
        return jnp.minimum(step_of(*g), job.nblk - 1)

    in_spec = pl.BlockSpec(job.in_block, lambda *g: (job.layer,) + tuple(job.in_index(clamped(*g))))
    out_spec = pl.BlockSpec(job.out_block, lambda *g: tuple(job.out_index(clamped(*g))))
    return in_spec, out_spec


def _when_live(job, step, nsteps, work):
    if job.nblk == nsteps:
        work()
    else:
        pl.when(step < job.nblk)(work)


def _cast_step(job, step, w_ref, o_ref, nsteps=None):
    def work():
        o_ref[...] = jnp.where(job.is_data(step), w_ref[...], 0.0).astype(o_ref.dtype)

    _when_live(job, step, nsteps, work)


def _rider_fits(rider, nsteps):
    return rider[0].nblk <= nsteps


def _attach_rider(rider, step_of, in_specs, out_specs, out_shape, args):
    job, rw = rider
    r_in, r_out = _cast_specs(job, step_of)
    in_specs.append(r_in)
    out_specs.append(r_out)
    out_shape.append(jax.ShapeDtypeStruct(job.out_shape, BF16))
    args.append(rw)
    return job


def _pad_cast_kernel(w_ref, o_ref, *, job):
    _cast_step(job, pl.program_id(0), w_ref, o_ref)


def _pad_cast_call(job, w):
    in_spec, out_spec = _cast_specs(job, lambda s: s)
    return pl.pallas_call(
        functools.partial(_pad_cast_kernel, job=job),
        grid=(job.nblk,),
        in_specs=[in_spec],
        out_specs=out_spec,
        out_shape=jax.ShapeDtypeStruct(job.out_shape, BF16),
        compiler_params=_params(("arbitrary",)),
        name="cast_w_pad",
    )(w)


def _adaln_kernel(c_ref, w_ref, b_ref, o_ref):
    c = c_ref[...]
    s = (c * jax.nn.sigmoid(c)).astype(BF16)
    o_ref[...] = _dot(s, w_ref[...].astype(BF16)) + b_ref[...]


class _ModJob(NamedTuple):
    layer: int
    tcol: int
    nblk: int


def _mod_job(ada_w, layer, nsteps):
    n = ada_w.shape[2]
    for tcol in range(LANES, n + 1, LANES):
        if n % tcol == 0 and n // tcol <= nsteps:
            return _ModJob(layer, tcol, n // tcol)
    return None


def _mod_attach(job, operands, step_of, in_specs, out_specs, out_shape, args):
    cond, ada_w, ada_b = operands
    r, d = cond.shape

    def clamped(*g):
        return jnp.minimum(step_of(*g), job.nblk - 1)

    in_specs += [
        pl.BlockSpec((r, d), lambda *g: (0, 0)),
        pl.BlockSpec((None, d, job.tcol), lambda *g: (job.layer, 0, clamped(*g))),
        pl.BlockSpec((None, 1, job.tcol), lambda *g: (job.layer, 0, clamped(*g))),
    ]
    out_specs.append(pl.BlockSpec((r, job.tcol), lambda *g: (0, clamped(*g))))
    out_shape.append(jax.ShapeDtypeStruct((r, ada_w.shape[2]), F32))
    args += [cond, ada_w, ada_b.reshape(ada_b.shape[0], 1, -1)]


def _mod_step(job, step, c_ref, w_ref, b_ref, o_ref, nsteps=None):
    _when_live(job, step, nsteps, functools.partial(_adaln_kernel, c_ref, w_ref, b_ref, o_ref))


def _adaln_call(cond, ada_w, ada_b, layer_lo=0, nlayer=None):
    all_layers, d, n = ada_w.shape
    depth = all_layers - layer_lo if nlayer is None else nlayer
    r = cond.shape[0]
    tn = _tile(n, 512)
    return pl.pallas_call(
        _adaln_kernel,
        grid=(depth, n // tn),
        in_specs=[
            pl.BlockSpec((r, d), lambda l, j: (0, 0)),
            pl.BlockSpec((None, d, tn), lambda l, j: (layer_lo + l, 0, j)),
            pl.BlockSpec((None, 1, tn), lambda l, j: (layer_lo + l, 0, j)),
        ],
        out_specs=pl.BlockSpec((None, r, tn), lambda l, j: (l, 0, j)),
        out_shape=jax.ShapeDtypeStruct((depth, r, n), F32),
        compiler_params=_params(("arbitrary", "arbitrary")),
        name="adaln",
    )(cond, ada_w, ada_b.reshape(all_layers, 1, n))


def _ln_mod_kernel(x_ref, sh_ref, sc_ref, o_ref):
    y = _layer_norm(x_ref[...])
    o_ref[...] = (y * (1.0 + sc_ref[...]) + sh_ref[...]).astype(o_ref.dtype)


def _ln_mod_call(x, mods, mod_row, k_shift, k_scale):
    bm, lm, d = x.shape
    tm = _tile(lm, 256)
    return pl.pallas_call(
        _ln_mod_kernel,
        grid=(bm, lm // tm),
        in_specs=[
            pl.BlockSpec((None, tm, d), lambda b, i: (b, i, 0)),
            _mod_spec(mod_row, k_shift, d),
            _mod_spec(mod_row, k_scale, d),
        ],
        out_specs=pl.BlockSpec((None, tm, d), lambda b, i: (b, i, 0)),
        out_shape=jax.ShapeDtypeStruct((bm, lm, d), BF16),
        compiler_params=_params(("arbitrary", "arbitrary")),
        name="ln_mod",
    )(x, mods, mods)


def _postnorm_kernel(res_ref, pre_ref, gate_ref, g_ref, b_ref, *rest, alpha, with_next):
    z = alpha * res_ref[...] + gate_ref[...] * pre_ref[...].astype(F32)
    x_new = _layer_norm(z) * g_ref[...] + b_ref[...]
    if with_next:
        sh_ref, sc_ref, o_ref, h_ref = rest
        o_ref[...] = x_new
        h_ref[...] = (_layer_norm(x_new) * (1.0 + sc_ref[...]) + sh_ref[...]).astype(h_ref.dtype)
    else:
        (o_ref,) = rest
        o_ref[...] = x_new


def _postnorm_call(res, pre, mods, mod_row, k_gate, ln_g, ln_b, alpha, next_mod=None):
    bm, lm, d = res.shape
    tm = _tile(lm, 256)
    row = pl.BlockSpec((None, tm, d), lambda b, i: (b, i, 0))
    vec = pl.BlockSpec((1, d), lambda b, i: (0, 0))
    in_specs = [row, row, _mod_spec(mod_row, k_gate, d), vec, vec]
    args = [res, pre, mods, ln_g, ln_b]
    out_specs = [row]
    out_shape = [jax.ShapeDtypeStruct((bm, lm, d), F32)]
    if next_mod is not None:
        nmods, k_shift, k_scale = next_mod
        in_specs += [_mod_spec(mod_row, k_shift, d), _mod_spec(mod_row, k_scale, d)]
        args += [nmods, nmods]
        out_specs.append(row)
        out_shape.append(jax.ShapeDtypeStruct((bm, lm, d), BF16))
    out = pl.pallas_call(
        functools.partial(_postnorm_kernel, alpha=alpha, with_next=next_mod is not None),
        grid=(bm, lm // tm),
        in_specs=in_specs,
        out_specs=out_specs,
        out_shape=out_shape,
        compiler_params=_params(("arbitrary", "arbitrary")),
        name="postnorm",
    )(*args)
    return (out[0], out[1]) if next_mod is not None else (out[0], None)


def _mm_kernel(a_ref, w_ref, *rest, nb, ni, nj, job, mod_job):
    n_side_in = (job is not None) + 3 * (mod_job is not None)
    side_in, (o_ref, *side_out) = rest[:n_side_in], rest[n_side_in:]
    step = (pl.program_id(0) * ni + pl.program_id(1)) * nj + pl.program_id(2)
    nsteps = nb * ni * nj
    if job is not None:
        _cast_step(job, step, side_in[0], side_out[0], nsteps)
    if mod_job is not None:
        _mod_step(mod_job, step, *side_in[-3:], side_out[-1], nsteps)
    o_ref[...] = _dot(a_ref[...], w_ref[...]).astype(o_ref.dtype)


def _mm_call(a, w, out_dtype, name, tm_pref=1024, tn_pref=1024, rider=None, mod_rider=None):
    ba, m, k = a.shape
    bw, _, n = w.shape
    nb = max(ba, bw)
    tm = _tile(m, tm_pref)
    tn = _tile(n, tn_pref)
    ni, nj = m // tm, n // tn
    if rider is not None and not _rider_fits(rider, nb * ni * nj):
        out = _mm_call(a, w, out_dtype, name, tm_pref, tn_pref, mod_rider=mod_rider)
        out = out if mod_rider is not None else (out,)
        return (out[0], _pad_cast_call(*rider)) + tuple(out[1:])
    rows_outer = ba == 1 and bw > 1
    n0, n1 = (ni, nb) if rows_outer else (nb, ni)

    def in_grid_order(f):
        return (lambda i, b, j: f(b, i, j)) if rows_outer else f

    a_map = (lambda b, i, j: (b, i, 0)) if ba > 1 else (lambda b, i, j: (0, i, 0))
    w_map = (lambda b, i, j: (b, 0, j)) if bw > 1 else (lambda b, i, j: (0, 0, j))
    in_specs = [pl.BlockSpec((None, tm, k), in_grid_order(a_map)),
                pl.BlockSpec((None, k, tn), in_grid_order(w_map))]
    out_specs = [pl.BlockSpec((None, tm, tn), in_grid_order(lambda b, i, j: (b, i, j)))]
    out_shape = [jax.ShapeDtypeStruct((nb, m, n), out_dtype)]
    args = [a, w]
    job = mod_job = None

    def step_of(p0, p1, j):
        return (p0 * n1 + p1) * nj + j

    if rider is not None:
        job = _attach_rider(rider, step_of, in_specs, out_specs, out_shape, args)
    if mod_rider is not None:
        mod_job, operands = mod_rider
        assert mod_job.nblk <= nb * ni * nj
        _mod_attach(mod_job, operands, step_of, in_specs, out_specs, out_shape, args)
    out = pl.pallas_call(
        functools.partial(_mm_kernel, nb=n0, ni=n1, nj=nj, job=job, mod_job=mod_job),
        grid=(n0, n1, nj),
        in_specs=in_specs,
        out_specs=out_specs,
        out_shape=out_shape,
        compiler_params=_params(("arbitrary", "arbitrary", "arbitrary")),
        name=name,
    )(*args)
    return out if len(out) > 1 else out[0]


def _mm_split_kernel(a1_ref, a2_ref, w1_ref, w2_ref, o_ref):
    acc = _dot(a1_ref[...], w1_ref[...]) + _dot(a2_ref[...], w2_ref[...])
    o_ref[...] = acc.astype(o_ref.dtype)


def _mm_split_call(a1, a2, w, out_dtype, name, tm_pref, tn_pref):
    nb, m, k1 = a1.shape
    k2 = a2.shape[2]
    n = w.shape[1]
    assert w.shape[0] == k1 + k2 and k1 % k2 == 0
    tm = _tile(m, tm_pref)
    tn = _tile(n, tn_pref)
    return pl.pallas_call(
        _mm_split_kernel,
        grid=(nb, m // tm, n // tn),
        in_specs=[
            pl.BlockSpec((None, tm, k1), lambda b, i, j: (b, i, 0)),
            pl.BlockSpec((None, tm, k2), lambda b, i, j: (b, i, 0)),
            pl.BlockSpec((k1, tn), lambda b, i, j: (0, j)),
            pl.BlockSpec((k2, tn), lambda b, i, j: (k1 // k2, j)),
        ],
        out_specs=pl.BlockSpec((None, tm, tn), lambda b, i, j: (b, i, j)),
        out_shape=jax.ShapeDtypeStruct((nb, m, n), out_dtype),
        compiler_params=_params(("arbitrary", "arbitrary", "arbitrary")),
        name=name,
    )(a1, a2, w, w)


def _mm_w32_kernel(a_ref, w_ref, *rest, ni, job):
    if job is None:
        o_ref, w_scr = rest
    else:
        rw_ref, o_ref, ro_ref, w_scr = rest
    i = pl.program_id(1)

    @pl.when(i == 0)
    def _():
        w_scr[...] = w_ref[...].astype(w_scr.dtype)

    if job is not None:
        _cast_step(job, pl.program_id(0) * ni + i, rw_ref, ro_ref)
    o_ref[...] = _dot(a_ref[...], w_scr[...]).astype(o_ref.dtype)


def _mm_w32_call(a, w, layer, col0, ncols, out_dtype, name, rider=None):
    m, k = a.shape
    tm = _tile(m, 1024)
    tn = _tile(ncols, 512)
    assert col0 % tn == 0
    j0 = col0 // tn
    ni = m // tm
    in_specs = [
        pl.BlockSpec((tm, k), lambda j, i: (i, 0)),
        pl.BlockSpec((None, k, tn), lambda j, i: (layer, 0, j0 + j)),
    ]
    out_specs = [pl.BlockSpec((tm, tn), lambda j, i: (i, j))]
    out_shape = [jax.ShapeDtypeStruct((m, ncols), out_dtype)]
    args = [a, w]
    job = None
    if rider is not None and not _rider_fits(rider, (ncols // tn) * ni):
        return _mm_w32_call(a, w, layer, col0, ncols, out_dtype, name), _pad_cast_call(*rider)
    if rider is not None:
        job = _attach_rider(rider, lambda j, i: j * ni + i, in_specs, out_specs, out_shape, args)
    out = pl.pallas_call(
        functools.partial(_mm_w32_kernel, ni=ni, job=job),
        grid=(ncols // tn, ni),
        in_specs=in_specs,
        out_specs=out_specs,
        out_shape=out_shape,
        scratch_shapes=[pltpu.VMEM((k, tn), BF16)],
        compiler_params=_params(("arbitrary", "arbitrary")),
        name=name,
    )(*args)
    return out if rider is not None else out[0]


def _chan_dft_kernel(u_ref, cs_ref, o_ref, *, gd):
    r = _dot(u_ref[...], cs_ref[...])
    o_ref[0] = r[:, :gd].astype(o_ref.dtype)
    o_ref[1] = r[:, gd:].astype(o_ref.dtype)


def _chan_dft_call(proj, cs, fw):
    bs, ls, _ = proj.shape
    gd = fw // FOURIER_GROUPS
    tm = _tile(ls, 1024)
    return pl.pallas_call(
        functools.partial(_chan_dft_kernel, gd=gd),
        grid=(bs, ls // tm, FOURIER_GROUPS),
        in_specs=[
            pl.BlockSpec((None, tm, gd), lambda b, i, g: (b, i, g)),
            pl.BlockSpec((gd, 2 * gd), lambda b, i, g: (0, 0)),
        ],
        out_specs=pl.BlockSpec((None, 2, tm, gd), lambda b, i, g: (b, 0, i, g)),
        out_shape=jax.ShapeDtypeStruct((bs, 2, ls, fw), BF16),
        compiler_params=_params(("arbitrary", "arbitrary", "arbitrary")),
        name="chan_dft",
    )(proj, cs)


def _ssm_kernel(uxf_ref, uxb_ref, ucf_ref, ucb_ref, bm_ref, cm_ref, a_ref, *rest,
                chunk, nbat, half, n_ctx, n_all, job):
    side_work = None
    if job is None:
        yf_ref, yb_ref, *scratch = rest
    else:
        rw_ref, yf_ref, yb_ref, ro_ref, *scratch = rest
        step = pl.program_id(0) * n_all + pl.program_id(1)
        side_work = functools.partial(_cast_step, job, step, rw_ref, ro_ref)
    _ssm_step(uxf_ref, uxb_ref, ucf_ref, ucb_ref, bm_ref, cm_ref, a_ref, yf_ref, yb_ref,
              *scratch, chunk=chunk, nbat=nbat, half=half, n_ctx=n_ctx, side_work=side_work)


def _ssm_step(uxf_ref, uxb_ref, ucf_ref, ucb_ref, bm_ref, cm_ref, a_ref, yf_ref, yb_ref,
              u_scr, uin_scr, s_scr, y_scr, h_scr, *, chunk, nbat, half, n_ctx, side_work):
    k = pl.program_id(0)
    c = pl.program_id(1)
    rows = SUBLANES
    hw = half // 2
    mm_rows = min(chunk * rows, SSM_MM_ROWS)
    fwd_starts = list(range(0, chunk * rows, mm_rows))
    row_starts = (fwd_starts, fwd_starts[::-1])

    @pl.when(jnp.logical_and(k == 0, c == 0))
    def _():
        uin_scr[...] = jnp.zeros_like(uin_scr)

    @pl.when(c == 0)
    def _():
        h_scr[...] = jnp.zeros_like(h_scr)

    @pl.when(c < n_ctx)
    def _():
        u_scr[0] = ucf_ref[...]
        u_scr[1] = ucb_ref[...]

    @pl.when(c >= n_ctx)
    def _():
        u_scr[0] = uxf_ref[...]
        u_scr[1] = uxb_ref[...]

    if side_work is not None:
        side_work()
    for d in range(2):
        for b in range(nbat):
            ub = u_scr[d, b].astype(F32)
            for h in range(2):
                uin_scr[d, h, pl.ds(2 * b + h, chunk, stride=rows), :] = ub[:, h * LANES:(h + 1) * LANES]
        bm = bm_ref[d]
        for r0 in row_starts[d]:
            rs = slice(r0, r0 + mm_rows)
            lhs = jnp.concatenate([uin_scr[d, 0, rs], uin_scr[d, 1, rs]], axis=1).astype(BF16)
            s_scr[d, rs] = _dot(lhs, bm)

    arf, aif = a_ref[0, 0], a_ref[0, 1]
    arb, aib = a_ref[1, 0], a_ref[1, 1]

    def body(t, carry):
        hrf, hif, hrb, hib = carry
        rowf = t * rows
        rowb = (chunk - 1 - t) * rows
        xf = s_scr[0, pl.ds(rowf, rows), :]
        xb = s_scr[1, pl.ds(rowb, rows), :]
        nrf = arf * hrf - aif * hif + xf[:, :hw]
        nif = arf * hif + aif * hrf + xf[:, hw:]
        nrb = arb * hrb - aib * hib + xb[:, :hw]
        nib = arb * hib + aib * hrb + xb[:, hw:]
        s_scr[0, pl.ds(rowf, rows), :hw] = nrf
        s_scr[0, pl.ds(rowf, rows), hw:] = nif
        s_scr[1, pl.ds(rowb, rows), :hw] = nrb
        s_scr[1, pl.ds(rowb, rows), hw:] = nib
        return nrf, nif, nrb, nib

    init = (h_scr[0, :, :hw], h_scr[0, :, hw:], h_scr[1, :, :hw], h_scr[1, :, hw:])
    carry = init
    for t in range(chunk):
        carry = body(t, carry)
    hrf, hif, hrb, hib = carry
    h_scr[0, :, :hw] = hrf
    h_scr[0, :, hw:] = hif
    h_scr[1, :, :hw] = hrb
    h_scr[1, :, hw:] = hib

    for d, y_ref in enumerate((yf_ref, yb_ref)):
        cm = cm_ref[d]
        for r0 in row_starts[d]:
            rs = slice(r0, r0 + mm_rows)
            out = _dot(s_scr[d, rs].astype(BF16), cm)
            y_scr[d, 0, rs] = out[:, :LANES]
            y_scr[d, 1, rs] = out[:, LANES:]
        for b in range(nbat):
            for h in range(2):
                piece = y_scr[d, h, pl.ds(2 * b + h, chunk, stride=rows), :]
                y_ref[b, :, h * LANES:(h + 1) * LANES] = piece.astype(y_ref.dtype)


def _ssm_call(ux, ux_col, uc, uc_col, bmat, cmat, amat, chunk, rider=None):
    nbat, seq, _ = ux.shape
    clen = uc.shape[1]
    assert 2 * nbat == SUBLANES, "the scan packs 2 column halves x batch on the 8 sublanes"
    ndir, nblk, cin, cst = bmat.shape
    assert cin == 2 * LANES, "each column half of a group block is one 128-lane input slab"
    sw = nblk * cin
    n_lat, n_ctx = seq // chunk, clen // chunk
    n_all = n_lat + n_ctx
    assert ux_col % cin == 0 and uc_col % cin == 0
    xo, co = ux_col // cin, uc_col // cin

    def lat_f(c):
        return jnp.maximum(c - n_ctx, 0)

    def lat_b(c):
        return n_lat - 1 - jnp.maximum(c - n_ctx, 0)

    def ctx_f(c):
        return jnp.minimum(c, n_ctx - 1)

    def ctx_b(c):
        return jnp.maximum(n_ctx - 1 - c, 0)

    def out_f(c):
        return jnp.where(c < n_ctx, n_lat + c, c - n_ctx)

    def out_b(c):
        return jnp.where(c < n_ctx, n_lat + n_ctx - 1 - c, n_lat - 1 - (c - n_ctx))

    ublk = (nbat, chunk, cin)
    in_specs = [
        pl.BlockSpec(ublk, lambda k, c: (0, lat_f(c), xo + k)),
        pl.BlockSpec(ublk, lambda k, c: (0, lat_b(c), xo + k)),
        pl.BlockSpec(ublk, lambda k, c: (0, ctx_f(c), co + k)),
        pl.BlockSpec(ublk, lambda k, c: (0, ctx_b(c), co + k)),
        pl.BlockSpec((ndir, None, cin, cst), lambda k, c: (0, k, 0, 0)),
        pl.BlockSpec((ndir, None, cst, cin), lambda k, c: (0, k, 0, 0)),
        pl.BlockSpec((ndir, None, 2, SUBLANES, cst // 2), lambda k, c: (0, k, 0, 0, 0)),
    ]
    out_specs = [
        pl.BlockSpec(ublk, lambda k, c: (0, out_f(c), k)),
        pl.BlockSpec(ublk, lambda k, c: (0, out_b(c), k)),
    ]
    out_shape = [jax.ShapeDtypeStruct((nbat, seq + clen, sw), BF16)] * 2
    args = [ux, ux, uc, uc, bmat, cmat, amat]
    job = None
    if rider is not None and not _rider_fits(rider, nblk * n_all):
        yf, yb = _ssm_call(ux, ux_col, uc, uc_col, bmat, cmat, amat, chunk)
        return yf, yb, _pad_cast_call(*rider)
    if rider is not None:
        job = _attach_rider(rider, lambda k, c: k * n_all + c, in_specs, out_specs, out_shape, args)
    return pl.pallas_call(
        functools.partial(_ssm_kernel, chunk=chunk, nbat=nbat, half=cst, n_ctx=n_ctx,
                          n_all=n_all, job=job),
        grid=(nblk, n_all),
        in_specs=in_specs,
        out_specs=out_specs,
        out_shape=out_shape,
        scratch_shapes=[
            pltpu.VMEM((ndir, nbat, chunk, cin), BF16),
            pltpu.VMEM((ndir, 2, SUBLANES * chunk, LANES), F32),
            pltpu.VMEM((ndir, SUBLANES * chunk, cst), F32),
            pltpu.VMEM((ndir, 2, SUBLANES * chunk, LANES), F32),
            pltpu.VMEM((ndir, SUBLANES, cst), F32),
        ],
        compiler_params=_params(("arbitrary", "arbitrary")),
        name="ssm",
    )(*args)


def _ssm_tables(a_re, a_im, log_dt, b_re, b_im, c_re, c_im):
    ndir, g, p = a_re.shape
    cg = b_re.shape[-1]
    gb = SSM_BLOCK_GROUPS
    nblk = g // gb
    hg = gb // 2
    npair = hg // 2
    assert 2 * p == LANES and g % gb == 0 and hg * cg == LANES
    lr, li = a_re.astype(F32), a_im.astype(F32)
    dt = jnp.exp(log_dt.astype(F32))[..., None]
    zr, zi = lr * dt, li * dt
    a_bar_r = jnp.exp(zr) * jnp.cos(zi)
    a_bar_i = jnp.exp(zr) * jnp.sin(zi)
    em1_r = jnp.expm1(zr) * jnp.cos(zi) - 2.0 * jnp.square(jnp.sin(0.5 * zi))
    den = lr * lr + li * li
    cf_r = ((em1_r * lr + a_bar_i * li) / den)[..., None]
    cf_i = ((a_bar_i * lr - em1_r * li) / den)[..., None]
    br, bi = b_re.astype(F32), b_im.astype(F32)
    b_bar_r = cf_r * br - cf_i * bi
    b_bar_i = cf_r * bi + cf_i * br

    lane_q = np.arange(LANES) // p
    own = (np.arange(gb)[:, None, None] % hg
           == 2 * np.arange(npair)[None, :, None] + lane_q[None, None, :])
    own_in = np.repeat(own, cg, axis=0).astype(np.float32)
    own_in = own_in.reshape(gb * cg, npair * LANES)
    own_out = np.ascontiguousarray(own_in.T)

    def in_mat(bb):
        t = jnp.swapaxes(bb.reshape(ndir, nblk, gb, p, cg), -1, -2).reshape(ndir, nblk, gb * cg, p)
        return jnp.tile(t, (1, 1, 1, 2 * npair)) * own_in

    def out_mat(cc):
        t = jnp.swapaxes(cc.reshape(ndir, nblk, gb * cg, p), -1, -2)
        return jnp.tile(t, (1, 1, 2 * npair, 1)) * own_out

    bmat = jnp.concatenate([in_mat(b_bar_r), in_mat(b_bar_i)], axis=-1).astype(BF16)
    cmat = jnp.concatenate([out_mat(c_re.astype(F32)), out_mat(-c_im.astype(F32))], axis=-2).astype(BF16)

    def decay(a):
        t = a.reshape(ndir, nblk, 2, npair * LANES)
        return jnp.tile(t, (1, 1, SUBLANES // 2, 1))
    aa = jnp.stack([decay(a_bar_r), decay(a_bar_i)], axis=2)
    return bmat, cmat, aa


def _ssm_act_kernel(u_ref, yf_ref, yb_ref, d_ref, o_ref):
    y = d_ref[...] * u_ref[...].astype(F32) + yf_ref[...].astype(F32) + yb_ref[...].astype(F32)
    o_ref[...] = jax.nn.gelu(y).astype(o_ref.dtype)


def _ssm_act_call(proj, yf, yb, row_off, ssm_d, fw, sw):
    bs, ls, _ = proj.shape
    tm = _tile(ls, 512)
    assert fw % sw == 0 and row_off % tm == 0
    u_blk, r_blk = fw // sw, row_off // tm
    return pl.pallas_call(
        _ssm_act_kernel,
        grid=(bs, ls // tm),
        in_specs=[
            pl.BlockSpec((None, tm, sw), lambda b, i: (b, i, u_blk)),
            pl.BlockSpec((None, tm, sw), lambda b, i: (b, i + r_blk, 0)),
            pl.BlockSpec((None, tm, sw), lambda b, i: (b, i + r_blk, 0)),
            pl.BlockSpec((1, sw), lambda b, i: (0, 0)),
        ],
        out_specs=pl.BlockSpec((None, tm, sw), lambda b, i: (b, i, 0)),
        out_shape=jax.ShapeDtypeStruct((bs, ls, sw), BF16),
        compiler_params=_params(("arbitrary", "arbitrary")),
        name="ssm_act",
    )(proj, yf, yb, ssm_d)


def _glu_kernel(a_ref, wv_ref, wg_ref, gf_ref, gs_ref, fo_ref, *rest, nb, ni, job):
    if job is None:
        (o_ref,) = rest
    else:
        rw_ref, o_ref, ro_ref = rest
        step = (pl.program_id(0) * nb + pl.program_id(1)) * ni + pl.program_id(2)
        _cast_step(job, step, rw_ref, ro_ref)
    a = a_ref[...]
    tn = o_ref.shape[1]
    sub = min(tn, GLU_SUBTILE)
    for n0 in range(0, tn, sub):
        cols = slice(n0, n0 + sub)
        y_s = (_dot(a, wv_ref[:, cols]) * jax.nn.sigmoid(_dot(a, wg_ref[:, cols]))).astype(BF16)
        o_ref[:, cols] = (jax.nn.sigmoid(gf_ref[:, cols]) * fo_ref[:, cols]
                          + jax.nn.sigmoid(gs_ref[:, cols]) * y_s)


def _glu_call(act, proj, glu_w, y_f, fw, sw, d, rider=None):
    bs, ls, _ = act.shape
    tm = _tile(ls, 512)
    tn = _tile(d, 1024)
    assert (fw + sw) % tn == 0
    gf_blk = (fw + sw) // tn
    gs_blk = (fw + sw + d) // tn
    ni, nj = ls // tm, d // tn
    if rider is not None and not _rider_fits(rider, bs * ni * nj):
        return _glu_call(act, proj, glu_w, y_f, fw, sw, d), _pad_cast_call(*rider)
    in_specs = [
        pl.BlockSpec((None, tm, sw), lambda j, b, i: (b, i, 0)),
        pl.BlockSpec((sw, tn), lambda j, b, i: (0, j)),
        pl.BlockSpec((sw, tn), lambda j, b, i: (0, nj + j)),
        pl.BlockSpec((None, tm, tn), lambda j, b, i: (b, i, gf_blk + j)),
        pl.BlockSpec((None, tm, tn), lambda j, b, i: (b, i, gs_blk + j)),
        pl.BlockSpec((None, tm, tn), lambda j, b, i: (b, i, j)),
    ]
    out_specs = [pl.BlockSpec((None, tm, tn), lambda j, b, i: (b, i, j))]
    out_shape = [jax.ShapeDtypeStruct((bs, ls, d), BF16)]
    args = [act, glu_w, glu_w, proj, proj, y_f]
    job = None
    if rider is not None:
        job = _attach_rider(rider, lambda j, b, i: (j * bs + b) * ni + i,
                            in_specs, out_specs, out_shape, args)
    out = pl.pallas_call(
        functools.partial(_glu_kernel, nb=bs, ni=ni, job=job),
        grid=(nj, bs, ni),
        in_specs=in_specs,
        out_specs=out_specs,
        out_shape=out_shape,
        compiler_params=_params(("arbitrary", "arbitrary", "arbitrary")),
        name="glu_merge",
    )(*args)
    return out if rider is not None else out[0]


def _ffn1_kernel(h_ref, wu_ref, wv_ref, cw_ref, cb_ref, *rest, period, nb, ni, nj, job):
    if job is None:
        (o_ref,) = rest
    else:
        rw_ref, o_ref, ro_ref = rest
        step = (pl.program_id(0) * ni + pl.program_id(1)) * nj + pl.program_id(2)
        _cast_step(job, step, rw_ref, ro_ref, nb * ni * nj)
    h = h_ref[...]
    u = _dot(h, wu_ref[...])
    v = _dot(h, wv_ref[...])
    tm = u.shape[0]
    pos = lax.broadcasted_iota(jnp.int32, u.shape, 0) % period
    prev = jnp.where(pos == 0, 0.0, pltpu.roll(u, 1, axis=0))
    nxt = jnp.where(pos == period - 1, 0.0, pltpu.roll(u, tm - 1, axis=0))
    cw = cw_ref[...]
    conv = cb_ref[...] + cw[0:1] * prev + cw[1:2] * u + cw[2:3] * nxt
    o_ref[...] = (jax.nn.gelu(conv) * v).astype(o_ref.dtype)


def _ffn1_call(h, w12, conv_w, conv_b, period, rider=None, tn=None, j0=0, nj=None):
    bm, lm, d = h.shape
    ffn = w12.shape[1] // 2
    tm = _tile(lm, 1024)
    tn = _tile(ffn, 512) if tn is None else tn
    nj = ffn // tn if nj is None else nj
    ni = lm // tm
    v0 = ffn // tn + j0
    assert tm % period == 0 and ffn % tn == 0
    if rider is not None and not _rider_fits(rider, bm * ni * nj):
        return _ffn1_call(h, w12, conv_w, conv_b, period, None, tn, j0, nj), _pad_cast_call(*rider)
    in_specs = [
        pl.BlockSpec((None, tm, d), lambda b, i, j: (b, i, 0)),
        pl.BlockSpec((d, tn), lambda b, i, j: (0, j0 + j)),
        pl.BlockSpec((d, tn), lambda b, i, j: (0, v0 + j)),
        pl.BlockSpec((3, tn), lambda b, i, j: (0, j0 + j)),
        pl.BlockSpec((1, tn), lambda b, i, j: (0, j0 + j)),
    ]
    out_specs = [pl.BlockSpec((None, tm, tn), lambda b, i, j: (b, i, j))]
    out_shape = [jax.ShapeDtypeStruct((bm, lm, nj * tn), BF16)]
    args = [h, w12, w12, conv_w, conv_b]
    job = None
    if rider is not None:
        job = _attach_rider(rider, lambda b, i, j: (b * ni + i) * nj + j,
                            in_specs, out_specs, out_shape, args)
    out = pl.pallas_call(
        functools.partial(_ffn1_kernel, period=period, nb=bm, ni=ni, nj=nj, job=job),
        grid=(bm, ni, nj),
        in_specs=in_specs,
        out_specs=out_specs,
        out_shape=out_shape,
        compiler_params=_params(("arbitrary", "arbitrary", "arbitrary")),
        name="ffn1",
    )(*args)
    return out if rider is not None else out[0]


def _dft_cos_sin(n):
    idx = np.arange(n, dtype=np.int64)
    ang = (2.0 * math.pi / n) * ((idx[:, None] * idx[None, :]) % n)
    scale = 1.0 / math.sqrt(n)
    return np.cos(ang) * scale, np.sin(ang) * scale


def _bf16_const(a):
    return jnp.asarray(np.asarray(a, np.float32).astype(ml_dtypes.bfloat16))


def kernel(x, c, ctx, c_ctx, ada_w, ada_b, w_in, fourier_w, ssm_a_re, ssm_a_im, ssm_log_dt,
           ssm_b_re, ssm_b_im, ssm_c_re, ssm_c_im, ssm_d, glu_w, w_out, ln1_g, ln1_b,
           ffn_w12, ffn_conv_w, ffn_conv_b, ffn_w2, ln2_g, ln2_b):
    nb, seq, d = x.shape
    clen = ctx.shape[1]
    depth = ada_w.shape[0]
    fw = fourier_w.shape[1]
    sw = ssm_d.shape[1]
    ffn = ffn_conv_b.shape[1]
    alpha = (2.0 * depth) ** 0.25
    gd = fw // FOURIER_GROUPS
    chunk = clen
    assert seq % chunk == 0 and seq % GRID_W == 0
    ffn_pad = FFN_TILE // 2 if (ffn % FFN_TILE == FFN_TILE // 2) else 0

    ctx_row = nb
    n_rows = -(-(nb + 1) // SUBLANES) * SUBLANES
    cond = jnp.zeros((n_rows, d), F32).at[:nb].set(c).at[ctx_row].set(c_ctx)
    ride_mods = ffn_pad > 0
    first_mods = _adaln_call(cond, ada_w, ada_b, 0, 1 if ride_mods else depth)
    mods_list = [m.reshape(n_rows, 1, N_MOD * d) for m in first_mods]

    cos_c, sin_c = _dft_cos_sin(gd)
    cs_chan = _bf16_const(np.concatenate([cos_c, sin_c], axis=1))

    def pos_dft_matrix(n):
        cos_l, sin_l = _dft_cos_sin(n)
        return _bf16_const(np.concatenate([cos_l, -sin_l], axis=1))[None]

    csl_x = pos_dft_matrix(seq)
    csl_c = pos_dft_matrix(clen)

    def token_mix_tail(proj, csl, yf, yb, row_off, lw):
        ls = proj.shape[1]
        ab = _chan_dft_call(proj, cs_chan, fw)
        ab = ab.reshape(nb, 2 * ls, fw)
        if "fourier_w" not in lw:
            f, lw["fourier_w"] = _mm_call(csl, ab, BF16, "pos_dft", rider=lw.pop("ride_fourier_w"))
        else:
            f = _mm_call(csl, ab, BF16, "pos_dft")
        f = f.reshape(1, nb * ls, fw)
        if "glu_w" not in lw:
            y_f, lw["glu_w"] = _mm_call(f, lw["fourier_w"][None], BF16, "fourier_out",
                                        rider=lw.pop("ride_glu_w"))
        else:
            y_f = _mm_call(f, lw["fourier_w"][None], BF16, "fourier_out")
        y_f = y_f.reshape(nb, ls, d)
        act = _ssm_act_call(proj, yf, yb, row_off, lw["ssm_d"], fw, sw)
        if "w_out" not in lw:
            merged, lw["w_out"] = _glu_call(act, proj, lw["glu_w"], y_f, fw, sw, d,
                                            rider=lw.pop("ride_w_out"))
        else:
            merged = _glu_call(act, proj, lw["glu_w"], y_f, fw, sw, d)
        merged = merged.reshape(1, nb * ls, d)
        if "ride_next_w_in" in lw:
            tn_ride = 512
            nsteps = (nb * ls // _tile(nb * ls, 1024)) * (d // _tile(d, tn_ride))
            mod_job = _mod_job(ada_w, lw["layer"] + 1, nsteps)
            mod_rider = None if mod_job is None else (mod_job, (cond, ada_w, ada_b))
            res = _mm_call(merged, lw["w_out"][None], BF16, "w_out", tn_pref=tn_ride,
                           rider=lw.pop("ride_next_w_in"), mod_rider=mod_rider)
            out, nxt_w["w_in"] = res[0], res[1]
            if mod_rider is not None:
                nxt_w["mods"] = res[2]
        else:
            out = _mm_call(merged, lw["w_out"][None], BF16, "w_out")
        return out.reshape(nb, ls, d)

    def conv_ffn(h, lw, period):
        ffn1 = functools.partial(_ffn1_call, h, lw["w12"], lw["conv_w"], lw["conv_b"], period)
        rider = lw.pop("ride_next_w12", None)
        if not ffn_pad:
            act = ffn1(rider)
            if rider is not None:
                act, nxt_w["w12"] = act
            return _mm_call(act, lw["w2"][None], BF16, "ffn2", tm_pref=512, tn_pref=512)
        n_main = ffn // FFN_TILE
        act = ffn1(rider, FFN_TILE, 0, n_main)
        if rider is not None:
            act, nxt_w["w12"] = act
        rest = ffn1(None, ffn_pad, n_main * (FFN_TILE // ffn_pad), 1)
        return _mm_split_call(act, rest, lw["w2"], BF16, "ffn2", 512, 512)

    xs = x
    cs = ctx.reshape(1, nb * clen, d)
    nxt_w = {}
    hx = _ln_mod_call(xs, mods_list[0], None, 0, 1)
    hc = _ln_mod_call(cs, mods_list[0], ctx_row, 0, 1)
    for i in range(depth):
        last = i == depth - 1
        mods = mods_list[i]
        lw = {
            "layer": i,
            "ssm_d": ssm_d[i].reshape(1, sw),
            "ride_fourier_w": (_rows_cast(fourier_w, i, _tile(fw, CAST_ROWS // 2)), fourier_w),
            "ride_glu_w": (_rows_cast(glu_w, i, _tile(sw, CAST_ROWS // 4)), glu_w),
            "ride_w_out": (_rows_cast(w_out, i, _tile(d, CAST_ROWS // 4)), w_out),
        }
        cur_w, nxt_w = nxt_w, {}
        ride_w12 = ride_w2 = None
        if ffn_pad:
            if "w12" in cur_w:
                lw["w12"] = cur_w["w12"]
            else:
                ride_w12 = (_pad_halves_cast(ffn_w12, i, ffn_pad, HALVES_CAST_COLS), ffn_w12)
            ride_w2 = (_rows_cast(ffn_w2, i, ffn_pad), ffn_w2)
            if not last:
                lw["ride_next_w_in"] = (_rows_cast(w_in, i + 1, _tile(d, CAST_ROWS // 4)), w_in)
                lw["ride_next_w12"] = (_pad_halves_cast(ffn_w12, i + 1, ffn_pad, ffn_pad), ffn_w12)
        else:
            lw["w12"] = _cast_call(ffn_w12, i)
            lw["w2"] = _cast_call(ffn_w2, i)
        lw["conv_w"] = jnp.pad(ffn_conv_w[i], ((0, 0), (0, ffn_pad)))
        lw["conv_b"] = jnp.pad(ffn_conv_b[i].reshape(1, ffn), ((0, 0), (0, ffn_pad)))
        g1, b1 = ln1_g[i].reshape(1, d), ln1_b[i].reshape(1, d)
        g2, b2 = ln2_g[i].reshape(1, d), ln2_b[i].reshape(1, d)
        bmat, cmat, amat = _ssm_tables(ssm_a_re[i], ssm_a_im[i], ssm_log_dt[i], ssm_b_re[i],
                                       ssm_b_im[i], ssm_c_re[i], ssm_c_im[i])

        n_in = w_in.shape[2]
        hc2d = hc.reshape(nb * clen, d)
        c_col = 0 if last else fw
        if "w_in" in cur_w:
            w_in_b = cur_w["w_in"]
            proj_x = _mm_call(hx.reshape(1, nb * seq, d), w_in_b[None], BF16, "w_in", rider=ride_w12)
            w_in_c = w_in_b[None, :, fw:fw + sw] if last else w_in_b[None]
            proj_c = _mm_call(hc2d[None], w_in_c, BF16, "w_in_ctx")
        else:
            proj_x = _mm_w32_call(hx.reshape(nb * seq, d), w_in, i, 0, n_in, BF16, "w_in", ride_w12)
            if last:
                proj_c = _mm_w32_call(hc2d, w_in, i, fw, sw, BF16, "w_in_ctx")
            else:
                proj_c = _mm_w32_call(hc2d, w_in, i, 0, n_in, BF16, "w_in_ctx")
        if ride_w12 is not None:
            proj_x, lw["w12"] = proj_x
        proj_x = proj_x.reshape(nb, seq, n_in)
        proj_c = proj_c.reshape(nb, clen, -1)
        ssm_out = _ssm_call(proj_x, fw, proj_c, c_col, bmat, cmat, amat, chunk, ride_w2)
        yf, yb = ssm_out[0], ssm_out[1]
        if ride_w2 is not None:
            lw["w2"] = ssm_out[2]

        out_x = token_mix_tail(proj_x, csl_x, yf, yb, 0, lw)
        if not last and len(mods_list) == i + 1:
            if "mods" in nxt_w:
                mods_list.append(nxt_w["mods"].reshape(n_rows, 1, N_MOD * d))
            else:
                mods_list.append(_adaln_call(cond, ada_w, ada_b, i + 1, 1).reshape(n_rows, 1, N_MOD * d))
        nxt = None if last else (mods_list[i + 1], 0, 1)
        x1, h2 = _postnorm_call(xs, out_x, mods, None, 2, g1, b1, alpha, (mods, 3, 4))
        ffn_x = conv_ffn(h2, lw, GRID_W)
        xs, hx = _postnorm_call(x1, ffn_x, mods, None, 5, g2, b2, alpha, nxt)

        if not last:
            out_c = token_mix_tail(proj_c, csl_c, yf, yb, seq, lw).reshape(1, nb * clen, d)
            c1, hc2 = _postnorm_call(cs, out_c, mods, ctx_row, 2, g1, b1, alpha, (mods, 3, 4))
            ffn_c = conv_ffn(hc2, lw, clen)
            cs, hc = _postnorm_call(c1, ffn_c, mods, ctx_row, 5, g2, b2, alpha, nxt)
    return xs
```

```python
import functools
import math
from typing import Callable, NamedTuple

import jax
import jax.numpy as jnp
import ml_dtypes
import numpy as np
from jax import lax
from jax.experimental import pallas as pl
from jax.experimental.pallas import tpu as pltpu

GRID_W = 64
FOURIER_GROUPS = 4
N_MOD = 6
LN_EPS = 1e-6
SSM_BLOCK_GROUPS = 16
LANES = 128
SUBLANES = 8
VMEM_LIMIT = 56 * 1024 * 1024
GLU_SUBTILE = 256
CAST_ROWS = 256
HALVES_CAST_COLS = 128
FFN_TILE = 512
SSM_MM_ROWS = 256

F32 = jnp.float32
BF16 = jnp.bfloat16


def _params(sem):
    return pltpu.CompilerParams(dimension_semantics=sem, vmem_limit_bytes=VMEM_LIMIT)


def _tile(n, pref):
    if n <= pref:
        return n
    while n % pref:
        pref //= 2
    assert pref >= SUBLANES, (n, pref)
    return pref


def _lane_tile(n, cap):
    for t in range(cap - cap % LANES, 0, -LANES):
        if n % t == 0:
            return t
    return n


def _dot(a, b):
    return jnp.dot(a, b, preferred_element_type=F32)


def _layer_norm(x):
    mu = jnp.mean(x, axis=-1, keepdims=True)
    xc = x - mu
    var = jnp.mean(xc * xc, axis=-1, keepdims=True)
    return xc * lax.rsqrt(var + LN_EPS)


def _mod_spec(mod_row, k, d):
    if mod_row is None:
        return pl.BlockSpec((None, 1, d), lambda b, *_: (b, 0, k))
    return pl.BlockSpec((None, 1, d), lambda b, *_: (mod_row, 0, k))


def _cast_kernel(w_ref, o_ref):
    o_ref[...] = w_ref[...].astype(o_ref.dtype)


def _cast_call(w, layer):
    _, r, c = w.shape
    tr = _tile(r, 512)
    tc = _lane_tile(c, 6144)
    return pl.pallas_call(
        _cast_kernel,
        grid=(r // tr, c // tc),
        in_specs=[pl.BlockSpec((None, tr, tc), lambda i, j: (layer, i, j))],
        out_specs=pl.BlockSpec((tr, tc), lambda i, j: (i, j)),
        out_shape=jax.ShapeDtypeStruct((r, c), BF16),
        compiler_params=_params(("arbitrary", "arbitrary")),
        name="cast_w",
    )(w)


class _PadCast(NamedTuple):
    layer: int
    in_block: tuple
    in_index: Callable
    out_block: tuple
    out_index: Callable
    out_shape: tuple
    nblk: int
    is_data: Callable


def _rows_cast(w, layer, tr):
    _, r, c = w.shape
    assert r % tr == 0
    return _PadCast(layer, (None, tr, c), lambda s: (s, 0), (tr, c), lambda s: (s, 0), (r, c),
                    r // tr, lambda s: s >= 0)


def _pad_halves_cast(w, layer, pad, tc):
    _, r, c2 = w.shape
    f = c2 // 2
    nvalid = f // tc
    assert f % tc == 0 and pad % tc == 0
    per = nvalid + pad // tc
    return _PadCast(layer, (None, r, tc),
                    lambda s: (0, (s // per) * nvalid + jnp.minimum(s % per, nvalid - 1)),
                    (r, tc), lambda s: (0, s), (r, 2 * (f + pad)), 2 * per, lambda s: s % per < nvalid)


def _cast_specs(job, step_of):
    def clamped(*g):
        return jnp.minimum(step_of(*g), job.nblk - 1)

    in_spec = pl.BlockSpec(job.in_block, lambda *g: (job.layer,) + tuple(job.in_index(clamped(*g))))
    out_spec = pl.BlockSpec(job.out_block, lambda *g: tuple(job.out_index(clamped(*g))))
    return in_spec, out_spec


def _when_live(job, step, nsteps, work):
    if job.nblk == nsteps:
        work()
    else:
        pl.when(step < job.nblk)(work)


def _cast_step(job, step, w_ref, o_ref, nsteps=None):
    def work():
        o_ref[...] = jnp.where(job.is_data(step), w_ref[...], 0.0).astype(o_ref.dtype)

    _when_live(job, step, nsteps, work)


def _rider_fits(rider, nsteps):
    return rider[0].nblk <= nsteps


def _attach_rider(rider, step_of, in_specs, out_specs, out_shape, args):
    job, rw = rider
    r_in, r_out = _cast_specs(job, step_of)
    in_specs.append(r_in)
    out_specs.append(r_out)
    out_shape.append(jax.ShapeDtypeStruct(job.out_shape, BF16))
    args.append(rw)
    return job


def _pad_cast_kernel(w_ref, o_ref, *, job):
    _cast_step(job, pl.program_id(0), w_ref, o_ref)


def _pad_cast_call(job, w):
    in_spec, out_spec = _cast_specs(job, lambda s: s)
    return pl.pallas_call(
        functools.partial(_pad_cast_kernel, job=job),
        grid=(job.nblk,),
        in_specs=[in_spec],
        out_specs=out_spec,
        out_shape=jax.ShapeDtypeStruct(job.out_shape, BF16),
        compiler_params=_params(("arbitrary",)),
        name="cast_w_pad",
    )(w)


def _adaln_kernel(c_ref, w_ref, b_ref, o_ref):
    c = c_ref[...]
    s = (c * jax.nn.sigmoid(c)).astype(BF16)
    o_ref[...] = _dot(s, w_ref[...].astype(BF16)) + b_ref[...]


class _ModJob(NamedTuple):
    layer: int
    tcol: int
    nblk: int


def _mod_job(ada_w, layer, nsteps):
    n = ada_w.shape[2]
    for tcol in range(LANES, n + 1, LANES):
        if n % tcol == 0 and n // tcol <= nsteps:
            return _ModJob(layer, tcol, n // tcol)
    return None


def _mod_attach(job, operands, step_of, in_specs, out_specs, out_shape, args):
    cond, ada_w, ada_b = operands
    r, d = cond.shape

    def clamped(*g):
        return jnp.minimum(step_of(*g), job.nblk - 1)

    in_specs += [
        pl.BlockSpec((r, d), lambda *g: (0, 0)),
        pl.BlockSpec((None, d, job.tcol), lambda *g: (job.layer, 0, clamped(*g))),
        pl.BlockSpec((None, 1, job.tcol), lambda *g: (job.layer, 0, clamped(*g))),
    ]
    out_specs.append(pl.BlockSpec((r, job.tcol), lambda *g: (0, clamped(*g))))
    out_shape.append(jax.ShapeDtypeStruct((r, ada_w.shape[2]), F32))
    args += [cond, ada_w, ada_b.reshape(ada_b.shape[0], 1, -1)]


def _mod_step(job, step, c_ref, w_ref, b_ref, o_ref, nsteps=None):
    _when_live(job, step, nsteps, functools.partial(_adaln_kernel, c_ref, w_ref, b_ref, o_ref))


def _adaln_call(cond, ada_w, ada_b, layer_lo=0, nlayer=None):
    all_layers, d, n = ada_w.shape
    depth = all_layers - layer_lo if nlayer is None else nlayer
    r = cond.shape[0]
    tn = _tile(n, 512)
    return pl.pallas_call(
        _adaln_kernel,
        grid=(depth, n // tn),
        in_specs=[
            pl.BlockSpec((r, d), lambda l, j: (0, 0)),
            pl.BlockSpec((None, d, tn), lambda l, j: (layer_lo + l, 0, j)),
            pl.BlockSpec((None, 1, tn), lambda l, j: (layer_lo + l, 0, j)),
        ],
        out_specs=pl.BlockSpec((None, r, tn), lambda l, j: (l, 0, j)),
        out_shape=jax.ShapeDtypeStruct((depth, r, n), F32),
        compiler_params=_params(("arbitrary", "arbitrary")),
        name="adaln",
    )(cond, ada_w, ada_b.reshape(all_layers, 1, n))


def _ln_mod_kernel(x_ref, sh_ref, sc_ref, o_ref):
    y = _layer_norm(x_ref[...])
    o_ref[...] = (y * (1.0 + sc_ref[...]) + sh_ref[...]).astype(o_ref.dtype)


def _ln_mod_call(x, mods, mod_row, k_shift, k_scale):
    bm, lm, d = x.shape
    tm = _tile(lm, 256)
    return pl.pallas_call(
        _ln_mod_kernel,
        grid=(bm, lm // tm),
        in_specs=[
            pl.BlockSpec((None, tm, d), lambda b, i: (b, i, 0)),
            _mod_spec(mod_row, k_shift, d),
            _mod_spec(mod_row, k_scale, d),
        ],
        out_specs=pl.BlockSpec((None, tm, d), lambda b, i: (b, i, 0)),
        out_shape=jax.ShapeDtypeStruct((bm, lm, d), BF16),
        compiler_params=_params(("arbitrary", "arbitrary")),
        name="ln_mod",
    )(x, mods, mods)


def _postnorm_kernel(res_ref, pre_ref, gate_ref, g_ref, b_ref, *rest, alpha, with_next):
    z = alpha * res_ref[...] + gate_ref[...] * pre_ref[...].astype(F32)
    x_new = _layer_norm(z) * g_ref[...] + b_ref[...]
    if with_next:
        sh_ref, sc_ref, o_ref, h_ref = rest
        o_ref[...] = x_new
        h_ref[...] = (_layer_norm(x_new) * (1.0 + sc_ref[...]) + sh_ref[...]).astype(h_ref.dtype)
    else:
        (o_ref,) = rest
        o_ref[...] = x_new


def _postnorm_call(res, pre, mods, mod_row, k_gate, ln_g, ln_b, alpha, next_mod=None):
    bm, lm, d = res.shape
    tm = _tile(lm, 256)
    row = pl.BlockSpec((None, tm, d), lambda b, i: (b, i, 0))
    vec = pl.BlockSpec((1, d), lambda b, i: (0, 0))
    in_specs = [row, row, _mod_spec(mod_row, k_gate, d), vec, vec]
    args = [res, pre, mods, ln_g, ln_b]
    out_specs = [row]
    out_shape = [jax.ShapeDtypeStruct((bm, lm, d), F32)]
    if next_mod is not None:
        nmods, k_shift, k_scale = next_mod
        in_specs += [_mod_spec(mod_row, k_shift, d), _mod_spec(mod_row, k_scale, d)]
        args += [nmods, nmods]
        out_specs.append(row)
        out_shape.append(jax.ShapeDtypeStruct((bm, lm, d), BF16))
    out = pl.pallas_call(
        functools.partial(_postnorm_kernel, alpha=alpha, with_next=next_mod is not None),
        grid=(bm, lm // tm),
        in_specs=in_specs,
        out_specs=out_specs,
        out_shape=out_shape,
        compiler_params=_params(("arbitrary", "arbitrary")),
        name="postnorm",
    )(*args)
    return (out[0], out[1]) if next_mod is not None else (out[0], None)


def _mm_kernel(a_ref, w_ref, *rest, nb, ni, nj, job, mod_job):
    n_side_in = (job is not None) + 3 * (mod_job is not None)
    side_in, (o_ref, *side_out) = rest[:n_side_in], rest[n_side_in:]
    step = (pl.program_id(0) * ni + pl.program_id(1)) * nj + pl.program_id(2)
    nsteps = nb * ni * nj
    if job is not None:
        _cast_step(job, step, side_in[0], side_out[0], nsteps)
    if mod_job is not None:
        _mod_step(mod_job, step, *side_in[-3:], side_out[-1], nsteps)
    o_ref[...] = _dot(a_ref[...], w_ref[...]).astype(o_ref.dtype)


def _mm_call(a, w, out_dtype, name, tm_pref=1024, tn_pref=1024, rider=None, mod_rider=None):
    ba, m, k = a.shape
    bw, _, n = w.shape
    nb = max(ba, bw)
    tm = _tile(m, tm_pref)
    tn = _tile(n, tn_pref)
    ni, nj = m // tm, n // tn
    if rider is not None and not _rider_fits(rider, nb * ni * nj):
        out = _mm_call(a, w, out_dtype, name, tm_pref, tn_pref, mod_rider=mod_rider)
        out = out if mod_rider is not None else (out,)
        return (out[0], _pad_cast_call(*rider)) + tuple(out[1:])
    rows_outer = ba == 1 and bw > 1
    n0, n1 = (ni, nb) if rows_outer else (nb, ni)

    def in_grid_order(f):
        return (lambda i, b, j: f(b, i, j)) if rows_outer else f

    a_map = (lambda b, i, j: (b, i, 0)) if ba > 1 else (lambda b, i, j: (0, i, 0))
    w_map = (lambda b, i, j: (b, 0, j)) if bw > 1 else (lambda b, i, j: (0, 0, j))
    in_specs = [pl.BlockSpec((None, tm, k), in_grid_order(a_map)),
                pl.BlockSpec((None, k, tn), in_grid_order(w_map))]
    out_specs = [pl.BlockSpec((None, tm, tn), in_grid_order(lambda b, i, j: (b, i, j)))]
    out_shape = [jax.ShapeDtypeStruct((nb, m, n), out_dtype)]
    args = [a, w]
    job = mod_job = None

    def step_of(p0, p1, j):
        return (p0 * n1 + p1) * nj + j

    if rider is not None:
        job = _attach_rider(rider, step_of, in_specs, out_specs, out_shape, args)
    if mod_rider is not None:
        mod_job, operands = mod_rider
        assert mod_job.nblk <= nb * ni * nj
        _mod_attach(mod_job, operands, step_of, in_specs, out_specs, out_shape, args)
    out = pl.pallas_call(
        functools.partial(_mm_kernel, nb=n0, ni=n1, nj=nj, job=job, mod_job=mod_job),
        grid=(n0, n1, nj),
        in_specs=in_specs,
        out_specs=out_specs,
        out_shape=out_shape,
        compiler_params=_params(("arbitrary", "arbitrary", "arbitrary")),
        name=name,
    )(*args)
    return out if len(out) > 1 else out[0]


def _mm_split_kernel(a1_ref, a2_ref, w1_ref, w2_ref, o_ref):
    acc = _dot(a1_ref[...], w1_ref[...]) + _dot(a2_ref[...], w2_ref[...])
    o_ref[...] = acc.astype(o_ref.dtype)


def _mm_split_call(a1, a2, w, out_dtype, name, tm_pref, tn_pref):
    nb, m, k1 = a1.shape
    k2 = a2.shape[2]
    n = w.shape[1]
    assert w.shape[0] == k1 + k2 and k1 % k2 == 0
    tm = _tile(m, tm_pref)
    tn = _tile(n, tn_pref)
    return pl.pallas_call(
        _mm_split_kernel,
        grid=(nb, m // tm, n // tn),
        in_specs=[
            pl.BlockSpec((None, tm, k1), lambda b, i, j: (b, i, 0)),
            pl.BlockSpec((None, tm, k2), lambda b, i, j: (b, i, 0)),
            pl.BlockSpec((k1, tn), lambda b, i, j: (0, j)),
            pl.BlockSpec((k2, tn), lambda b, i, j: (k1 // k2, j)),
        ],
        out_specs=pl.BlockSpec((None, tm, tn), lambda b, i, j: (b, i, j)),
        out_shape=jax.ShapeDtypeStruct((nb, m, n), out_dtype),
        compiler_params=_params(("arbitrary", "arbitrary", "arbitrary")),
        name=name,
    )(a1, a2, w, w)


def _mm_w32_kernel(a_ref, w_ref, *rest, ni, job):
    if job is None:
        o_ref, w_scr = rest
    else:
        rw_ref, o_ref, ro_ref, w_scr = rest
    i = pl.program_id(1)

    @pl.when(i == 0)
    def _():
        w_scr[...] = w_ref[...].astype(w_scr.dtype)

    if job is not None:
        _cast_step(job, pl.program_id(0) * ni + i, rw_ref, ro_ref)
    o_ref[...] = _dot(a_ref[...], w_scr[...]).astype(o_ref.dtype)


def _mm_w32_call(a, w, layer, col0, ncols, out_dtype, name, rider=None):
    m, k = a.shape
    tm = _tile(m, 1024)
    tn = _tile(ncols, 512)
    assert col0 % tn == 0
    j0 = col0 // tn
    ni = m // tm
    in_specs = [
        pl.BlockSpec((tm, k), lambda j, i: (i, 0)),
        pl.BlockSpec((None, k, tn), lambda j, i: (layer, 0, j0 + j)),
    ]
    out_specs = [pl.BlockSpec((tm, tn), lambda j, i: (i, j))]
    out_shape = [jax.ShapeDtypeStruct((m, ncols), out_dtype)]
    args = [a, w]
    job = None
    if rider is not None and not _rider_fits(rider, (ncols // tn) * ni):
        return _mm_w32_call(a, w, layer, col0, ncols, out_dtype, name), _pad_cast_call(*rider)
    if rider is not None:
        job = _attach_rider(rider, lambda j, i: j * ni + i, in_specs, out_specs, out_shape, args)
    out = pl.pallas_call(
        functools.partial(_mm_w32_kernel, ni=ni, job=job),
        grid=(ncols // tn, ni),
        in_specs=in_specs,
        out_specs=out_specs,
        out_shape=out_shape,
        scratch_shapes=[pltpu.VMEM((k, tn), BF16)],
        compiler_params=_params(("arbitrary", "arbitrary")),
        name=name,
    )(*args)
    return out if rider is not None else out[0]


def _chan_dft_kernel(u_ref, cs_ref, o_ref, *, gd):
    r = _dot(u_ref[...], cs_ref[...])
    o_ref[0] = r[:, :gd].astype(o_ref.dtype)
    o_ref[1] = r[:, gd:].astype(o_ref.dtype)


def _chan_dft_call(proj, cs, fw):
    bs, ls, _ = proj.shape
    gd = fw // FOURIER_GROUPS
    tm = _tile(ls, 1024)
    return pl.pallas_call(
        functools.partial(_chan_dft_kernel, gd=gd),
        grid=(bs, ls // tm, FOURIER_GROUPS),
        in_specs=[
            pl.BlockSpec((None, tm, gd), lambda b, i, g: (b, i, g)),
            pl.BlockSpec((gd, 2 * gd), lambda b, i, g: (0, 0)),
        ],
        out_specs=pl.BlockSpec((None, 2, tm, gd), lambda b, i, g: (b, 0, i, g)),
        out_shape=jax.ShapeDtypeStruct((bs, 2, ls, fw), BF16),
        compiler_params=_params(("arbitrary", "arbitrary", "arbitrary")),
        name="chan_dft",
    )(proj, cs)


def _ssm_kernel(uxf_ref, uxb_ref, ucf_ref, ucb_ref, bm_ref, cm_ref, a_ref, *rest,
                chunk, nbat, half, n_ctx, n_all, job):
    side_work = None
    if job is None:
        yf_ref, yb_ref, *scratch = rest
    else:
        rw_ref, yf_ref, yb_ref, ro_ref, *scratch = rest
        step = pl.program_id(0) * n_all + pl.program_id(1)
        side_work = functools.partial(_cast_step, job, step, rw_ref, ro_ref)
    _ssm_step(uxf_ref, uxb_ref, ucf_ref, ucb_ref, bm_ref, cm_ref, a_ref, yf_ref, yb_ref,
              *scratch, chunk=chunk, nbat=nbat, half=half, n_ctx=n_ctx, side_work=side_work)


def _ssm_step(uxf_ref, uxb_ref, ucf_ref, ucb_ref, bm_ref, cm_ref, a_ref, yf_ref, yb_ref,
              u_scr, uin_scr, s_scr, y_scr, h_scr, *, chunk, nbat, half, n_ctx, side_work):
    k = pl.program_id(0)
    c = pl.program_id(1)
    rows = SUBLANES
    hw = half // 2
    mm_rows = min(chunk * rows, SSM_MM_ROWS)
    fwd_starts = list(range(0, chunk * rows, mm_rows))
    row_starts = (fwd_starts, fwd_starts[::-1])

    @pl.when(jnp.logical_and(k == 0, c == 0))
    def _():
        uin_scr[...] = jnp.zeros_like(uin_scr)

    @pl.when(c == 0)
    def _():
        h_scr[...] = jnp.zeros_like(h_scr)

    @pl.when(c < n_ctx)
    def _():
        u_scr[0] = ucf_ref[...]
        u_scr[1] = ucb_ref[...]

    @pl.when(c >= n_ctx)
    def _():
        u_scr[0] = uxf_ref[...]
        u_scr[1] = uxb_ref[...]

    if side_work is not None:
        side_work()
    for d in range(2):
        for b in range(nbat):
            ub = u_scr[d, b].astype(F32)
            for h in range(2):
                uin_scr[d, h, pl.ds(2 * b + h, chunk, stride=rows), :] = ub[:, h * LANES:(h + 1) * LANES]
        bm = bm_ref[d]
        for r0 in row_starts[d]:
            rs = slice(r0, r0 + mm_rows)
            lhs = jnp.concatenate([uin_scr[d, 0, rs], uin_scr[d, 1, rs]], axis=1).astype(BF16)
            s_scr[d, rs] = _dot(lhs, bm)

    arf, aif = a_ref[0, 0], a_ref[0, 1]
    arb, aib = a_ref[1, 0], a_ref[1, 1]

    def body(t, carry):
        hrf, hif, hrb, hib = carry
        rowf = t * rows
        rowb = (chunk - 1 - t) * rows
        xf = s_scr[0, pl.ds(rowf, rows), :]
        xb = s_scr[1, pl.ds(rowb, rows), :]
        nrf = arf * hrf - aif * hif + xf[:, :hw]
        nif = arf * hif + aif * hrf + xf[:, hw:]
        nrb = arb * hrb - aib * hib + xb[:, :hw]
        nib = arb * hib + aib * hrb + xb[:, hw:]
        s_scr[0, pl.ds(rowf, rows), :hw] = nrf
        s_scr[0, pl.ds(rowf, rows), hw:] = nif
        s_scr[1, pl.ds(rowb, rows), :hw] = nrb
        s_scr[1, pl.ds(rowb, rows), hw:] = nib
        return nrf, nif, nrb, nib

    init = (h_scr[0, :, :hw], h_scr[0, :, hw:], h_scr[1, :, :hw], h_scr[1, :, hw:])
    carry = init
    for t in range(chunk):
        carry = body(t, carry)
    hrf, hif, hrb, hib = carry
    h_scr[0, :, :hw] = hrf
    h_scr[0, :, hw:] = hif
    h_scr[1, :, :hw] = hrb
    h_scr[1, :, hw:] = hib

    for d, y_ref in enumerate((yf_ref, yb_ref)):
        cm = cm_ref[d]
        for r0 in row_starts[d]:
            rs = slice(r0, r0 + mm_rows)
            out = _dot(s_scr[d, rs].astype(BF16), cm)
            y_scr[d, 0, rs] = out[:, :LANES]
            y_scr[d, 1, rs] = out[:, LANES:]
        for b in range(nbat):
            for h in range(2):
                piece = y_scr[d, h, pl.ds(2 * b + h, chunk, stride=rows), :]
                y_ref[b, :, h * LANES:(h + 1) * LANES] = piece.astype(y_ref.dtype)


def _ssm_call(ux, ux_col, uc, uc_col, bmat, cmat, amat, chunk, rider=None):
    nbat, seq, _ = ux.shape
    clen = uc.shape[1]
    assert 2 * nbat == SUBLANES, "the scan packs 2 column halves x batch on the 8 sublanes"
    ndir, nblk, cin, cst = bmat.shape
    assert cin == 2 * LANES, "each column half of a group block is one 128-lane input slab"
    sw = nblk * cin
    n_lat, n_ctx = seq // chunk, clen // chunk
    n_all = n_lat + n_ctx
    assert ux_col % cin == 0 and uc_col % cin == 0
    xo, co = ux_col // cin, uc_col // cin

    def lat_f(c):
        return jnp.maximum(c - n_ctx, 0)

    def lat_b(c):
        return n_lat - 1 - jnp.maximum(c - n_ctx, 0)

    def ctx_f(c):
        return jnp.minimum(c, n_ctx - 1)

    def ctx_b(c):
        return jnp.maximum(n_ctx - 1 - c, 0)

    def out_f(c):
        return jnp.where(c < n_ctx, n_lat + c, c - n_ctx)

    def out_b(c):
        return jnp.where(c < n_ctx, n_lat + n_ctx - 1 - c, n_lat - 1 - (c - n_ctx))

    ublk = (nbat, chunk, cin)
    in_specs = [
        pl.BlockSpec(ublk, lambda k, c: (0, lat_f(c), xo + k)),
        pl.BlockSpec(ublk, lambda k, c: (0, lat_b(c), xo + k)),
        pl.BlockSpec(ublk, lambda k, c: (0, ctx_f(c), co + k)),
        pl.BlockSpec(ublk, lambda k, c: (0, ctx_b(c), co + k)),
        pl.BlockSpec((ndir, None, cin, cst), lambda k, c: (0, k, 0, 0)),
        pl.BlockSpec((ndir, None, cst, cin), lambda k, c: (0, k, 0, 0)),
        pl.BlockSpec((ndir, None, 2, SUBLANES, cst // 2), lambda k, c: (0, k, 0, 0, 0)),
    ]
    out_specs = [
        pl.BlockSpec(ublk, lambda k, c: (0, out_f(c), k)),
        pl.BlockSpec(ublk, lambda k, c: (0, out_b(c), k)),
    ]
    out_shape = [jax.ShapeDtypeStruct((nbat, seq + clen, sw), BF16)] * 2
    args = [ux, ux, uc, uc, bmat, cmat, amat]
    job = None
    if rider is not None and not _rider_fits(rider, nblk * n_all):
        yf, yb = _ssm_call(ux, ux_col, uc, uc_col, bmat, cmat, amat, chunk)
        return yf, yb, _pad_cast_call(*rider)
    if rider is not None:
        job = _attach_rider(rider, lambda k, c: k * n_all + c, in_specs, out_specs, out_shape, args)
    return pl.pallas_call(
        functools.partial(_ssm_kernel, chunk=chunk, nbat=nbat, half=cst, n_ctx=n_ctx,
                          n_all=n_all, job=job),
        grid=(nblk, n_all),
        in_specs=in_specs,
        out_specs=out_specs,
        out_shape=out_shape,
        scratch_shapes=[
            pltpu.VMEM((ndir, nbat, chunk, cin), BF16),
            pltpu.VMEM((ndir, 2, SUBLANES * chunk, LANES), F32),
            pltpu.VMEM((ndir, SUBLANES * chunk, cst), F32),
            pltpu.VMEM((ndir, 2, SUBLANES * chunk, LANES), F32),
            pltpu.VMEM((ndir, SUBLANES, cst), F32),
        ],
        compiler_params=_params(("arbitrary", "arbitrary")),
        name="ssm",
    )(*args)


def _ssm_tables(a_re, a_im, log_dt, b_re, b_im, c_re, c_im):
    ndir, g, p = a_re.shape
    cg = b_re.shape[-1]
    gb = SSM_BLOCK_GROUPS
    nblk = g // gb
    hg = gb // 2
    npair = hg // 2
    assert 2 * p == LANES and g % gb == 0 and hg * cg == LANES
    lr, li = a_re.astype(F32), a_im.astype(F32)
    dt = jnp.exp(log_dt.astype(F32))[..., None]
    zr, zi = lr * dt, li * dt
    a_bar_r = jnp.exp(zr) * jnp.cos(zi)
    a_bar_i = jnp.exp(zr) * jnp.sin(zi)
    em1_r = jnp.expm1(zr) * jnp.cos(zi) - 2.0 * jnp.square(jnp.sin(0.5 * zi))
    den = lr * lr + li * li
    cf_r = ((em1_r * lr + a_bar_i * li) / den)[..., None]
    cf_i = ((a_bar_i * lr - em1_r * li) / den)[..., None]
    br, bi = b_re.astype(F32), b_im.astype(F32)
    b_bar_r = cf_r * br - cf_i * bi
    b_bar_i = cf_r * bi + cf_i * br

    lane_q = np.arange(LANES) // p
    own = (np.arange(gb)[:, None, None] % hg
           == 2 * np.arange(npair)[None, :, None] + lane_q[None, None, :])
    own_in = np.repeat(own, cg, axis=0).astype(np.float32)
    own_in = own_in.reshape(gb * cg, npair * LANES)
    own_out = np.ascontiguousarray(own_in.T)

    def in_mat(bb):
        t = jnp.swapaxes(bb.reshape(ndir, nblk, gb, p, cg), -1, -2).reshape(ndir, nblk, gb * cg, p)
        return jnp.tile(t, (1, 1, 1, 2 * npair)) * own_in

    def out_mat(cc):
        t = jnp.swapaxes(cc.reshape(ndir, nblk, gb * cg, p), -1, -2)
        return jnp.tile(t, (1, 1, 2 * npair, 1)) * own_out

    bmat = jnp.concatenate([in_mat(b_bar_r), in_mat(b_bar_i)], axis=-1).astype(BF16)
    cmat = jnp.concatenate([out_mat(c_re.astype(F32)), out_mat(-c_im.astype(F32))], axis=-2).astype(BF16)

    def decay(a):
        t = a.reshape(ndir, nblk, 2, npair * LANES)
        return jnp.tile(t, (1, 1, SUBLANES // 2, 1))
    aa = jnp.stack([decay(a_bar_r), decay(a_bar_i)], axis=2)
    return bmat, cmat, aa


def _ssm_act_kernel(u_ref, yf_ref, yb_ref, d_ref, o_ref):
    y = d_ref[...] * u_ref[...].astype(F32) + yf_ref[...].astype(F32) + yb_ref[...].astype(F32)
    o_ref[...] = jax.nn.gelu(y).astype(o_ref.dtype)


def _ssm_act_call(proj, yf, yb, row_off, ssm_d, fw, sw):
    bs, ls, _ = proj.shape
    tm = _tile(ls, 512)
    assert fw % sw == 0 and row_off % tm == 0
    u_blk, r_blk = fw // sw, row_off // tm
    return pl.pallas_call(
        _ssm_act_kernel,
        grid=(bs, ls // tm),
        in_specs=[
            pl.BlockSpec((None, tm, sw), lambda b, i: (b, i, u_blk)),
            pl.BlockSpec((None, tm, sw), lambda b, i: (b, i + r_blk, 0)),
            pl.BlockSpec((None, tm, sw), lambda b, i: (b, i + r_blk, 0)),
            pl.BlockSpec((1, sw), lambda b, i: (0, 0)),
        ],
        out_specs=pl.BlockSpec((None, tm, sw), lambda b, i: (b, i, 0)),
        out_shape=jax.ShapeDtypeStruct((bs, ls, sw), BF16),
        compiler_params=_params(("arbitrary", "arbitrary")),
        name="ssm_act",
    )(proj, yf, yb, ssm_d)


def _glu_kernel(a_ref, wv_ref, wg_ref, gf_ref, gs_ref, fo_ref, *rest, nb, ni, job):
    if job is None:
        (o_ref,) = rest
    else:
        rw_ref, o_ref, ro_ref = rest
        step = (pl.program_id(0) * nb + pl.program_id(1)) * ni + pl.program_id(2)
        _cast_step(job, step, rw_ref, ro_ref)
    a = a_ref[...]
    tn = o_ref.shape[1]
    sub = min(tn, GLU_SUBTILE)
    for n0 in range(0, tn, sub):
        cols = slice(n0, n0 + sub)
        y_s = (_dot(a, wv_ref[:, cols]) * jax.nn.sigmoid(_dot(a, wg_ref[:, cols]))).astype(BF16)
        o_ref[:, cols] = (jax.nn.sigmoid(gf_ref[:, cols]) * fo_ref[:, cols]
                          + jax.nn.sigmoid(gs_ref[:, cols]) * y_s)


def _glu_call(act, proj, glu_w, y_f, fw, sw, d, rider=None):
    bs, ls, _ = act.shape
    tm = _tile(ls, 1024)
    tn = _tile(d, 1024)
    assert (fw + sw) % tn == 0
    gf_blk = (fw + sw) // tn
    gs_blk = (fw + sw + d) // tn
    ni, nj = ls // tm, d // tn
    if rider is not None and not _rider_fits(rider, bs * ni * nj):
        return _glu_call(act, proj, glu_w, y_f, fw, sw, d), _pad_cast_call(*rider)
    in_specs = [
        pl.BlockSpec((None, tm, sw), lambda j, b, i: (b, i, 0)),
        pl.BlockSpec((sw, tn), lambda j, b, i: (0, j)),
        pl.BlockSpec((sw, tn), lambda j, b, i: (0, nj + j)),
        pl.BlockSpec((None, tm, tn), lambda j, b, i: (b, i, gf_blk + j)),
        pl.BlockSpec((None, tm, tn), lambda j, b, i: (b, i, gs_blk + j)),
        pl.BlockSpec((None, tm, tn), lambda j, b, i: (b, i, j)),
    ]
    out_specs = [pl.BlockSpec((None, tm, tn), lambda j, b, i: (b, i, j))]
    out_shape = [jax.ShapeDtypeStruct((bs, ls, d), BF16)]
    args = [act, glu_w, glu_w, proj, proj, y_f]
    job = None
    if rider is not None:
        job = _attach_rider(rider, lambda j, b, i: (j * bs + b) * ni + i,
                            in_specs, out_specs, out_shape, args)
    out = pl.pallas_call(
        functools.partial(_glu_kernel, nb=bs, ni=ni, job=job),
        grid=(nj, bs, ni),
        in_specs=in_specs,
        out_specs=out_specs,
        out_shape=out_shape,
        compiler_params=_params(("arbitrary", "arbitrary", "arbitrary")),
        name="glu_merge",
    )(*args)
    return out if rider is not None else out[0]


def _ffn1_kernel(h_ref, wu_ref, wv_ref, cw_ref, cb_ref, *rest, period, nb, ni, nj, job):
    if job is None:
        (o_ref,) = rest
    else:
        rw_ref, o_ref, ro_ref = rest
        step = (pl.program_id(0) * ni + pl.program_id(1)) * nj + pl.program_id(2)
        _cast_step(job, step, rw_ref, ro_ref, nb * ni * nj)
    h = h_ref[...]
    u = _dot(h, wu_ref[...])
    v = _dot(h, wv_ref[...])
    tm = u.shape[0]
    pos = lax.broadcasted_iota(jnp.int32, u.shape, 0) % period
    prev = jnp.where(pos == 0, 0.0, pltpu.roll(u, 1, axis=0))
    nxt = jnp.where(pos == period - 1, 0.0, pltpu.roll(u, tm - 1, axis=0))
    cw = cw_ref[...]
    conv = cb_ref[...] + cw[0:1] * prev + cw[1:2] * u + cw[2:3] * nxt
    o_ref[...] = (jax.nn.gelu(conv) * v).astype(o_ref.dtype)


def _ffn1_call(h, w12, conv_w, conv_b, period, rider=None, tn=None, j0=0, nj=None):
    bm, lm, d = h.shape
    ffn = w12.shape[1] // 2
    tm = _tile(lm, 1024)
    tn = _tile(ffn, 512) if tn is None else tn
    nj = ffn // tn if nj is None else nj
    ni = lm // tm
    v0 = ffn // tn + j0
    assert tm % period == 0 and ffn % tn == 0
    if rider is not None and not _rider_fits(rider, bm * ni * nj):
        return _ffn1_call(h, w12, conv_w, conv_b, period, None, tn, j0, nj), _pad_cast_call(*rider)
    in_specs = [
        pl.BlockSpec((None, tm, d), lambda b, i, j: (b, i, 0)),
        pl.BlockSpec((d, tn), lambda b, i, j: (0, j0 + j)),
        pl.BlockSpec((d, tn), lambda b, i, j: (0, v0 + j)),
        pl.BlockSpec((3, tn), lambda b, i, j: (0, j0 + j)),
        pl.BlockSpec((1, tn), lambda b, i, j: (0, j0 + j)),
    ]
    out_specs = [pl.BlockSpec((None, tm, tn), lambda b, i, j: (b, i, j))]
    out_shape = [jax.ShapeDtypeStruct((bm, lm, nj * tn), BF16)]
    args = [h, w12, w12, conv_w, conv_b]
    job = None
    if rider is not None:
        job = _attach_rider(rider, lambda b, i, j: (b * ni + i) * nj + j,
                            in_specs, out_specs, out_shape, args)
    out = pl.pallas_call(
        functools.partial(_ffn1_kernel, period=period, nb=bm, ni=ni, nj=nj, job=job),
        grid=(bm, ni, nj),
        in_specs=in_specs,
        out_specs=out_specs,
        out_shape=out_shape,
        compiler_params=_params(("arbitrary", "arbitrary", "arbitrary")),
        name="ffn1",
    )(*args)
    return out if rider is not None else out[0]


def _dft_cos_sin(n):
    idx = np.arange(n, dtype=np.int64)
    ang = (2.0 * math.pi / n) * ((idx[:, None] * idx[None, :]) % n)
    scale = 1.0 / math.sqrt(n)
    return np.cos(ang) * scale, np.sin(ang) * scale


def _bf16_const(a):
    return jnp.asarray(np.asarray(a, np.float32).astype(ml_dtypes.bfloat16))


def kernel(x, c, ctx, c_ctx, ada_w, ada_b, w_in, fourier_w, ssm_a_re, ssm_a_im, ssm_log_dt,
           ssm_b_re, ssm_b_im, ssm_c_re, ssm_c_im, ssm_d, glu_w, w_out, ln1_g, ln1_b,
           ffn_w12, ffn_conv_w, ffn_conv_b, ffn_w2, ln2_g, ln2_b):
    nb, seq, d = x.shape
    clen = ctx.shape[1]
    depth = ada_w.shape[0]
    fw = fourier_w.shape[1]
    sw = ssm_d.shape[1]
    ffn = ffn_conv_b.shape[1]
    alpha = (2.0 * depth) ** 0.25
    gd = fw // FOURIER_GROUPS
    chunk = clen
    assert seq % chunk == 0 and seq % GRID_W == 0
    ffn_pad = FFN_TILE // 2 if (ffn % FFN_TILE == FFN_TILE // 2) else 0

    ctx_row = nb
    n_rows = -(-(nb + 1) // SUBLANES) * SUBLANES
    cond = jnp.zeros((n_rows, d), F32).at[:nb].set(c).at[ctx_row].set(c_ctx)
    ride_mods = ffn_pad > 0
    first_mods = _adaln_call(cond, ada_w, ada_b, 0, 1 if ride_mods else depth)
    mods_list = [m.reshape(n_rows, 1, N_MOD * d) for m in first_mods]

    cos_c, sin_c = _dft_cos_sin(gd)
    cs_chan = _bf16_const(np.concatenate([cos_c, sin_c], axis=1))

    def pos_dft_matrix(n):
        cos_l, sin_l = _dft_cos_sin(n)
        return _bf16_const(np.concatenate([cos_l, -sin_l], axis=1))[None]

    csl_x = pos_dft_matrix(seq)
    csl_c = pos_dft_matrix(clen)

    def token_mix_tail(proj, csl, yf, yb, row_off, lw):
        ls = proj.shape[1]
        ab = _chan_dft_call(proj, cs_chan, fw)
        ab = ab.reshape(nb, 2 * ls, fw)
        if "fourier_w" not in lw:
            f, lw["fourier_w"] = _mm_call(csl, ab, BF16, "pos_dft", rider=lw.pop("ride_fourier_w"))
        else:
            f = _mm_call(csl, ab, BF16, "pos_dft")
        f = f.reshape(1, nb * ls, fw)
        if "glu_w" not in lw:
            y_f, lw["glu_w"] = _mm_call(f, lw["fourier_w"][None], BF16, "fourier_out",
                                        rider=lw.pop("ride_glu_w"))
        else:
            y_f = _mm_call(f, lw["fourier_w"][None], BF16, "fourier_out")
        y_f = y_f.reshape(nb, ls, d)
        act = _ssm_act_call(proj, yf, yb, row_off, lw["ssm_d"], fw, sw)
        if "w_out" not in lw:
            merged, lw["w_out"] = _glu_call(act, proj, lw["glu_w"], y_f, fw, sw, d,
                                            rider=lw.pop("ride_w_out"))
        else:
            merged = _glu_call(act, proj, lw["glu_w"], y_f, fw, sw, d)
        merged = merged.reshape(1, nb * ls, d)
        if "ride_next_w_in" in lw:
            tn_ride = 512
            nsteps = (nb * ls // _tile(nb * ls, 1024)) * (d // _tile(d, tn_ride))
            mod_job = _mod_job(ada_w, lw["layer"] + 1, nsteps)
            mod_rider = None if mod_job is None else (mod_job, (cond, ada_w, ada_b))
            res = _mm_call(merged, lw["w_out"][None], BF16, "w_out", tn_pref=tn_ride,
                           rider=lw.pop("ride_next_w_in"), mod_rider=mod_rider)
            out, nxt_w["w_in"] = res[0], res[1]
            if mod_rider is not None:
                nxt_w["mods"] = res[2]
        else:
            out = _mm_call(merged, lw["w_out"][None], BF16, "w_out")
        return out.reshape(nb, ls, d)

    def conv_ffn(h, lw, period):
        ffn1 = functools.partial(_ffn1_call, h, lw["w12"], lw["conv_w"], lw["conv_b"], period)
        rider = lw.pop("ride_next_w12", None)
        if not ffn_pad:
            act = ffn1(rider)
            if rider is not None:
                act, nxt_w["w12"] = act
            return _mm_call(act, lw["w2"][None], BF16, "ffn2", tm_pref=512, tn_pref=512)
        n_main = ffn // FFN_TILE
        act = ffn1(rider, FFN_TILE, 0, n_main)
        if rider is not None:
            act, nxt_w["w12"] = act
        rest = ffn1(None, ffn_pad, n_main * (FFN_TILE // ffn_pad), 1)
        return _mm_split_call(act, rest, lw["w2"], BF16, "ffn2", 512, 512)

    xs = x
    cs = ctx.reshape(1, nb * clen, d)
    nxt_w = {}
    hx = _ln_mod_call(xs, mods_list[0], None, 0, 1)
    hc = _ln_mod_call(cs, mods_list[0], ctx_row, 0, 1)
    for i in range(depth):
        last = i == depth - 1
        mods = mods_list[i]
        lw = {
            "layer": i,
            "ssm_d": ssm_d[i].reshape(1, sw),
            "ride_fourier_w": (_rows_cast(fourier_w, i, _tile(fw, CAST_ROWS // 2)), fourier_w),
            "ride_glu_w": (_rows_cast(glu_w, i, _tile(sw, CAST_ROWS // 4)), glu_w),
            "ride_w_out": (_rows_cast(w_out, i, _tile(d, CAST_ROWS // 2)), w_out),
        }
        cur_w, nxt_w = nxt_w, {}
        ride_w12 = ride_w2 = None
        if ffn_pad:
            if "w12" in cur_w:
                lw["w12"] = cur_w["w12"]
            else:
                ride_w12 = (_pad_halves_cast(ffn_w12, i, ffn_pad, HALVES_CAST_COLS), ffn_w12)
            ride_w2 = (_rows_cast(ffn_w2, i, ffn_pad), ffn_w2)
            if not last:
                lw["ride_next_w_in"] = (_rows_cast(w_in, i + 1, _tile(d, CAST_ROWS // 4)), w_in)
                lw["ride_next_w12"] = (_pad_halves_cast(ffn_w12, i + 1, ffn_pad, ffn_pad), ffn_w12)
        else:
            lw["w12"] = _cast_call(ffn_w12, i)
            lw["w2"] = _cast_call(ffn_w2, i)
        lw["conv_w"] = jnp.pad(ffn_conv_w[i], ((0, 0), (0, ffn_pad)))
        lw["conv_b"] = jnp.pad(ffn_conv_b[i].reshape(1, ffn), ((0, 0), (0, ffn_pad)))
        g1, b1 = ln1_g[i].reshape(1, d), ln1_b[i].reshape(1, d)
        g2, b2 = ln2_g[i].reshape(1, d), ln2_b[i].reshape(1, d)
        bmat, cmat, amat = _ssm_tables(ssm_a_re[i], ssm_a_im[i], ssm_log_dt[i], ssm_b_re[i],
                                       ssm_b_im[i], ssm_c_re[i], ssm_c_im[i])

        n_in = w_in.shape[2]
        hc2d = hc.reshape(nb * clen, d)
        c_col = 0 if last else fw
        if "w_in" in cur_w:
            w_in_b = cur_w["w_in"]
            proj_x = _mm_call(hx.reshape(1, nb * seq, d), w_in_b[None], BF16, "w_in", rider=ride_w12)
            w_in_c = w_in_b[None, :, fw:fw + sw] if last else w_in_b[None]
            proj_c = _mm_call(hc2d[None], w_in_c, BF16, "w_in_ctx")
        else:
            proj_x = _mm_w32_call(hx.reshape(nb * seq, d), w_in, i, 0, n_in, BF16, "w_in", ride_w12)
            if last:
                proj_c = _mm_w32_call(hc2d, w_in, i, fw, sw, BF16, "w_in_ctx")
            else:
                proj_c = _mm_w32_call(hc2d, w_in, i, 0, n_in, BF16, "w_in_ctx")
        if ride_w12 is not None:
            proj_x, lw["w12"] = proj_x
        proj_x = proj_x.reshape(nb, seq, n_in)
        proj_c = proj_c.reshape(nb, clen, -1)
        ssm_out = _ssm_call(proj_x, fw, proj_c, c_col, bmat, cmat, amat, chunk, ride_w2)
        yf, yb = ssm_out[0], ssm_out[1]
        if ride_w2 is not None:
            lw["w2"] = ssm_out[2]

        out_x = token_mix_tail(proj_x, csl_x, yf, yb, 0, lw)
        if not last and len(mods_list) == i + 1:
            if "mods" in nxt_w:
                mods_list.append(nxt_w["mods"].reshape(n_rows, 1, N_MOD * d))
            else:
                mods_list.append(_adaln_call(cond, ada_w, ada_b, i + 1, 1).reshape(n_rows, 1, N_MOD * d))
        nxt = None if last else (mods_list[i + 1], 0, 1)
        x1, h2 = _postnorm_call(xs, out_x, mods, None, 2, g1, b1, alpha, (mods, 3, 4))
        ffn_x = conv_ffn(h2, lw, GRID_W)
        xs, hx = _postnorm_call(x1, ffn_x, mods, None, 5, g2, b2, alpha, nxt)

        if not last:
            out_c = token_mix_tail(proj_c, csl_c, yf, yb, seq, lw).reshape(1, nb * clen, d)
            c1, hc2 = _postnorm_call(cs, out_c, mods, ctx_row, 2, g1, b1, alpha, (mods, 3, 4))
            ffn_c = conv_ffn(hc2, lw, clen)
            cs, hc = _postnorm_call(c1, ffn_c, mods, ctx_row, 5, g2, b2, alpha, nxt)
    return xs
```
